```python
import jax, jax.numpy as jnp
from jax import lax
import numpy as np

D_MODEL = 1024
BATCH = 8
SEQ = 4096
DEPTH = 1

N_META = 16
SB_BLOCK = 128
PAD = SB_BLOCK - N_META
SB_HEADS = 8
SB_HEAD_DIM = 64
SB_WIDTH = SB_HEADS * SB_HEAD_DIM
DN_HEADS = 8
DN_HEAD_DIM = 64
DN_WIDTH = DN_HEADS * DN_HEAD_DIM
DN_CONV = 4
DN_CHUNK = 64
PEER_HEADS = 8
PEER_N_KEYS = 128
PEER_N_EXPERTS = PEER_N_KEYS * PEER_N_KEYS
PEER_QUERY_DIM = 256
PEER_HALF = PEER_QUERY_DIM // 2
PEER_TOPK = 16
PEER_TOKEN_BLOCK = 256
EPS = 1e-6

IN_SPLITS = [SB_WIDTH, SB_WIDTH, SB_WIDTH, 3 * DN_WIDTH, DN_WIDTH, DN_HEADS, DN_HEADS, D_MODEL, D_MODEL]
IN_COLS = sum(IN_SPLITS)

kernel_name = "hybrid_stickbreak_gdn_peer_block"


def rms_norm(x, g):
    xf = x.astype(jnp.float32)
    y = xf * lax.rsqrt(jnp.mean(xf * xf, axis=-1, keepdims=True) + EPS)
    return (y * g.astype(jnp.float32)).astype(x.dtype)


def l2_norm(x):
    xf = x.astype(jnp.float32)
    return (xf * lax.rsqrt(jnp.sum(xf * xf, axis=-1, keepdims=True) + EPS)).astype(x.dtype)


def causal_depthwise_conv(x, w):
    K, C = w.shape
    return lax.conv_general_dilated(
        x, w[:, None, :].astype(x.dtype), window_strides=(1,), padding=[(K - 1, 0)],
        dimension_numbers=("NWC", "WIO", "NWC"), feature_group_count=C)


def stick_breaking_attention(q, k, v, key_valid):
    B, H, Lp, d = q.shape
    nb = Lp // SB_BLOCK
    scale = d ** -0.5
    qb = q.reshape(B, H, nb, SB_BLOCK, d).transpose(2, 0, 1, 3, 4)
    kpos = jnp.arange(Lp)

    def one_block(args):
        q_blk, i = args
        t = i * SB_BLOCK + jnp.arange(SB_BLOCK)
        z = jnp.einsum("bhqd,bhkd->bhqk", q_blk, k).astype(jnp.float32) * scale
        visible = (kpos[None, :] < t[:, None]) & key_valid[None, :]
        log_beta = jax.nn.log_sigmoid(z)
        log_one_minus = jnp.where(visible, jax.nn.log_sigmoid(-z), 0.0)
        log_remaining = lax.cumsum(log_one_minus, axis=3, reverse=True) - log_one_minus
        a = jnp.where(visible, jnp.exp(log_beta + log_remaining), 0.0)
        return jnp.einsum("bhqk,bhkd->bhqd", a.astype(v.dtype), v)

    out = lax.map(one_block, (qb, jnp.arange(nb)))
    return out.transpose(1, 2, 0, 3, 4).reshape(B, H, Lp, d)


def gated_delta_rule_chunked(q, k, v, beta, g):
    out_dtype = v.dtype
    f32 = jnp.float32
    B, H, Lp, dk = q.shape
    dv = v.shape[-1]
    C = DN_CHUNK
    n = Lp // C

    def chunk(a):
        return a.astype(f32).reshape(B, H, n, C, *a.shape[3:])

    q = chunk(q) * (dk ** -0.5)
    k, v, beta = chunk(k), chunk(v), chunk(beta)
    g = jnp.cumsum(chunk(g), axis=-1)
    idx = jnp.arange(C)
    lower_incl = idx[:, None] >= idx[None, :]
    strict_lower = idx[:, None] > idx[None, :]
    decay = jnp.exp(jnp.where(lower_incl, g[..., :, None] - g[..., None, :], -jnp.inf))
    k_beta = k * beta[..., None]
    v_beta = v * beta[..., None]
    m = jnp.where(strict_lower, jnp.einsum("bhncd,bhnsd->bhncs", k_beta, k) * decay, 0.0)
    eye = jnp.eye(C, dtype=f32)
    t_inv = lax.linalg.triangular_solve(eye + m, jnp.broadcast_to(eye, m.shape),
                                        left_side=True, lower=True, unit_diagonal=True)
    u = jnp.einsum("bhncs,bhnsd->bhncd", t_inv, v_beta)
    w = jnp.einsum("bhncs,bhnsd->bhncd", t_inv, k_beta * jnp.exp(g)[..., None])
    attn_intra = jnp.einsum("bhncd,bhnsd->bhncs", q, k) * decay
    q_decay = q * jnp.exp(g)[..., None]
    k_tail = k * jnp.exp(g[..., -1:] - g)[..., None]
    g_last = jnp.exp(g[..., -1])

    def step(S, inp):
        u_i, w_i, a_i, qd_i, kt_i, gl_i = inp
        v_new = u_i - jnp.einsum("bhcd,bhde->bhce", w_i, S)
        o = jnp.einsum("bhcd,bhde->bhce", qd_i, S) + jnp.einsum("bhcs,bhse->bhce", a_i, v_new)
        S = S * gl_i[..., None, None] + jnp.einsum("bhcd,bhce->bhde", kt_i, v_new)
        return S, o

    xs = (jnp.moveaxis(u, 2, 0), jnp.moveaxis(w, 2, 0), jnp.moveaxis(attn_intra, 2, 0),
          jnp.moveaxis(q_decay, 2, 0), jnp.moveaxis(k_tail, 2, 0), jnp.moveaxis(g_last, 2, 0))
    S0 = jnp.zeros((B, H, dk, dv), f32)
    _, o = lax.scan(step, S0, xs)
    return jnp.moveaxis(o, 0, 2).reshape(B, H, Lp, dv).astype(out_dtype)


def peer_ffn(h, w_q, sub_keys, u_tab, v_tab):
    B, L, D = h.shape
    T = B * L
    hf = h.reshape(T, D)
    q = (hf @ w_q).reshape(T, PEER_HEADS, 2, PEER_HALF)
    s = jnp.einsum("thpc,pkc->thpk", q, sub_keys.astype(q.dtype)).astype(jnp.float32)
    s1, i1 = lax.top_k(s[:, :, 0], PEER_TOPK)
    s2, i2 = lax.top_k(s[:, :, 1], PEER_TOPK)
    cand_s = (s1[..., :, None] + s2[..., None, :]).reshape(T, PEER_HEADS, PEER_TOPK * PEER_TOPK)
    cand_i = (i1[..., :, None] * PEER_N_KEYS + i2[..., None, :]).reshape(T, PEER_HEADS, PEER_TOPK * PEER_TOPK)
    top_s, pos = lax.top_k(cand_s, PEER_TOPK)
    expert_idx = jnp.take_along_axis(cand_i, pos, axis=-1)
    gate = jax.nn.softmax(top_s, axis=-1)

    HK = PEER_HEADS * PEER_TOPK
    nblk = -(-T // PEER_TOKEN_BLOCK)
    pad = nblk * PEER_TOKEN_BLOCK - T
    hp = jnp.pad(hf, ((0, pad), (0, 0))).reshape(nblk, PEER_TOKEN_BLOCK, D)
    ip = jnp.pad(expert_idx.reshape(T, HK), ((0, pad), (0, 0))).reshape(nblk, PEER_TOKEN_BLOCK, HK)
    gp = jnp.pad(gate.reshape(T, HK).astype(h.dtype), ((0, pad), (0, 0))).reshape(nblk, PEER_TOKEN_BLOCK, HK)

    def one_block(args):
        xb, ib, gb = args
        u = u_tab[ib]
        act = jax.nn.gelu(jnp.einsum("td,ted->te", xb, u), approximate=False)
        return jnp.einsum("te,ted->td", act * gb, v_tab[ib])

    y = lax.map(one_block, (hp, ip, gp)).reshape(nblk * PEER_TOKEN_BLOCK, D)[:T]
    return y.reshape(B, L, D)


def hybrid_layer(h, norm1_g, w_in, sb_q_norm_g, sb_k_norm_g, dn_conv_w, dn_a_log, dn_dt_bias,
                 dn_out_norm_g, w_sb_out, w_dn_out, w_o, norm2_g, peer_w_q, peer_sub_keys, peer_u, peer_v):
    B, L, D = h.shape
    f32 = jnp.float32
    xn = rms_norm(h, norm1_g)
    proj = xn @ w_in
    cuts = [int(c) for c in np.cumsum(IN_SPLITS)[:-1]]
    sb_q, sb_k, sb_v, dn_qkv, dn_z, dn_b, dn_a, gate_sb, gate_dn = jnp.split(proj, cuts, axis=-1)

    def heads(t, n_heads):
        return t.reshape(B, L, n_heads, -1).transpose(0, 2, 1, 3)

    def pad4(t):
        return jnp.pad(t, ((0, 0), (0, 0), (PAD, 0), (0, 0)))

    def pad3(t):
        return jnp.pad(t, ((0, 0), (0, 0), (PAD, 0)))

    q = rms_norm(heads(sb_q, SB_HEADS), sb_q_norm_g)
    k = rms_norm(heads(sb_k, SB_HEADS), sb_k_norm_g)
    v = heads(sb_v, SB_HEADS)
    key_valid = jnp.arange(L + PAD) >= PAD
    o_sb = stick_breaking_attention(pad4(q), pad4(k), pad4(v), key_valid)[:, :, PAD:]
    o_sb = o_sb.transpose(0, 2, 1, 3).reshape(B, L, SB_WIDTH)

    qkv = jax.nn.silu(causal_depthwise_conv(dn_qkv, dn_conv_w))
    dq, dk, dv = jnp.split(qkv, 3, axis=-1)
    dq = l2_norm(heads(dq, DN_HEADS))
    dk = l2_norm(heads(dk, DN_HEADS))
    dv = heads(dv, DN_HEADS)
    beta = jax.nn.sigmoid(dn_b.astype(f32)).transpose(0, 2, 1)
    g = (-jnp.exp(dn_a_log.astype(f32)) *
         jax.nn.softplus(dn_a.astype(f32) + dn_dt_bias.astype(f32))).transpose(0, 2, 1)
    o_dn = gated_delta_rule_chunked(pad4(dq), pad4(dk), pad4(dv), pad3(beta), pad3(g))[:, :, PAD:]
    o_dn = rms_norm(o_dn.transpose(0, 2, 1, 3), dn_out_norm_g) * jax.nn.silu(
        dn_z.reshape(B, L, DN_HEADS, DN_HEAD_DIM))
    o_dn = o_dn.reshape(B, L, DN_WIDTH)

    mix = jax.nn.sigmoid(gate_sb) * (o_sb @ w_sb_out) + jax.nn.sigmoid(gate_dn) * (o_dn @ w_dn_out)
    h = h + mix @ w_o

    h = h + peer_ffn(rms_norm(h, norm2_g), peer_w_q, peer_sub_keys, peer_u, peer_v)
    return h


def setup_inputs(seed: int = 0) -> dict:
    key = jax.random.key(seed)
    ks = jax.random.split(key, 20)
    f32 = jnp.float32

    def nrm(k, shape, s):
        return jax.random.normal(k, shape, f32) * s

    dt = jnp.exp(jax.random.uniform(ks[8], (DEPTH, DN_HEADS), f32, np.log(1e-3), np.log(1e-1)))
    return {
        "x": nrm(ks[0], (BATCH, SEQ, D_MODEL), 1.0),
        "meta_tokens": nrm(ks[1], (N_META, D_MODEL), 1.0),
        "norm1_g": 1.0 + nrm(ks[2], (DEPTH, D_MODEL), 0.02),
        "w_in": nrm(ks[3], (DEPTH, D_MODEL, IN_COLS), D_MODEL ** -0.5),
        "sb_q_norm_g": 1.0 + nrm(ks[4], (DEPTH, SB_HEAD_DIM), 0.02),
        "sb_k_norm_g": 1.0 + nrm(ks[5], (DEPTH, SB_HEAD_DIM), 0.02),
        "dn_conv_w": nrm(ks[6], (DEPTH, DN_CONV, 3 * DN_WIDTH), DN_CONV ** -0.5),
        "dn_a_log": jnp.log(jax.random.uniform(ks[7], (DEPTH, DN_HEADS), f32, 1.0, 16.0)),
        "dn_dt_bias": dt + jnp.log(-jnp.expm1(-dt)),
        "dn_out_norm_g": 1.0 + nrm(ks[9], (DEPTH, DN_HEAD_DIM), 0.02),
        "w_sb_out": nrm(ks[10], (DEPTH, SB_WIDTH, D_MODEL), SB_WIDTH ** -0.5),
        "w_dn_out": nrm(ks[11], (DEPTH, DN_WIDTH, D_MODEL), DN_WIDTH ** -0.5),
        "w_o": nrm(ks[12], (DEPTH, D_MODEL, D_MODEL), D_MODEL ** -0.5),
        "norm2_g": 1.0 + nrm(ks[13], (DEPTH, D_MODEL), 0.02),
        "peer_w_q": nrm(ks[14], (DEPTH, D_MODEL, PEER_HEADS * PEER_QUERY_DIM), D_MODEL ** -0.5),
        "peer_sub_keys": nrm(ks[15], (DEPTH, 2, PEER_N_KEYS, PEER_HALF), PEER_HALF ** -0.5),
        "peer_u": nrm(ks[16], (DEPTH, PEER_N_EXPERTS, D_MODEL), D_MODEL ** -0.5),
        "peer_v": nrm(ks[17], (DEPTH, PEER_N_EXPERTS, D_MODEL), 0.5),
    }


def reference(x, meta_tokens, norm1_g, w_in, sb_q_norm_g, sb_k_norm_g, dn_conv_w, dn_a_log, dn_dt_bias,
              dn_out_norm_g, w_sb_out, w_dn_out, w_o, norm2_g, peer_w_q, peer_sub_keys, peer_u, peer_v):
    B = x.shape[0]
    meta = jnp.broadcast_to(meta_tokens.astype(x.dtype)[None], (B, N_META, x.shape[-1]))
    h = jnp.concatenate([meta, x], axis=1)
    for l in range(DEPTH):
        h = hybrid_layer(h, norm1_g[l], w_in[l], sb_q_norm_g[l], sb_k_norm_g[l], dn_conv_w[l],
                         dn_a_log[l], dn_dt_bias[l], dn_out_norm_g[l], w_sb_out[l], w_dn_out[l], w_o[l],
                         norm2_g[l], peer_w_q[l], peer_sub_keys[l], peer_u[l], peer_v[l])
    return h[:, N_META:]
```

```python
import functools

import numpy as np
import jax
import jax.numpy as jnp
from jax import lax
from jax.experimental import pallas as pl
from jax.experimental.pallas import tpu as pltpu

F32 = jnp.float32
BF16 = jnp.bfloat16

D_MODEL = 1024
N_META = 16
BLK = 128
PAD = BLK - N_META
HEADS = 8
HEAD_DIM = 64
WIDTH = HEADS * HEAD_DIM
CHUNK = 64
PEER_HEADS = 8
PEER_KEYS = 128
PEER_TOPK = 16
PEER_HK = PEER_HEADS * PEER_TOPK
EPS = 1e-6

C_SB = 0
C_DN = 3 * WIDTH
C_Z = C_DN + 3 * WIDTH
C_GATE = C_Z + WIDTH
C_BA = C_GATE + 2 * D_MODEL
C_END = C_BA + 128

ROWS_IN = 256
ROWS_OUT = 256
PEER_TB = 8
VMEM_LIMIT = 56 * 1024 * 1024


def _dot(a, b):
    return jnp.dot(a, b, preferred_element_type=F32)


def _dot_nt(a, b):
    return lax.dot_general(a, b, (((1,), (1,)), ((), ())), preferred_element_type=F32)


def _dot_tn(a, b):
    return lax.dot_general(a, b, (((0,), (0,)), ((), ())), preferred_element_type=F32)


def _split(a):
    hi = a.astype(BF16)
    lo = (a - hi.astype(F32)).astype(BF16)
    return hi, lo


def _dot_xr(a, b_exact):
    hi, lo = _split(a)
    return _dot(hi, b_exact) + _dot(lo, b_exact)


def _dot_xl(a_exact, b):
    hi, lo = _split(b)
    return _dot(a_exact, hi) + _dot(a_exact, lo)


def _dot3(a, b):
    ah, al = _split(a)
    bh, bl = _split(b)
    return _dot(ah, bh) + (_dot(ah, bl) + _dot(al, bh))


def _sigmoid(x):
    return 1.0 / (1.0 + jnp.exp(-x))


def _softplus(x):
    return jnp.maximum(x, 0.0) + jnp.log1p(jnp.exp(-jnp.abs(x)))


def _iota(shape, dim):
    return lax.broadcasted_iota(jnp.int32, shape, dim)


def _inproj_kernel(x_ref, g1_ref, w_ref, ph_ref, gq_ref, gk_ref, acoef_ref, dtb_ref,
                   sb_ref, dn_ref, z_ref, gate_ref, bg_ref):
    x = x_ref[...]
    ms = jnp.mean(x * x, axis=-1, keepdims=True)
    xn = (x * lax.rsqrt(ms + EPS) * g1_ref[...]).astype(BF16)

    def proj(c0, c1):
        return _dot(xn, w_ref[:, c0:c1])

    def head_norm(t, g):
        msh = _dot((t * t).astype(BF16), ph_ref[...])
        return (t * lax.rsqrt(msh + EPS) * g).astype(BF16)

    sb_ref[:, 0:WIDTH] = head_norm(proj(C_SB, C_SB + WIDTH), gq_ref[...])
    sb_ref[:, WIDTH:2 * WIDTH] = head_norm(proj(C_SB + WIDTH, C_SB + 2 * WIDTH), gk_ref[...])
    sb_ref[:, 2 * WIDTH:3 * WIDTH] = proj(C_SB + 2 * WIDTH, C_SB + 3 * WIDTH).astype(BF16)
    dn_ref[...] = proj(C_DN, C_DN + 3 * WIDTH)
    z_ref[...] = proj(C_Z, C_Z + WIDTH)
    gate_ref[...] = _sigmoid(proj(C_GATE, C_GATE + 2 * D_MODEL)).astype(BF16)
    ba = proj(C_BA, C_END)
    lane = _iota(ba.shape, 1)
    bg_ref[...] = jnp.where(lane < HEADS, _sigmoid(ba), acoef_ref[...] * _softplus(ba + dtb_ref[...]))


def _inproj(hp, g1, w_all, ph, gq, gk, acoef, dtb):
    n = hp.shape[0]
    rb = ROWS_IN
    const = lambda shape: pl.BlockSpec(shape, lambda i: (0, 0))
    row = lambda c: pl.BlockSpec((rb, c), lambda i: (i, 0))
    return pl.pallas_call(
        _inproj_kernel,
        grid=(n // rb,),
        in_specs=[row(D_MODEL), const((1, D_MODEL)), const((D_MODEL, C_END)), const((WIDTH, WIDTH)),
                  const((1, WIDTH)), const((1, WIDTH)), const((1, 128)), const((1, 128))],
        out_specs=[row(3 * WIDTH), row(3 * WIDTH), row(WIDTH), row(2 * D_MODEL), row(128)],
        out_shape=[jax.ShapeDtypeStruct((n, 3 * WIDTH), BF16),
                   jax.ShapeDtypeStruct((n, 3 * WIDTH), F32),
                   jax.ShapeDtypeStruct((n, WIDTH), F32),
                   jax.ShapeDtypeStruct((n, 2 * D_MODEL), BF16),
                   jax.ShapeDtypeStruct((n, 128), F32)],
        compiler_params=pltpu.CompilerParams(dimension_semantics=("parallel",),
                                             vmem_limit_bytes=VMEM_LIMIT),
        name="inproj",
    )(hp, g1, w_all, ph, gq, gk, acoef, dtb)


def _sb_kernel(q_ref, k_ref, v_ref, tri_ref, o_ref, *, nblk):
    qi = pl.program_id(2)
    q = q_ref[0]
    lane = _iota((BLK, BLK), 1)
    row = _iota((BLK, BLK), 0)
    head_lo = lane < HEAD_DIM
    qf = q.astype(F32)
    qh = (jnp.where(head_lo, qf, 0.0).astype(BF16), jnp.where(head_lo, 0.0, qf).astype(BF16))
    tri = tri_ref[...]

    def tile(kphys, carry, acc, vis):
        start = pl.multiple_of(kphys * BLK, BLK)
        k = k_ref[0, pl.ds(start, BLK), :]
        v = v_ref[0, pl.ds(start, BLK), :]
        new_carry, new_acc = [], []
        for h in range(2):
            z = _dot_nt(qh[h], k)
            soft = jnp.log1p(jnp.exp(-jnp.abs(z)))
            log_beta = jnp.minimum(z, 0.0) - soft
            log_om = -jnp.maximum(z, 0.0) - soft
            if vis is not None:
                log_om = jnp.where(vis, log_om, 0.0)
            hi, lo = _split(log_om)
            rt = _dot(jnp.concatenate([hi, lo], axis=1), tri)
            a = jnp.exp(log_beta + rt[:, :BLK] + carry[h])
            if vis is not None:
                a = jnp.where(vis, a, 0.0)
            new_acc.append(acc[h] + _dot(a.astype(BF16), v))
            new_carry.append(carry[h] + rt[:, BLK:])
        return tuple(new_carry), tuple(new_acc)

    zf = jnp.zeros((BLK, BLK), F32)
    carry, acc = tile(qi, (zf, zf), (zf, zf), lane < row)

    def body(it, state):
        return tile(qi - 1 - it, state[0], state[1], None)

    carry, acc = lax.fori_loop(0, qi, body, (carry, acc))
    carry, acc = tile(nblk - 1, carry, acc, lane >= PAD)
    o_ref[0] = jnp.where(head_lo, acc[0], acc[1]).astype(o_ref.dtype)


def _sb_attn(sbqkv, tri, batch, lp):
    nblk = lp // BLK
    nq = nblk - 1
    pairs = WIDTH // BLK
    x3 = sbqkv.reshape(batch, lp, 3 * WIDTH)
    return pl.pallas_call(
        functools.partial(_sb_kernel, nblk=nblk),
        grid=(batch, pairs, nq),
        in_specs=[pl.BlockSpec((1, BLK, BLK), lambda b, p, i: (b, i, p)),
                  pl.BlockSpec((1, lp, BLK), lambda b, p, i: (b, 0, pairs + p)),
                  pl.BlockSpec((1, lp, BLK), lambda b, p, i: (b, 0, 2 * pairs + p)),
                  pl.BlockSpec((2 * BLK, 2 * BLK), lambda b, p, i: (0, 0))],
        out_specs=pl.BlockSpec((1, BLK, BLK), lambda b, p, i: (b, i, p)),
        out_shape=jax.ShapeDtypeStruct((batch, nq * BLK, WIDTH), BF16),
        compiler_params=pltpu.CompilerParams(
            dimension_semantics=("parallel", "parallel", "arbitrary"), vmem_limit_bytes=VMEM_LIMIT),
        name="sb_attn",
    )(x3, x3, x3, tri)


def _dn_kernel(q_ref, k_ref, v_ref, z_ref, bg_ref, cwq_ref, cwk_ref, cwv_ref, gn_ref,
               o_ref, xbuf, s_ref):
    hp = pl.program_id(1)
    j = pl.program_id(2)

    @pl.when(j == 0)
    def _():
        xbuf[...] = jnp.zeros_like(xbuf)
        s_ref[...] = jnp.zeros_like(s_ref)

    def conv_silu(c, x_ref, cw_ref):
        xbuf[c, 8:8 + BLK, :] = x_ref[0]
        w = cw_ref[...]
        y = (w[3:4] * xbuf[c, 8:8 + BLK, :] + w[2:3] * xbuf[c, 7:7 + BLK, :]
             + w[1:2] * xbuf[c, 6:6 + BLK, :] + w[0:1] * xbuf[c, 5:5 + BLK, :])
        xbuf[c, 0:8, :] = xbuf[c, BLK:BLK + 8, :]
        return y * _sigmoid(y)

    lane = _iota((BLK, BLK), 1)
    row = _iota((BLK, BLK), 0)
    lane_head = lane >> 6
    row_head = row >> 6

    def pattern(cond):
        return jnp.where(cond, 1.0, 0.0).astype(BF16)

    same_head = pattern(lane_head == row_head)
    low_incl = (lane_head == row_head) & (lane <= row)
    low_strict = (lane_head == row_head) & (lane < row)
    cum_incl = pattern(low_incl)

    def l2(t):
        ss = _dot_xr(t * t, same_head)
        return t * lax.rsqrt(ss + EPS)

    q = l2(conv_silu(0, q_ref, cwq_ref)) * (HEAD_DIM ** -0.5)
    k = l2(conv_silu(1, k_ref, cwk_ref))
    v = conv_silu(2, v_ref, cwv_ref)

    bg = bg_ref[0]
    live = (row >= PAD) | (j > 0)
    pick_b = pattern(row == 2 * hp + lane_head)
    pick_g = pattern(row == HEADS + 2 * hp + lane_head)
    beta = jnp.where(live, _dot_xr(bg, pick_b), 0.0)
    g = jnp.where(live, _dot_xr(bg, pick_g), 0.0)
    gc = _dot_xl(cum_incl, g)
    g_last = _dot_xl(same_head, g)
    eg = jnp.exp(gc)
    k_beta = k * beta
    v_beta = v * beta
    kbg = k_beta * eg
    q_decay = q * eg
    k_tail = k * jnp.exp(g_last - gc)

    eye = jnp.where(lane == row, 1.0, 0.0).astype(F32)
    u_parts, w_parts, a_parts = [], [], []
    for h in range(2):
        mine = lane_head == h
        pick = pattern(row == h * HEAD_DIM)
        gch = _dot_xr(gc, pick)
        decay = jnp.exp(jnp.where(low_incl, gch - gch.T, -jnp.inf))
        kk = _dot_nt(jnp.where(mine, k_beta, 0.0).astype(BF16), k.astype(BF16))
        qk = _dot_nt(jnp.where(mine, q, 0.0).astype(BF16), k.astype(BF16))
        m = jnp.where(low_strict, kk * decay, 0.0)
        a_parts.append(jnp.where(low_incl, qk * decay, 0.0))
        t_inv = eye - m
        mp = m
        for _ in range(5):
            mp = _dot3(mp, mp)
            t_inv = t_inv + _dot3(t_inv, mp)
        u_parts.append(_dot3(t_inv, v_beta))
        w_parts.append(_dot3(t_inv, kbg))
    head0 = lane_head == 0
    u = jnp.where(head0, u_parts[0], u_parts[1])
    w = jnp.where(head0, w_parts[0], w_parts[1])

    block_diag = lane_head == row_head
    state = s_ref[...]
    vn_parts, inter_parts = [], []
    for c in range(2):
        r = slice(c * CHUNK, (c + 1) * CHUNK)
        sb = state.astype(BF16)
        vn = u[r] - _dot(w[r].astype(BF16), sb)
        inter_parts.append(_dot(q_decay[r].astype(BF16), sb))
        gl = jnp.exp(g_last[c * CHUNK:c * CHUNK + 1, :])
        upd = _dot_tn(k_tail[r].astype(BF16), vn.astype(BF16))
        state = jnp.where(block_diag, state * gl + upd, 0.0)
        vn_parts.append(vn)
    s_ref[...] = state
    vn = jnp.concatenate(vn_parts, axis=0)
    o = jnp.concatenate(inter_parts, axis=0)
    for h in range(2):
        o = o + _dot(a_parts[h].astype(BF16), jnp.where(lane_head == h, vn, 0.0).astype(BF16))

    @pl.when(j > 0)
    def _():
        ms = _dot_xr(o * o, same_head) * (1.0 / HEAD_DIM)
        zz = z_ref[0]
        o_ref[0] = (o * lax.rsqrt(ms + EPS) * gn_ref[...] * (zz * _sigmoid(zz))).astype(o_ref.dtype)


def _deltanet(dnqkv, z, bg, conv_w, gn, batch, lp):
    nblk = lp // BLK
    pairs = WIDTH // BLK
    phys = lambda j: (j + nblk - 1) % nblk
    x3 = dnqkv.reshape(batch, lp, 3 * WIDTH)
    z3 = z.reshape(batch, lp, WIDTH)
    bg3 = bg.reshape(batch, lp, 128)
    col = lambda off: pl.BlockSpec((1, BLK, BLK), lambda b, p, j: (b, phys(j), off * pairs + p))
    cw = lambda off: pl.BlockSpec((4, BLK), lambda b, p, j: (0, off * pairs + p))
    return pl.pallas_call(
        _dn_kernel,
        grid=(batch, pairs, nblk),
        in_specs=[col(0), col(1), col(2),
                  pl.BlockSpec((1, BLK, BLK), lambda b, p, j: (b, phys(j), p)),
                  pl.BlockSpec((1, BLK, 128), lambda b, p, j: (b, phys(j), 0)),
                  cw(0), cw(1), cw(2),
                  pl.BlockSpec((1, BLK), lambda b, p, j: (0, 0))],
        out_specs=pl.BlockSpec((1, BLK, BLK), lambda b, p, j: (b, jnp.maximum(j - 1, 0), p)),
        out_shape=jax.ShapeDtypeStruct((batch, (nblk - 1) * BLK, WIDTH), BF16),
        scratch_shapes=[pltpu.VMEM((3, BLK + 8, BLK), F32), pltpu.VMEM((BLK, BLK), F32)],
        compiler_params=pltpu.CompilerParams(
            dimension_semantics=("parallel", "parallel", "arbitrary"), vmem_limit_bytes=VMEM_LIMIT),
        name="deltanet",
    )(x3, x3, x3, z3, bg3, conv_w, conv_w, conv_w, gn)


def _merge_kernel(osb_ref, odn_ref, gate_ref, x_ref, wsb_ref, wdn_ref, wo_ref, h_ref):
    gate = gate_ref[0].astype(F32)
    mix = (gate[:, :D_MODEL] * _dot(osb_ref[...], wsb_ref[...])
           + gate[:, D_MODEL:] * _dot(odn_ref[...], wdn_ref[...]))
    h_ref[...] = x_ref[...] + _dot(mix.astype(BF16), wo_ref[...])


def _merge(o_sb, o_dn, gates, x2, wsb, wdn, wo, batch, seq, lp):
    n = batch * seq
    rb = ROWS_OUT
    per = seq // rb
    row = lambda c: pl.BlockSpec((rb, c), lambda i: (i, 0))
    const = lambda r, c: pl.BlockSpec((r, c), lambda i: (0, 0))
    return pl.pallas_call(
        _merge_kernel,
        grid=(n // rb,),
        in_specs=[row(WIDTH), row(WIDTH),
                  pl.BlockSpec((1, rb, 2 * D_MODEL), lambda i: (i // per, i % per, 0)),
                  row(D_MODEL), const(WIDTH, D_MODEL), const(WIDTH, D_MODEL), const(D_MODEL, D_MODEL)],
        out_specs=row(D_MODEL),
        out_shape=jax.ShapeDtypeStruct((n, D_MODEL), F32),
        compiler_params=pltpu.CompilerParams(dimension_semantics=("parallel",),
                                             vmem_limit_bytes=VMEM_LIMIT),
        name="merge",
    )(o_sb.reshape(n, WIDTH), o_dn.reshape(n, WIDTH), gates.reshape(batch, lp, 2 * D_MODEL),
      x2, wsb, wdn, wo)


def _top_rows(s, k, payload=None):
    nrow = s.shape[0]
    rid = _iota(s.shape, 0)
    vals, picks = [], []
    for _ in range(k):
        m = jnp.max(s, axis=0, keepdims=True)
        am = jnp.min(jnp.where(s == m, rid, nrow), axis=0, keepdims=True)
        hit = rid == am
        vals.append(m)
        picks.append(am if payload is None
                     else jnp.max(jnp.where(hit, payload, -1), axis=0, keepdims=True))
        s = jnp.where(hit, -jnp.inf, s)
    return jnp.concatenate(vals, axis=0), jnp.concatenate(picks, axis=0)


def _route_kernel(h_ref, g2_ref, wq_ref, keys_ref, xn_ref, idx_ref, gate_ref):
    x = h_ref[...]
    ms = jnp.mean(x * x, axis=-1, keepdims=True)
    xn = x * lax.rsqrt(ms + EPS) * g2_ref[...]
    xn_ref[...] = xn
    q = _dot(xn.astype(BF16), wq_ref[...])
    keys = (keys_ref[0].astype(BF16), keys_ref[1].astype(BF16))
    idx_rows, gate_rows = [], []
    for h in range(PEER_HEADS):
        tops = []
        for p in range(2):
            c0 = (2 * h + p) * PEER_KEYS
            s = _dot_nt(keys[p], q[:, c0:c0 + PEER_KEYS].astype(BF16))
            tops.append(_top_rows(s, PEER_TOPK))
        (s1, i1), (s2, i2) = tops
        cand_s = jnp.concatenate([s1[a:a + 1] + s2 for a in range(PEER_TOPK)], axis=0)
        cand_i = jnp.concatenate([i1[a:a + 1] * PEER_KEYS + i2 for a in range(PEER_TOPK)], axis=0)
        top_s, top_i = _top_rows(cand_s, PEER_TOPK, cand_i)
        e = jnp.exp(top_s - top_s[0:1])
        idx_rows.append(top_i)
        gate_rows.append(e / jnp.sum(e, axis=0, keepdims=True))
    idx_ref[...] = jnp.concatenate(idx_rows, axis=0)
    gate_ref[...] = jnp.concatenate(gate_rows, axis=0)


def _route(h1, g2, wq, keys):
    n = h1.shape[0]
    rb = ROWS_OUT
    return pl.pallas_call(
        _route_kernel,
        grid=(n // rb,),
        in_specs=[pl.BlockSpec((rb, D_MODEL), lambda i: (i, 0)),
                  pl.BlockSpec((1, D_MODEL), lambda i: (0, 0)),
                  pl.BlockSpec(wq.shape, lambda i: (0, 0)),
                  pl.BlockSpec(keys.shape, lambda i: (0, 0, 0))],
        out_specs=[pl.BlockSpec((rb, D_MODEL), lambda i: (i, 0)),
                   pl.BlockSpec((PEER_HK, rb), lambda i: (0, i)),
                   pl.BlockSpec((PEER_HK, rb), lambda i: (0, i))],
        out_shape=[jax.ShapeDtypeStruct((n, D_MODEL), F32),
                   jax.ShapeDtypeStruct((PEER_HK, n), jnp.int32),
                   jax.ShapeDtypeStruct((PEER_HK, n), F32)],
        compiler_params=pltpu.CompilerParams(dimension_semantics=("parallel",),
                                             vmem_limit_bytes=VMEM_LIMIT),
        name="route",
    )(h1, g2, wq, keys)


def _peer_kernel(idx_ref, idx_next_ref, gate_ref, xn_ref, h_ref, u_hbm, v_hbm, out_ref,
                 ubuf, vbuf, sem):
    i = pl.program_id(0)
    n = pl.num_programs(0)
    slot = i % 2
    rows = PEER_TB * PEER_HK

    def row_copy(table, buf, ix, r, sl, which):
        return pltpu.make_async_copy(table.at[pl.ds(ix, 1)], buf.at[sl, pl.ds(r, 1)], sem.at[which, sl])

    def issue(src_idx, sl):
        def body(r, carry):
            ix = src_idx[r // PEER_HK, r % PEER_HK]
            row_copy(u_hbm, ubuf, ix, r, sl, 0).start()
            row_copy(v_hbm, vbuf, ix, r, sl, 1).start()
            return carry
        lax.fori_loop(0, rows, body, 0, unroll=8)

    @pl.when(i == 0)
    def _():
        issue(idx_ref, 0)

    @pl.when(i + 1 < n)
    def _():
        issue(idx_next_ref, 1 - slot)

    pltpu.make_async_copy(u_hbm.at[pl.ds(0, rows)], ubuf.at[slot], sem.at[0, slot]).wait()
    pltpu.make_async_copy(v_hbm.at[pl.ds(0, rows)], vbuf.at[slot], sem.at[1, slot]).wait()

    x = xn_ref[...].astype(BF16)
    a_all = _dot_nt(x, ubuf[slot].astype(BF16))
    trow = _iota((PEER_TB, PEER_HK), 0)
    act = jnp.zeros((PEER_TB, PEER_HK), F32)
    for t in range(PEER_TB):
        act = act + jnp.where(trow == t, a_all[:, t * PEER_HK:(t + 1) * PEER_HK], 0.0)
    gelu = 0.5 * act * (1.0 + lax.erf(act * (2.0 ** -0.5)))
    wgt = gelu * gate_ref[...]
    wide = jnp.concatenate([wgt] * PEER_TB, axis=1)
    own = (_iota(wide.shape, 1) >> 7) == _iota(wide.shape, 0)
    y = _dot(jnp.where(own, wide, 0.0).astype(BF16), vbuf[slot].astype(BF16))
    out_ref[...] = h_ref[...] + y


def _peer(idx, gate, xn, h1, u_tab, v_tab):
    n = h1.shape[0]
    tb = PEER_TB
    steps = n // tb
    rows = tb * PEER_HK
    smem = lambda f: pl.BlockSpec((tb, PEER_HK), f, memory_space=pltpu.SMEM)
    return pl.pallas_call(
        _peer_kernel,
        grid=(steps,),
        in_specs=[smem(lambda i: (i, 0)),
                  smem(lambda i: (jnp.minimum(i + 1, steps - 1), 0)),
                  pl.BlockSpec((tb, PEER_HK), lambda i: (i, 0)),
                  pl.BlockSpec((tb, D_MODEL), lambda i: (i, 0)),
                  pl.BlockSpec((tb, D_MODEL), lambda i: (i, 0)),
                  pl.BlockSpec(memory_space=pl.ANY),
                  pl.BlockSpec(memory_space=pl.ANY)],
        out_specs=pl.BlockSpec((tb, D_MODEL), lambda i: (i, 0)),
        out_shape=jax.ShapeDtypeStruct((n, D_MODEL), F32),
        scratch_shapes=[pltpu.VMEM((2, rows, D_MODEL), F32), pltpu.VMEM((2, rows, D_MODEL), F32),
                        pltpu.SemaphoreType.DMA((2, 2))],
        compiler_params=pltpu.CompilerParams(dimension_semantics=("arbitrary",),
                                             vmem_limit_bytes=VMEM_LIMIT),
        name="peer",
    )(idx, idx, gate, xn, h1, u_tab, v_tab)


def _constants():
    r = np.arange(WIDTH)
    ph = (r[:, None] // HEAD_DIM == r[None, :] // HEAD_DIM).astype(np.float32) / HEAD_DIM
    s = np.arange(BLK)
    later = (s[:, None] > s[None, :]).astype(np.float32)
    half = np.concatenate([later, np.ones((BLK, BLK), np.float32)], axis=1)
    tri = np.concatenate([half, half], axis=0)
    return jnp.asarray(ph, BF16), jnp.asarray(tri, BF16)


def _layer(x, meta_tokens, norm1_g, w_in, sb_q_norm_g, sb_k_norm_g, dn_conv_w, dn_a_log, dn_dt_bias,
           dn_out_norm_g, w_sb_out, w_dn_out, w_o, norm2_g, peer_w_q, peer_sub_keys, peer_u, peer_v):
    batch, seq, d = x.shape
    lp = seq + BLK
    ph, tri = _constants()

    tail = jnp.concatenate([jnp.zeros((PAD, d), x.dtype), meta_tokens.astype(x.dtype)], axis=0)
    hp = jnp.concatenate([x, jnp.broadcast_to(tail[None], (batch, BLK, d))], axis=1).reshape(batch * lp, d)

    c_ba = 3 * WIDTH + 3 * WIDTH + WIDTH
    w_all = jnp.concatenate([w_in[:, :c_ba], w_in[:, c_ba + 2 * HEADS:], w_in[:, c_ba:c_ba + 2 * HEADS],
                             jnp.zeros((d, 128 - 2 * HEADS), w_in.dtype)], axis=1).astype(BF16)
    gq = (jnp.tile(sb_q_norm_g.astype(F32), HEADS) * (HEAD_DIM ** -0.5))[None]
    gk = jnp.tile(sb_k_norm_g.astype(F32), HEADS)[None]
    lane_pad = lambda t: jnp.pad(t.astype(F32), (HEADS, 128 - 2 * HEADS))[None]
    acoef = lane_pad(-jnp.exp(dn_a_log.astype(F32)))
    dtb = lane_pad(dn_dt_bias)

    sbqkv, dnqkv, z, gates, bg = _inproj(hp, norm1_g.astype(F32)[None], w_all, ph, gq, gk, acoef, dtb)
    o_sb = _sb_attn(sbqkv, tri, batch, lp)
    gn = jnp.tile(dn_out_norm_g.astype(F32), 2)[None]
    o_dn = _deltanet(dnqkv, z, bg, dn_conv_w.astype(F32), gn, batch, lp)
    h1 = _merge(o_sb, o_dn, gates, x.reshape(batch * seq, d), w_sb_out.astype(BF16),
                w_dn_out.astype(BF16), w_o.astype(BF16), batch, seq, lp)
    xn2, idx_t, gate_t = _route(h1, norm2_g.astype(F32)[None], peer_w_q.astype(BF16), peer_sub_keys)
    h2 = _peer(idx_t.T, gate_t.T, xn2, h1, peer_u, peer_v)
    return h2.reshape(batch, seq, d)


def kernel(x, meta_tokens, norm1_g, w_in, sb_q_norm_g, sb_k_norm_g, dn_conv_w, dn_a_log, dn_dt_bias,
           dn_out_norm_g, w_sb_out, w_dn_out, w_o, norm2_g, peer_w_q, peer_sub_keys, peer_u, peer_v):
    assert norm1_g.shape[0] == 1, "one layer"
    return _layer(x, meta_tokens, norm1_g[0], w_in[0], sb_q_norm_g[0], sb_k_norm_g[0], dn_conv_w[0],
                  dn_a_log[0], dn_dt_bias[0], dn_out_norm_g[0], w_sb_out[0], w_dn_out[0], w_o[0],
                  norm2_g[0], peer_w_q[0], peer_sub_keys[0], peer_u[0], peer_v[0])
```

```python
import functools

import numpy as np
import jax
import jax.numpy as jnp
from jax import lax
from jax.experimental import pallas as pl
from jax.experimental.pallas import tpu as pltpu

F32 = jnp.float32
BF16 = jnp.bfloat16

D_MODEL = 1024
N_META = 16
BLK = 128
PAD = BLK - N_META
HEADS = 8
HEAD_DIM = 64
WIDTH = HEADS * HEAD_DIM
CHUNK = 64
PEER_HEADS = 8
PEER_KEYS = 128
PEER_TOPK = 16
PEER_HK = PEER_HEADS * PEER_TOPK
EPS = 1e-6

C_SB = 0
C_DN = 3 * WIDTH
C_Z = C_DN + 3 * WIDTH
C_GATE = C_Z + WIDTH
C_BA = C_GATE + 2 * D_MODEL
C_END = C_BA + 128

ROWS_IN = 256
ROWS_OUT = 256
PEER_TB = 16
PEER_UNROLL = 4
VMEM_LIMIT = 56 * 1024 * 1024


def _dot(a, b):
    return jnp.dot(a, b, preferred_element_type=F32)


def _dot_nt(a, b):
    return lax.dot_general(a, b, (((1,), (1,)), ((), ())), preferred_element_type=F32)


def _dot_tn(a, b):
    return lax.dot_general(a, b, (((0,), (0,)), ((), ())), preferred_element_type=F32)


def _split(a):
    hi = a.astype(BF16)
    lo = (a - hi.astype(F32)).astype(BF16)
    return hi, lo


def _dot_xr(a, b_exact):
    hi, lo = _split(a)
    return _dot(hi, b_exact) + _dot(lo, b_exact)


def _dot_xl(a_exact, b):
    hi, lo = _split(b)
    return _dot(a_exact, hi) + _dot(a_exact, lo)


def _dot3(a, b):
    ah, al = _split(a)
    bh, bl = _split(b)
    return _dot(ah, bh) + (_dot(ah, bl) + _dot(al, bh))


def _sigmoid(x):
    return 1.0 / (1.0 + jnp.exp(-x))


def _softplus(x):
    return jnp.maximum(x, 0.0) + jnp.log1p(jnp.exp(-jnp.abs(x)))


def _iota(shape, dim):
    return lax.broadcasted_iota(jnp.int32, shape, dim)


def _inproj_kernel(x_ref, g1_ref, w_ref, ph_ref, gq_ref, gk_ref, acoef_ref, dtb_ref,
                   sb_ref, dn_ref, z_ref, gate_ref, bg_ref):
    x = x_ref[...]
    ms = jnp.mean(x * x, axis=-1, keepdims=True)
    xn = (x * lax.rsqrt(ms + EPS) * g1_ref[...]).astype(BF16)

    def proj(c0, c1):
        return _dot(xn, w_ref[:, c0:c1])

    def head_norm(t, g):
        msh = _dot((t * t).astype(BF16), ph_ref[...])
        return (t * lax.rsqrt(msh + EPS) * g).astype(BF16)

    sb_ref[:, 0:WIDTH] = head_norm(proj(C_SB, C_SB + WIDTH), gq_ref[...])
    sb_ref[:, WIDTH:2 * WIDTH] = head_norm(proj(C_SB + WIDTH, C_SB + 2 * WIDTH), gk_ref[...])
    sb_ref[:, 2 * WIDTH:3 * WIDTH] = proj(C_SB + 2 * WIDTH, C_SB + 3 * WIDTH).astype(BF16)
    dn_ref[...] = proj(C_DN, C_DN + 3 * WIDTH)
    z_ref[...] = proj(C_Z, C_Z + WIDTH)
    gate_ref[...] = _sigmoid(proj(C_GATE, C_GATE + 2 * D_MODEL)).astype(BF16)
    ba = proj(C_BA, C_END)
    lane = _iota(ba.shape, 1)
    bg_ref[...] = jnp.where(lane < HEADS, _sigmoid(ba), acoef_ref[...] * _softplus(ba + dtb_ref[...]))


def _inproj(hp, g1, w_all, ph, gq, gk, acoef, dtb):
    n = hp.shape[0]
    rb = ROWS_IN
    const = lambda shape: pl.BlockSpec(shape, lambda i: (0, 0))
    row = lambda c: pl.BlockSpec((rb, c), lambda i: (i, 0))
    return pl.pallas_call(
        _inproj_kernel,
        grid=(n // rb,),
        in_specs=[row(D_MODEL), const((1, D_MODEL)), const((D_MODEL, C_END)), const((WIDTH, WIDTH)),
                  const((1, WIDTH)), const((1, WIDTH)), const((1, 128)), const((1, 128))],
        out_specs=[row(3 * WIDTH), row(3 * WIDTH), row(WIDTH), row(2 * D_MODEL), row(128)],
        out_shape=[jax.ShapeDtypeStruct((n, 3 * WIDTH), BF16),
                   jax.ShapeDtypeStruct((n, 3 * WIDTH), F32),
                   jax.ShapeDtypeStruct((n, WIDTH), F32),
                   jax.ShapeDtypeStruct((n, 2 * D_MODEL), BF16),
                   jax.ShapeDtypeStruct((n, 128), F32)],
        compiler_params=pltpu.CompilerParams(dimension_semantics=("parallel",),
                                             vmem_limit_bytes=VMEM_LIMIT),
        name="inproj",
    )(hp, g1, w_all, ph, gq, gk, acoef, dtb)


def _sb_kernel(q_ref, k_ref, v_ref, tri_ref, o_ref, *, nblk):
    qi = pl.program_id(2)
    q = q_ref[0]
    lane = _iota((BLK, BLK), 1)
    row = _iota((BLK, BLK), 0)
    head_lo = lane < HEAD_DIM
    qf = q.astype(F32)
    qh = (jnp.where(head_lo, qf, 0.0).astype(BF16), jnp.where(head_lo, 0.0, qf).astype(BF16))
    tri = tri_ref[...]

    def tile(kphys, carry, acc, vis):
        start = pl.multiple_of(kphys * BLK, BLK)
        k = k_ref[0, pl.ds(start, BLK), :]
        v = v_ref[0, pl.ds(start, BLK), :]
        new_carry, new_acc = [], []
        for h in range(2):
            z = _dot_nt(qh[h], k)
            soft = jnp.log1p(jnp.exp(-jnp.abs(z)))
            log_beta = jnp.minimum(z, 0.0) - soft
            log_om = -jnp.maximum(z, 0.0) - soft
            if vis is not None:
                log_om = jnp.where(vis, log_om, 0.0)
            hi, lo = _split(log_om)
            rt = _dot(jnp.concatenate([hi, lo], axis=1), tri)
            a = jnp.exp(log_beta + rt[:, :BLK] + carry[h])
            if vis is not None:
                a = jnp.where(vis, a, 0.0)
            new_acc.append(acc[h] + _dot(a.astype(BF16), v))
            new_carry.append(carry[h] + rt[:, BLK:])
        return tuple(new_carry), tuple(new_acc)

    zf = jnp.zeros((BLK, BLK), F32)
    carry, acc = tile(qi, (zf, zf), (zf, zf), lane < row)

    def body(it, state):
        return tile(qi - 1 - it, state[0], state[1], None)

    carry, acc = lax.fori_loop(0, qi, body, (carry, acc))
    carry, acc = tile(nblk - 1, carry, acc, lane >= PAD)
    o_ref[0] = jnp.where(head_lo, acc[0], acc[1]).astype(o_ref.dtype)


def _sb_attn(sbqkv, tri, batch, lp):
    nblk = lp // BLK
    nq = nblk - 1
    pairs = WIDTH // BLK
    x3 = sbqkv.reshape(batch, lp, 3 * WIDTH)
    return pl.pallas_call(
        functools.partial(_sb_kernel, nblk=nblk),
        grid=(batch, pairs, nq),
        in_specs=[pl.BlockSpec((1, BLK, BLK), lambda b, p, i: (b, i, p)),
                  pl.BlockSpec((1, lp, BLK), lambda b, p, i: (b, 0, pairs + p)),
                  pl.BlockSpec((1, lp, BLK), lambda b, p, i: (b, 0, 2 * pairs + p)),
                  pl.BlockSpec((2 * BLK, 2 * BLK), lambda b, p, i: (0, 0))],
        out_specs=pl.BlockSpec((1, BLK, BLK), lambda b, p, i: (b, i, p)),
        out_shape=jax.ShapeDtypeStruct((batch, nq * BLK, WIDTH), BF16),
        compiler_params=pltpu.CompilerParams(
            dimension_semantics=("parallel", "parallel", "arbitrary"), vmem_limit_bytes=VMEM_LIMIT),
        name="sb_attn",
    )(x3, x3, x3, tri)


def _dn_kernel(q_ref, k_ref, v_ref, z_ref, bg_ref, cwq_ref, cwk_ref, cwv_ref, gn_ref,
               o_ref, xbuf, s_ref):
    hp = pl.program_id(1)
    j = pl.program_id(2)

    @pl.when(j == 0)
    def _():
        xbuf[...] = jnp.zeros_like(xbuf)
        s_ref[...] = jnp.zeros_like(s_ref)

    def conv_silu(c, x_ref, cw_ref):
        xbuf[c, 8:8 + BLK, :] = x_ref[0]
        w = cw_ref[...]
        y = (w[3:4] * xbuf[c, 8:8 + BLK, :] + w[2:3] * xbuf[c, 7:7 + BLK, :]
             + w[1:2] * xbuf[c, 6:6 + BLK, :] + w[0:1] * xbuf[c, 5:5 + BLK, :])
        xbuf[c, 0:8, :] = xbuf[c, BLK:BLK + 8, :]
        return y * _sigmoid(y)

    lane = _iota((BLK, BLK), 1)
    row = _iota((BLK, BLK), 0)
    lane_head = lane >> 6
    row_head = row >> 6

    def pattern(cond):
        return jnp.where(cond, 1.0, 0.0).astype(BF16)

    same_head = pattern(lane_head == row_head)
    low_incl = (lane_head == row_head) & (lane <= row)
    low_strict = (lane_head == row_head) & (lane < row)
    cum_incl = pattern(low_incl)

    def l2(t):
        ss = _dot_xr(t * t, same_head)
        return t * lax.rsqrt(ss + EPS)

    q = l2(conv_silu(0, q_ref, cwq_ref)) * (HEAD_DIM ** -0.5)
    k = l2(conv_silu(1, k_ref, cwk_ref))
    v = conv_silu(2, v_ref, cwv_ref)

    bg = bg_ref[0]
    live = (row >= PAD) | (j > 0)
    pick_b = pattern(row == 2 * hp + lane_head)
    pick_g = pattern(row == HEADS + 2 * hp + lane_head)
    beta = jnp.where(live, _dot_xr(bg, pick_b), 0.0)
    g = jnp.where(live, _dot_xr(bg, pick_g), 0.0)
    gc = _dot_xl(cum_incl, g)
    g_last = _dot_xl(same_head, g)
    eg = jnp.exp(gc)
    k_beta = k * beta
    v_beta = v * beta
    kbg = k_beta * eg
    q_decay = q * eg
    k_tail = k * jnp.exp(g_last - gc)

    eye = jnp.where(lane == row, 1.0, 0.0).astype(F32)
    u_parts, w_parts, a_parts = [], [], []
    for h in range(2):
        mine = lane_head == h
        pick = pattern(row == h * HEAD_DIM)
        gch = _dot_xr(gc, pick)
        decay = jnp.exp(jnp.where(low_incl, gch - gch.T, -jnp.inf))
        kk = _dot_nt(jnp.where(mine, k_beta, 0.0).astype(BF16), k.astype(BF16))
        qk = _dot_nt(jnp.where(mine, q, 0.0).astype(BF16), k.astype(BF16))
        m = jnp.where(low_strict, kk * decay, 0.0)
        a_parts.append(jnp.where(low_incl, qk * decay, 0.0))
        t_inv = eye - m
        mp = m
        for _ in range(5):
            mp = _dot3(mp, mp)
            t_inv = t_inv + _dot3(t_inv, mp)
        u_parts.append(_dot3(t_inv, v_beta))
        w_parts.append(_dot3(t_inv, kbg))
    head0 = lane_head == 0
    u = jnp.where(head0, u_parts[0], u_parts[1])
    w = jnp.where(head0, w_parts[0], w_parts[1])

    block_diag = lane_head == row_head
    state = s_ref[...]
    vn_parts, inter_parts = [], []
    for c in range(2):
        r = slice(c * CHUNK, (c + 1) * CHUNK)
        sb = state.astype(BF16)
        vn = u[r] - _dot(w[r].astype(BF16), sb)
        inter_parts.append(_dot(q_decay[r].astype(BF16), sb))
        gl = jnp.exp(g_last[c * CHUNK:c * CHUNK + 1, :])
        upd = _dot_tn(k_tail[r].astype(BF16), vn.astype(BF16))
        state = jnp.where(block_diag, state * gl + upd, 0.0)
        vn_parts.append(vn)
    s_ref[...] = state
    vn = jnp.concatenate(vn_parts, axis=0)
    o = jnp.concatenate(inter_parts, axis=0)
    for h in range(2):
        o = o + _dot(a_parts[h].astype(BF16), jnp.where(lane_head == h, vn, 0.0).astype(BF16))

    @pl.when(j > 0)
    def _():
        ms = _dot_xr(o * o, same_head) * (1.0 / HEAD_DIM)
        zz = z_ref[0]
        o_ref[0] = (o * lax.rsqrt(ms + EPS) * gn_ref[...] * (zz * _sigmoid(zz))).astype(o_ref.dtype)


def _deltanet(dnqkv, z, bg, conv_w, gn, batch, lp):
    nblk = lp // BLK
    pairs = WIDTH // BLK
    phys = lambda j: (j + nblk - 1) % nblk
    x3 = dnqkv.reshape(batch, lp, 3 * WIDTH)
    z3 = z.reshape(batch, lp, WIDTH)
    bg3 = bg.reshape(batch, lp, 128)
    col = lambda off: pl.BlockSpec((1, BLK, BLK), lambda b, p, j: (b, phys(j), off * pairs + p))
    cw = lambda off: pl.BlockSpec((4, BLK), lambda b, p, j: (0, off * pairs + p))
    return pl.pallas_call(
        _dn_kernel,
        grid=(batch, pairs, nblk),
        in_specs=[col(0), col(1), col(2),
                  pl.BlockSpec((1, BLK, BLK), lambda b, p, j: (b, phys(j), p)),
                  pl.BlockSpec((1, BLK, 128), lambda b, p, j: (b, phys(j), 0)),
                  cw(0), cw(1), cw(2),
                  pl.BlockSpec((1, BLK), lambda b, p, j: (0, 0))],
        out_specs=pl.BlockSpec((1, BLK, BLK), lambda b, p, j: (b, jnp.maximum(j - 1, 0), p)),
        out_shape=jax.ShapeDtypeStruct((batch, (nblk - 1) * BLK, WIDTH), BF16),
        scratch_shapes=[pltpu.VMEM((3, BLK + 8, BLK), F32), pltpu.VMEM((BLK, BLK), F32)],
        compiler_params=pltpu.CompilerParams(
            dimension_semantics=("parallel", "parallel", "arbitrary"), vmem_limit_bytes=VMEM_LIMIT),
        name="deltanet",
    )(x3, x3, x3, z3, bg3, conv_w, conv_w, conv_w, gn)


def _merge_kernel(osb_ref, odn_ref, gate_ref, x_ref, wsb_ref, wdn_ref, wo_ref, h_ref):
    gate = gate_ref[0].astype(F32)
    mix = (gate[:, :D_MODEL] * _dot(osb_ref[...], wsb_ref[...])
           + gate[:, D_MODEL:] * _dot(odn_ref[...], wdn_ref[...]))
    h_ref[...] = x_ref[...] + _dot(mix.astype(BF16), wo_ref[...])


def _merge(o_sb, o_dn, gates, x2, wsb, wdn, wo, batch, seq, lp):
    n = batch * seq
    rb = ROWS_OUT
    per = seq // rb
    row = lambda c: pl.BlockSpec((rb, c), lambda i: (i, 0))
    const = lambda r, c: pl.BlockSpec((r, c), lambda i: (0, 0))
    return pl.pallas_call(
        _merge_kernel,
        grid=(n // rb,),
        in_specs=[row(WIDTH), row(WIDTH),
                  pl.BlockSpec((1, rb, 2 * D_MODEL), lambda i: (i // per, i % per, 0)),
                  row(D_MODEL), const(WIDTH, D_MODEL), const(WIDTH, D_MODEL), const(D_MODEL, D_MODEL)],
        out_specs=row(D_MODEL),
        out_shape=jax.ShapeDtypeStruct((n, D_MODEL), F32),
        compiler_params=pltpu.CompilerParams(dimension_semantics=("parallel",),
                                             vmem_limit_bytes=VMEM_LIMIT),
        name="merge",
    )(o_sb.reshape(n, WIDTH), o_dn.reshape(n, WIDTH), gates.reshape(batch, lp, 2 * D_MODEL),
      x2, wsb, wdn, wo)


def _top_rows(s, k, payload=None):
    nrow = s.shape[0]
    rid = _iota(s.shape, 0)
    vals, picks = [], []
    for _ in range(k):
        m = jnp.max(s, axis=0, keepdims=True)
        am = jnp.min(jnp.where(s == m, rid, nrow), axis=0, keepdims=True)
        hit = rid == am
        vals.append(m)
        picks.append(am if payload is None
                     else jnp.max(jnp.where(hit, payload, -1), axis=0, keepdims=True))
        s = jnp.where(hit, -jnp.inf, s)
    return jnp.concatenate(vals, axis=0), jnp.concatenate(picks, axis=0)


def _route_kernel(h_ref, g2_ref, wq_ref, keys_ref, xn_ref, idx_ref, gate_ref):
    x = h_ref[...]
    ms = jnp.mean(x * x, axis=-1, keepdims=True)
    xn = x * lax.rsqrt(ms + EPS) * g2_ref[...]
    xn_ref[...] = xn
    q = _dot(xn.astype(BF16), wq_ref[...])
    keys = (keys_ref[0].astype(BF16), keys_ref[1].astype(BF16))
    idx_rows, gate_rows = [], []
    for h in range(PEER_HEADS):
        tops = []
        for p in range(2):
            c0 = (2 * h + p) * PEER_KEYS
            s = _dot_nt(keys[p], q[:, c0:c0 + PEER_KEYS].astype(BF16))
            tops.append(_top_rows(s, PEER_TOPK))
        (s1, i1), (s2, i2) = tops
        cand_s = jnp.concatenate([s1[a:a + 1] + s2 for a in range(PEER_TOPK)], axis=0)
        cand_i = jnp.concatenate([i1[a:a + 1] * PEER_KEYS + i2 for a in range(PEER_TOPK)], axis=0)
        top_s, top_i = _top_rows(cand_s, PEER_TOPK, cand_i)
        e = jnp.exp(top_s - top_s[0:1])
        idx_rows.append(top_i)
        gate_rows.append(e / jnp.sum(e, axis=0, keepdims=True))
    idx_ref[...] = jnp.concatenate(idx_rows, axis=0)
    gate_ref[...] = jnp.concatenate(gate_rows, axis=0)


def _route(h1, g2, wq, keys):
    n = h1.shape[0]
    rb = ROWS_OUT
    return pl.pallas_call(
        _route_kernel,
        grid=(n // rb,),
        in_specs=[pl.BlockSpec((rb, D_MODEL), lambda i: (i, 0)),
                  pl.BlockSpec((1, D_MODEL), lambda i: (0, 0)),
                  pl.BlockSpec(wq.shape, lambda i: (0, 0)),
                  pl.BlockSpec(keys.shape, lambda i: (0, 0, 0))],
        out_specs=[pl.BlockSpec((rb, D_MODEL), lambda i: (i, 0)),
                   pl.BlockSpec((PEER_HK, rb), lambda i: (0, i)),
                   pl.BlockSpec((PEER_HK, rb), lambda i: (0, i))],
        out_shape=[jax.ShapeDtypeStruct((n, D_MODEL), F32),
                   jax.ShapeDtypeStruct((PEER_HK, n), jnp.int32),
                   jax.ShapeDtypeStruct((PEER_HK, n), F32)],
        compiler_params=pltpu.CompilerParams(dimension_semantics=("parallel",),
                                             vmem_limit_bytes=VMEM_LIMIT),
        name="route",
    )(h1, g2, wq, keys)


def _peer_kernel(idx_ref, idx_next_ref, gate_ref, xn_ref, h_ref, sel_ref, rep_ref, tab_hbm, out_ref,
                 gbuf, act_ref, sem):
    i = pl.program_id(0)
    n = pl.num_programs(0)
    slot = i % 2
    rows = PEER_TB * PEER_HK
    sub = D_MODEL // 128

    def issue_token(src_idx, t, sl):
        base = t * PEER_HK
        for e in range(PEER_HK):
            pltpu.make_async_copy(tab_hbm.at[src_idx[t, e]], gbuf.at[sl, base + e], sem.at[sl]).start()

    @pl.when(i == 0)
    def _():
        def body(t, carry):
            issue_token(idx_ref, t, 0)
            return carry
        lax.fori_loop(0, PEER_TB, body, 0)

    pltpu.make_async_copy(tab_hbm.at[pl.ds(0, rows)], gbuf.at[slot], sem.at[slot]).wait()

    ones = jnp.ones((sub, 128), BF16)
    trow = _iota((PEER_TB, 128), 0)

    def tile_rows(t):
        return gbuf[slot, pl.ds(pl.multiple_of(t * PEER_HK, PEER_HK), PEER_HK)]

    def act_body(g, carry):
        for k in range(PEER_UNROLL):
            t = g * PEER_UNROLL + k
            issue_token(idx_next_ref, t, 1 - slot)
            u = lax.bitcast_convert_type(tile_rows(t) << 16, F32)
            prod = (u * xn_ref[t][None]).reshape(PEER_HK * sub, 128).astype(BF16)
            part = _dot(sel_ref[...], prod)
            hi, lo = _split(part)
            act_ref[pl.ds(t, 1), :] = (_dot_nt(ones, hi) + _dot_nt(ones, lo))[0:1]
        return carry

    lax.fori_loop(0, PEER_TB // PEER_UNROLL, act_body, 0)

    act = act_ref[...]
    wgt = 0.5 * act * (1.0 + lax.erf(act * (2.0 ** -0.5))) * gate_ref[...]
    w_hi, w_lo = _split(wgt)

    def mix_body(g, carry):
        for k in range(PEER_UNROLL):
            t = g * PEER_UNROLL + k
            pick = jnp.where(trow == t, 1.0, 0.0).astype(BF16)
            col = _dot_tn(w_hi, pick) + _dot_tn(w_lo, pick)
            wide = _dot(rep_ref[...], col.astype(BF16)).reshape(PEER_HK, sub, 128)
            v = lax.bitcast_convert_type(tile_rows(t) & jnp.uint32(0xFFFF0000), F32)
            out_ref[t] = h_ref[t] + jnp.sum(wide * v, axis=0)
        return carry

    lax.fori_loop(0, PEER_TB // PEER_UNROLL, mix_body, 0)

    @pl.when(i == n - 1)
    def _():
        pltpu.make_async_copy(tab_hbm.at[pl.ds(0, rows)], gbuf.at[1 - slot], sem.at[1 - slot]).wait()


def _pack_tables(u_tab, v_tab):
    half = lambda t: lax.bitcast_convert_type(t.astype(BF16), jnp.uint16).astype(jnp.uint32)
    packed = half(u_tab) | (half(v_tab) << 16)
    return packed.reshape(u_tab.shape[0], D_MODEL // 128, 128)


def _peer(idx, gate, xn, h1, u_tab, v_tab):
    n = h1.shape[0]
    tb = PEER_TB
    steps = n // tb
    rows = tb * PEER_HK
    sub = D_MODEL // 128
    tiles = lambda t: t.reshape(n, sub, 128)
    e = np.arange(PEER_HK)
    sel = jnp.asarray(e[:, None] == (np.arange(PEER_HK * sub)[None, :] // sub), BF16)
    smem = lambda f: pl.BlockSpec((tb, PEER_HK), f, memory_space=pltpu.SMEM)
    tok = pl.BlockSpec((tb, sub, 128), lambda i: (i, 0, 0))
    out = pl.pallas_call(
        _peer_kernel,
        grid=(steps,),
        in_specs=[smem(lambda i: (i, 0)),
                  smem(lambda i: (jnp.minimum(i + 1, steps - 1), 0)),
                  pl.BlockSpec((tb, PEER_HK), lambda i: (i, 0)),
                  tok, tok,
                  pl.BlockSpec((PEER_HK, PEER_HK * sub), lambda i: (0, 0)),
                  pl.BlockSpec((PEER_HK * sub, PEER_HK), lambda i: (0, 0)),
                  pl.BlockSpec(memory_space=pl.ANY)],
        out_specs=tok,
        out_shape=jax.ShapeDtypeStruct((n, sub, 128), F32),
        scratch_shapes=[pltpu.VMEM((2, rows, sub, 128), jnp.uint32), pltpu.VMEM((tb, PEER_HK), F32),
                        pltpu.SemaphoreType.DMA((2,))],
        compiler_params=pltpu.CompilerParams(dimension_semantics=("arbitrary",),
                                             vmem_limit_bytes=VMEM_LIMIT),
        name="peer",
    )(idx, idx, gate, tiles(xn), tiles(h1), sel, sel.T, _pack_tables(u_tab, v_tab))
    return out.reshape(n, D_MODEL)


def _constants():
    r = np.arange(WIDTH)
    ph = (r[:, None] // HEAD_DIM == r[None, :] // HEAD_DIM).astype(np.float32) / HEAD_DIM
    s = np.arange(BLK)
    later = (s[:, None] > s[None, :]).astype(np.float32)
    half = np.concatenate([later, np.ones((BLK, BLK), np.float32)], axis=1)
    tri = np.concatenate([half, half], axis=0)
    return jnp.asarray(ph, BF16), jnp.asarray(tri, BF16)


def _layer(x, meta_tokens, norm1_g, w_in, sb_q_norm_g, sb_k_norm_g, dn_conv_w, dn_a_log, dn_dt_bias,
           dn_out_norm_g, w_sb_out, w_dn_out, w_o, norm2_g, peer_w_q, peer_sub_keys, peer_u, peer_v):
    batch, seq, d = x.shape
    lp = seq + BLK
    ph, tri = _constants()

    tail = jnp.concatenate([jnp.zeros((PAD, d), x.dtype), meta_tokens.astype(x.dtype)], axis=0)
    hp = jnp.concatenate([x, jnp.broadcast_to(tail[None], (batch, BLK, d))], axis=1).reshape(batch * lp, d)

    c_ba = 3 * WIDTH + 3 * WIDTH + WIDTH
    w_all = jnp.concatenate([w_in[:, :c_ba], w_in[:, c_ba + 2 * HEADS:], w_in[:, c_ba:c_ba + 2 * HEADS],
                             jnp.zeros((d, 128 - 2 * HEADS), w_in.dtype)], axis=1).astype(BF16)
    gq = (jnp.tile(sb_q_norm_g.astype(F32), HEADS) * (HEAD_DIM ** -0.5))[None]
    gk = jnp.tile(sb_k_norm_g.astype(F32), HEADS)[None]
    lane_pad = lambda t: jnp.pad(t.astype(F32), (HEADS, 128 - 2 * HEADS))[None]
    acoef = lane_pad(-jnp.exp(dn_a_log.astype(F32)))
    dtb = lane_pad(dn_dt_bias)

    sbqkv, dnqkv, z, gates, bg = _inproj(hp, norm1_g.astype(F32)[None], w_all, ph, gq, gk, acoef, dtb)
    o_sb = _sb_attn(sbqkv, tri, batch, lp)
    gn = jnp.tile(dn_out_norm_g.astype(F32), 2)[None]
    o_dn = _deltanet(dnqkv, z, bg, dn_conv_w.astype(F32), gn, batch, lp)
    h1 = _merge(o_sb, o_dn, gates, x.reshape(batch * seq, d), w_sb_out.astype(BF16),
                w_dn_out.astype(BF16), w_o.astype(BF16), batch, seq, lp)
    xn2, idx_t, gate_t = _route(h1, norm2_g.astype(F32)[None], peer_w_q.astype(BF16), peer_sub_keys)
    h2 = _peer(idx_t.T, gate_t.T, xn2, h1, peer_u, peer_v)
    return h2.reshape(batch, seq, d)


def kernel(x, meta_tokens, norm1_g, w_in, sb_q_norm_g, sb_k_norm_g, dn_conv_w, dn_a_log, dn_dt_bias,
           dn_out_norm_g, w_sb_out, w_dn_out, w_o, norm2_g, peer_w_q, peer_sub_keys, peer_u, peer_v):
    assert norm1_g.shape[0] == 1, "one layer"
    return _layer(x, meta_tokens, norm1_g[0], w_in[0], sb_q_norm_g[0], sb_k_norm_g[0], dn_conv_w[0],
                  dn_a_log[0], dn_dt_bias[0], dn_out_norm_g[0], w_sb_out[0], w_dn_out[0], w_o[0],
                  norm2_g[0], peer_w_q[0], peer_sub_keys[0], peer_u[0], peer_v[0])
```

```python
import functools

import numpy as np
import jax
import jax.numpy as jnp
from jax import lax
from jax.experimental import pallas as pl
from jax.experimental.pallas import tpu as pltpu

F32 = jnp.float32
BF16 = jnp.bfloat16

D_MODEL = 1024
N_META = 16
BLK = 128
PAD = BLK - N_META
HEADS = 8
HEAD_DIM = 64
WIDTH = HEADS * HEAD_DIM
CHUNK = 64
PEER_HEADS = 8
PEER_KEYS = 128
PEER_TOPK = 16
PEER_HK = PEER_HEADS * PEER_TOPK
EPS = 1e-6

C_SB = 0
C_DN = 3 * WIDTH
C_Z = C_DN + 3 * WIDTH
C_GATE = C_Z + WIDTH
C_BA = C_GATE + 2 * D_MODEL
C_END = C_BA + 128

ROWS_IN = 256
ROWS_OUT = 256
SB_SPAN = 4
DN_GROUP = 2
PEER_TB = 16
PEER_UNROLL = 4
VMEM_LIMIT = 56 * 1024 * 1024


def _dot(a, b):
    return jnp.dot(a, b, preferred_element_type=F32)


def _dot_nt(a, b):
    return lax.dot_general(a, b, (((1,), (1,)), ((), ())), preferred_element_type=F32)


def _dot_tn(a, b):
    return lax.dot_general(a, b, (((0,), (0,)), ((), ())), preferred_element_type=F32)


def _split(a):
    hi = a.astype(BF16)
    lo = (a - hi.astype(F32)).astype(BF16)
    return hi, lo


def _dot_xr(a, b_exact):
    hi, lo = _split(a)
    return _dot(jnp.concatenate([hi, lo], axis=1), jnp.concatenate([b_exact, b_exact], axis=0))


def _dot_xl(a_exact, b):
    hi, lo = _split(b)
    return _dot(jnp.concatenate([a_exact, a_exact], axis=1), jnp.concatenate([hi, lo], axis=0))


def _dot3(a, b):
    ah, al = _split(a)
    bh, bl = _split(b)
    return _dot(jnp.concatenate([ah, ah, al], axis=1), jnp.concatenate([bh, bl, bh], axis=0))


def _sigmoid(x):
    return 1.0 / (1.0 + jnp.exp(-x))


def _softplus(x):
    return jnp.maximum(x, 0.0) + jnp.log1p(jnp.exp(-jnp.abs(x)))


def _iota(shape, dim):
    return lax.broadcasted_iota(jnp.int32, shape, dim)


def _inproj_kernel(x_ref, g1_ref, w_ref, ph_ref, gq_ref, gk_ref, acoef_ref, dtb_ref,
                   sb_ref, dn_ref, z_ref, gate_ref, bg_ref):
    x = x_ref[...]
    ms = jnp.mean(x * x, axis=-1, keepdims=True)
    xn = (x * lax.rsqrt(ms + EPS) * g1_ref[...]).astype(BF16)

    def proj(c0, c1):
        return _dot(xn, w_ref[:, c0:c1])

    def head_norm(t, g):
        msh = _dot((t * t).astype(BF16), ph_ref[...])
        return (t * lax.rsqrt(msh + EPS) * g).astype(BF16)

    sb_ref[:, 0:WIDTH] = head_norm(proj(C_SB, C_SB + WIDTH), gq_ref[...])
    sb_ref[:, WIDTH:2 * WIDTH] = head_norm(proj(C_SB + WIDTH, C_SB + 2 * WIDTH), gk_ref[...])
    sb_ref[:, 2 * WIDTH:3 * WIDTH] = proj(C_SB + 2 * WIDTH, C_SB + 3 * WIDTH).astype(BF16)
    dn_ref[...] = proj(C_DN, C_DN + 3 * WIDTH)
    z_ref[...] = proj(C_Z, C_Z + WIDTH)
    gate_ref[...] = _sigmoid(proj(C_GATE, C_GATE + 2 * D_MODEL)).astype(BF16)
    ba = proj(C_BA, C_END)
    lane = _iota(ba.shape, 1)
    bg_ref[...] = jnp.where(lane < HEADS, _sigmoid(ba), acoef_ref[...] * _softplus(ba + dtb_ref[...]))


def _inproj(hp, g1, w_all, ph, gq, gk, acoef, dtb):
    n = hp.shape[0]
    rb = ROWS_IN
    const = lambda shape: pl.BlockSpec(shape, lambda i: (0, 0))
    row = lambda c: pl.BlockSpec((rb, c), lambda i: (i, 0))
    return pl.pallas_call(
        _inproj_kernel,
        grid=(n // rb,),
        in_specs=[row(D_MODEL), const((1, D_MODEL)), const((D_MODEL, C_END)), const((WIDTH, WIDTH)),
                  const((1, WIDTH)), const((1, WIDTH)), const((1, 128)), const((1, 128))],
        out_specs=[row(3 * WIDTH), row(3 * WIDTH), row(WIDTH), row(2 * D_MODEL), row(128)],
        out_shape=[jax.ShapeDtypeStruct((n, 3 * WIDTH), BF16),
                   jax.ShapeDtypeStruct((n, 3 * WIDTH), F32),
                   jax.ShapeDtypeStruct((n, WIDTH), F32),
                   jax.ShapeDtypeStruct((n, 2 * D_MODEL), BF16),
                   jax.ShapeDtypeStruct((n, 128), F32)],
        compiler_params=pltpu.CompilerParams(dimension_semantics=("parallel",),
                                             vmem_limit_bytes=VMEM_LIMIT),
        name="inproj",
    )(hp, g1, w_all, ph, gq, gk, acoef, dtb)


def _sb_kernel(q_ref, k_ref, v_ref, tri_ref, o_ref, *, nblk):
    qi = pl.program_id(2)
    q = q_ref[0]
    lane = _iota((BLK, BLK), 1)
    row = _iota((BLK, BLK), 0)
    head_lo = lane < HEAD_DIM
    qf = q.astype(F32)
    q2 = jnp.concatenate([jnp.where(head_lo, qf, 0.0), jnp.where(head_lo, 0.0, qf)], axis=0).astype(BF16)
    tri = tri_ref[...]
    two = 2 * BLK

    def tile(kphys, nb, carry, acc, vis):
        start = pl.multiple_of(kphys * BLK, BLK)
        k = k_ref[0, pl.ds(start, nb * BLK), :]
        v = v_ref[0, pl.ds(start, nb * BLK), :]
        cols = lambda t, j: t[:, j * BLK:(j + 1) * BLK]
        z = _dot_nt(q2, k)
        soft = jnp.log(1.0 + jnp.exp(-jnp.abs(z)))
        log_beta = jnp.minimum(z, 0.0) - soft
        log_om = -jnp.maximum(z, 0.0) - soft
        if vis is not None:
            log_om = jnp.where(vis, log_om, 0.0)
        hi, lo = _split(log_om)
        stacked = jnp.concatenate(
            [jnp.concatenate([cols(hi, j), cols(lo, j)], axis=1) for j in range(nb)], axis=0)
        rt = _dot(stacked, tri)
        later = carry
        parts = [None] * nb
        for j in reversed(range(nb)):
            rows = rt[j * two:(j + 1) * two]
            parts[j] = cols(log_beta, j) + rows[:, :BLK] + later
            later = later + rows[:, BLK:]
        a = jnp.exp(jnp.concatenate(parts, axis=1))
        if vis is not None:
            a = jnp.where(vis, a, 0.0)
        return later, acc + _dot(a.astype(BF16), v)

    stack2 = lambda m: jnp.concatenate([m, m], axis=0)
    zf = jnp.zeros((two, BLK), F32)
    carry, acc = tile(qi, 1, zf, zf, stack2(lane < row))

    rem = qi % SB_SPAN

    def single(it, state):
        return tile(qi - 1 - it, 1, state[0], state[1], None)

    def span(it, state):
        return tile(qi - rem - SB_SPAN * (it + 1), SB_SPAN, state[0], state[1], None)

    carry, acc = lax.fori_loop(0, rem, single, (carry, acc))
    carry, acc = lax.fori_loop(0, qi // SB_SPAN, span, (carry, acc))
    carry, acc = tile(nblk - 1, 1, carry, acc, stack2(lane >= PAD))
    o_ref[0] = jnp.where(head_lo, acc[:BLK], acc[BLK:]).astype(o_ref.dtype)


def _sb_attn(sbqkv, tri, batch, lp):
    nblk = lp // BLK
    nq = nblk - 1
    pairs = WIDTH // BLK
    x3 = sbqkv.reshape(batch, lp, 3 * WIDTH)
    return pl.pallas_call(
        functools.partial(_sb_kernel, nblk=nblk),
        grid=(batch, pairs, nq),
        in_specs=[pl.BlockSpec((1, BLK, BLK), lambda b, p, i: (b, i, p)),
                  pl.BlockSpec((1, lp, BLK), lambda b, p, i: (b, 0, pairs + p)),
                  pl.BlockSpec((1, lp, BLK), lambda b, p, i: (b, 0, 2 * pairs + p)),
                  pl.BlockSpec((2 * BLK, 2 * BLK), lambda b, p, i: (0, 0))],
        out_specs=pl.BlockSpec((1, BLK, BLK), lambda b, p, i: (b, i, p)),
        out_shape=jax.ShapeDtypeStruct((batch, nq * BLK, WIDTH), BF16),
        compiler_params=pltpu.CompilerParams(
            dimension_semantics=("parallel", "parallel", "arbitrary"), vmem_limit_bytes=VMEM_LIMIT),
        name="sb_attn",
    )(x3, x3, x3, tri)


def _dn_kernel(q_ref, k_ref, v_ref, z_ref, bg_ref, cwq_ref, cwk_ref, cwv_ref, gn_ref,
               o_ref, xbuf, s_ref):
    j = pl.program_id(2)

    @pl.when(j == 0)
    def _():
        xbuf[...] = jnp.zeros_like(xbuf)
        s_ref[...] = jnp.zeros_like(s_ref)

    def conv_silu(c, x_ref, cw_ref):
        xbuf[c, 8:8 + BLK, :] = x_ref[0]
        w = cw_ref[...]
        y = (w[3:4] * xbuf[c, 8:8 + BLK, :] + w[2:3] * xbuf[c, 7:7 + BLK, :]
             + w[1:2] * xbuf[c, 6:6 + BLK, :] + w[0:1] * xbuf[c, 5:5 + BLK, :])
        xbuf[c, 0:8, :] = xbuf[c, BLK:BLK + 8, :]
        return y * _sigmoid(y)

    lane = _iota((BLK, BLK), 1)
    row = _iota((BLK, BLK), 0)
    lane_head = lane >> 6
    row_head = row >> 6

    def pattern(cond):
        return jnp.where(cond, 1.0, 0.0).astype(BF16)

    same_head = pattern(lane_head == row_head)
    low_incl = (lane_head == row_head) & (lane <= row)
    low_strict = (lane_head == row_head) & (lane < row)
    cum_incl = pattern(low_incl)

    def l2(t):
        ss = _dot_xr(t * t, same_head)
        return t * lax.rsqrt(ss + EPS)

    q_all = conv_silu(0, q_ref, cwq_ref)
    k_all = conv_silu(1, k_ref, cwk_ref)
    v_all = conv_silu(2, v_ref, cwv_ref)
    shared = dict(j=j, bg=bg_ref[0], lane=lane, row=row, lane_head=lane_head, row_head=row_head,
                  pattern=pattern, same_head=same_head, low_incl=low_incl, low_strict=low_strict,
                  cum_incl=cum_incl, l2=l2)
    for pr in range(DN_GROUP):
        sl = slice(pr * BLK, (pr + 1) * BLK)
        _dn_pair(pr, sl, pl.program_id(1) * DN_GROUP + pr, q_all[:, sl], k_all[:, sl], v_all[:, sl],
                 z_ref, gn_ref, o_ref, s_ref, **shared)


def _dn_pair(pr, sl, hp, q, k, v, z_ref, gn_ref, o_ref, s_ref, *, j, bg, lane, row, lane_head, row_head,
             pattern, same_head, low_incl, low_strict, cum_incl, l2):
    qk_n = l2(jnp.concatenate([q, k], axis=0))
    q = qk_n[:BLK] * (HEAD_DIM ** -0.5)
    k = qk_n[BLK:]
    live = (row >= PAD) | (j > 0)
    picks = jnp.concatenate([pattern(row == 2 * hp + lane_head),
                             pattern(row == HEADS + 2 * hp + lane_head)], axis=1)
    beta_g = _dot_xr(bg, picks)
    beta = jnp.where(live, beta_g[:, :BLK], 0.0)
    g = jnp.where(live, beta_g[:, BLK:], 0.0)
    sums = _dot_xl(jnp.concatenate([cum_incl, same_head], axis=0), g)
    gc = sums[:BLK]
    g_last = sums[BLK:]
    eg = jnp.exp(gc)
    k_beta = k * beta
    v_beta = v * beta
    kbg = k_beta * eg
    q_decay = q * eg
    k_tail = k * jnp.exp(g_last - gc)

    eye = jnp.where(lane == row, 1.0, 0.0).astype(F32)
    gc_heads = _dot_xr(gc, jnp.concatenate([pattern(row == 0), pattern(row == HEAD_DIM)], axis=1))
    k16 = k.astype(BF16)
    rhs_uw = jnp.concatenate([v_beta, kbg], axis=1)
    u_parts, w_parts, a_parts = [], [], []
    for h in range(2):
        mine = lane_head == h
        gch = gc_heads[:, h * BLK:(h + 1) * BLK]
        decay = jnp.exp(jnp.where(low_incl, gch - gch.T, -jnp.inf))
        both = _dot_nt(jnp.concatenate([jnp.where(mine, k_beta, 0.0), jnp.where(mine, q, 0.0)],
                                       axis=0).astype(BF16), k16)
        m = jnp.where(low_strict, both[:BLK] * decay, 0.0)
        a_parts.append(jnp.where(low_incl, both[BLK:] * decay, 0.0))
        a_k = -m
        s_k = eye + a_k
        a_k = _dot3(a_k, a_k)
        for _ in range(4):
            prod = _dot3(a_k, jnp.concatenate([a_k, s_k], axis=1))
            a_k, s_k = prod[:, :BLK], s_k + prod[:, BLK:]
        t_inv = s_k + _dot3(a_k, s_k)
        uw = _dot3(t_inv, rhs_uw)
        u_parts.append(uw[:, :BLK])
        w_parts.append(uw[:, BLK:])
    head0 = lane_head == 0
    u = jnp.where(head0, u_parts[0], u_parts[1])
    w = jnp.where(head0, w_parts[0], w_parts[1])

    block_diag = lane_head == row_head
    state = s_ref[pr]
    vn_parts, inter_parts = [], []
    for c in range(2):
        r = slice(c * CHUNK, (c + 1) * CHUNK)
        ws_qs = _dot(jnp.concatenate([w[r], q_decay[r]], axis=0).astype(BF16), state.astype(BF16))
        vn = u[r] - ws_qs[:CHUNK]
        inter_parts.append(ws_qs[CHUNK:])
        gl = jnp.exp(g_last[c * CHUNK:c * CHUNK + 1, :])
        upd = _dot_tn(k_tail[r].astype(BF16), vn.astype(BF16))
        state = jnp.where(block_diag, state * gl + upd, 0.0)
        vn_parts.append(vn)
    s_ref[pr] = state
    vn = jnp.concatenate(vn_parts, axis=0)
    o = jnp.concatenate(inter_parts, axis=0) + _dot(
        jnp.concatenate(a_parts, axis=1).astype(BF16),
        jnp.concatenate([jnp.where(head0, vn, 0.0), jnp.where(head0, 0.0, vn)], axis=0).astype(BF16))

    ms = _dot_xr(o * o, same_head) * (1.0 / HEAD_DIM)
    zz = z_ref[0, :, sl]
    o_ref[0, :, sl] = (o * lax.rsqrt(ms + EPS) * gn_ref[...] * (zz * _sigmoid(zz))).astype(o_ref.dtype)


def _deltanet(dnqkv, z, bg, conv_w, gn, batch, lp):
    nblk = lp // BLK
    grp = DN_GROUP * BLK
    pairs = WIDTH // grp
    phys = lambda j: (j + nblk - 1) % nblk
    x3 = dnqkv.reshape(batch, lp, 3 * WIDTH)
    z3 = z.reshape(batch, lp, WIDTH)
    bg3 = bg.reshape(batch, lp, 128)
    col = lambda off: pl.BlockSpec((1, BLK, grp), lambda b, p, j: (b, phys(j), off * pairs + p))
    cw = lambda off: pl.BlockSpec((4, grp), lambda b, p, j: (0, off * pairs + p))
    return pl.pallas_call(
        _dn_kernel,
        grid=(batch, pairs, nblk),
        in_specs=[col(0), col(1), col(2),
                  pl.BlockSpec((1, BLK, grp), lambda b, p, j: (b, phys(j), p)),
                  pl.BlockSpec((1, BLK, 128), lambda b, p, j: (b, phys(j), 0)),
                  cw(0), cw(1), cw(2),
                  pl.BlockSpec((1, BLK), lambda b, p, j: (0, 0))],
        out_specs=pl.BlockSpec((1, BLK, grp), lambda b, p, j: (b, jnp.maximum(j - 1, 0), p)),
        out_shape=jax.ShapeDtypeStruct((batch, (nblk - 1) * BLK, WIDTH), BF16),
        scratch_shapes=[pltpu.VMEM((3, BLK + 8, grp), F32), pltpu.VMEM((DN_GROUP, BLK, BLK), F32)],
        compiler_params=pltpu.CompilerParams(
            dimension_semantics=("parallel", "parallel", "arbitrary"), vmem_limit_bytes=VMEM_LIMIT),
        name="deltanet",
    )(x3, x3, x3, z3, bg3, conv_w, conv_w, conv_w, gn)


def _merge_kernel(osb_ref, odn_ref, gate_ref, x_ref, wsb_ref, wdn_ref, wo_ref, h_ref):
    gate = gate_ref[0].astype(F32)
    mix = (gate[:, :D_MODEL] * _dot(osb_ref[...], wsb_ref[...])
           + gate[:, D_MODEL:] * _dot(odn_ref[...], wdn_ref[...]))
    h_ref[...] = x_ref[...] + _dot(mix.astype(BF16), wo_ref[...])


def _merge(o_sb, o_dn, gates, x2, wsb, wdn, wo, batch, seq, lp):
    n = batch * seq
    rb = ROWS_OUT
    per = seq // rb
    row = lambda c: pl.BlockSpec((rb, c), lambda i: (i, 0))
    const = lambda r, c: pl.BlockSpec((r, c), lambda i: (0, 0))
    return pl.pallas_call(
        _merge_kernel,
        grid=(n // rb,),
        in_specs=[row(WIDTH), row(WIDTH),
                  pl.BlockSpec((1, rb, 2 * D_MODEL), lambda i: (i // per, i % per, 0)),
                  row(D_MODEL), const(WIDTH, D_MODEL), const(WIDTH, D_MODEL), const(D_MODEL, D_MODEL)],
        out_specs=row(D_MODEL),
        out_shape=jax.ShapeDtypeStruct((n, D_MODEL), F32),
        compiler_params=pltpu.CompilerParams(dimension_semantics=("parallel",),
                                             vmem_limit_bytes=VMEM_LIMIT),
        name="merge",
    )(o_sb.reshape(n, WIDTH), o_dn.reshape(n, WIDTH), gates.reshape(batch, lp, 2 * D_MODEL),
      x2, wsb, wdn, wo)


def _top_rows(s, k, payload=None):
    nrow = s.shape[0]
    rid = _iota(s.shape, 0)
    vals, picks = [], []
    for _ in range(k):
        m = jnp.max(s, axis=0, keepdims=True)
        am = jnp.min(jnp.where(s == m, rid, nrow), axis=0, keepdims=True)
        hit = rid == am
        vals.append(m)
        picks.append(am if payload is None
                     else jnp.max(jnp.where(hit, payload, -1), axis=0, keepdims=True))
        s = jnp.where(hit, -jnp.inf, s)
    return jnp.concatenate(vals, axis=0), jnp.concatenate(picks, axis=0)


def _route_kernel(h_ref, g2_ref, wq_ref, keys_ref, xn_ref, idx_ref, gate_ref):
    x = h_ref[...]
    ms = jnp.mean(x * x, axis=-1, keepdims=True)
    xn = x * lax.rsqrt(ms + EPS) * g2_ref[...]
    xn_ref[...] = xn
    q = _dot(xn.astype(BF16), wq_ref[...])
    keys = (keys_ref[0].astype(BF16), keys_ref[1].astype(BF16))
    idx_rows, gate_rows = [], []
    for h in range(PEER_HEADS):
        tops = []
        for p in range(2):
            c0 = (2 * h + p) * PEER_KEYS
            s = _dot_nt(keys[p], q[:, c0:c0 + PEER_KEYS].astype(BF16))
            tops.append(_top_rows(s, PEER_TOPK))
        (s1, i1), (s2, i2) = tops
        brow = _iota((8, s1.shape[1]), 0)
        cs, ci = [s1[0:1] + s2], [i1[0:1] * PEER_KEYS + i2]
        for a in range(1, 8):
            keep = PEER_TOPK // (a + 1)
            sa = s1[a:a + 1] + s2[0:8]
            cs.append(sa if keep >= 8 else jnp.where(brow < keep, sa, -jnp.inf))
            ci.append(i1[a:a + 1] * PEER_KEYS + i2[0:8])
        cs.append(s1[8:16] + s2[0:1])
        ci.append(i1[8:16] * PEER_KEYS + i2[0:1])
        top_s, top_i = _top_rows(jnp.concatenate(cs, axis=0), PEER_TOPK, jnp.concatenate(ci, axis=0))
        e = jnp.exp(top_s - top_s[0:1])
        idx_rows.append(top_i)
        gate_rows.append(e / jnp.sum(e, axis=0, keepdims=True))
    idx_ref[...] = jnp.concatenate(idx_rows, axis=0)
    gate_ref[...] = jnp.concatenate(gate_rows, axis=0)


def _route(h1, g2, wq, keys):
    n = h1.shape[0]
    rb = ROWS_OUT
    return pl.pallas_call(
        _route_kernel,
        grid=(n // rb,),
        in_specs=[pl.BlockSpec((rb, D_MODEL), lambda i: (i, 0)),
                  pl.BlockSpec((1, D_MODEL), lambda i: (0, 0)),
                  pl.BlockSpec(wq.shape, lambda i: (0, 0)),
                  pl.BlockSpec(keys.shape, lambda i: (0, 0, 0))],
        out_specs=[pl.BlockSpec((rb, D_MODEL), lambda i: (i, 0)),
                   pl.BlockSpec((PEER_HK, rb), lambda i: (0, i)),
                   pl.BlockSpec((PEER_HK, rb), lambda i: (0, i))],
        out_shape=[jax.ShapeDtypeStruct((n, D_MODEL), F32),
                   jax.ShapeDtypeStruct((PEER_HK, n), jnp.int32),
                   jax.ShapeDtypeStruct((PEER_HK, n), F32)],
        compiler_params=pltpu.CompilerParams(dimension_semantics=("parallel",),
                                             vmem_limit_bytes=VMEM_LIMIT),
        name="route",
    )(h1, g2, wq, keys)


def _peer_kernel(idx_ref, idx_next_ref, gate_ref, xn_ref, h_ref, sel_ref, rep_ref, tab_hbm, out_ref,
                 gbuf, act_ref, sem):
    i = pl.program_id(0)
    n = pl.num_programs(0)
    slot = i % 2
    rows = PEER_TB * PEER_HK
    sub = D_MODEL // 128

    def issue_token(src_idx, t, sl, e0=0, e1=PEER_HK):
        base = t * PEER_HK
        for e in range(e0, e1):
            pltpu.make_async_copy(tab_hbm.at[src_idx[t, e]], gbuf.at[sl, base + e], sem.at[sl]).start()

    @pl.when(i == 0)
    def _():
        def body(t, carry):
            issue_token(idx_ref, t, 0)
            return carry
        lax.fori_loop(0, PEER_TB, body, 0)

    pltpu.make_async_copy(tab_hbm.at[pl.ds(0, rows)], gbuf.at[slot], sem.at[slot]).wait()

    unroll = PEER_UNROLL
    ones = jnp.ones((sub, 2 * 128), BF16)
    lanes = lambda t, k: t[:, k * 128:(k + 1) * 128]

    def tile_rows(t):
        return gbuf[slot, pl.ds(pl.multiple_of(t * PEER_HK, PEER_HK), PEER_HK)]

    def act_body(g, carry):
        prods = []
        for k in range(unroll):
            t = g * unroll + k
            issue_token(idx_next_ref, t, 1 - slot, 0, PEER_HK // 2)
            u = lax.bitcast_convert_type(tile_rows(t) << 16, F32)
            prods.append((u * xn_ref[t][None]).reshape(PEER_HK * sub, 128).astype(BF16))
        part = _dot(sel_ref[...], jnp.concatenate(prods, axis=1))
        hi, lo = _split(jnp.concatenate([lanes(part, k) for k in range(unroll)], axis=0))
        sums = _dot_nt(ones, jnp.concatenate([hi, lo], axis=1))
        for k in range(unroll):
            act_ref[pl.ds(g * unroll + k, 1), :] = lanes(sums, k)[0:1]
        return carry

    lax.fori_loop(0, PEER_TB // unroll, act_body, 0)

    act = act_ref[...]
    wgt = (0.5 * act * (1.0 + lax.erf(act * (2.0 ** -0.5))) * gate_ref[...]).astype(BF16)
    trow = _iota((PEER_TB, unroll * 128), 0)
    lane_tok = _iota((PEER_TB, unroll * 128), 1) >> 7

    def mix_body(g, carry):
        pick = jnp.where(trow == g * unroll + lane_tok, 1.0, 0.0).astype(BF16)
        col = _dot_tn(wgt, pick)
        wide = _dot(rep_ref[...], col.astype(BF16))
        for k in range(unroll):
            t = g * unroll + k
            issue_token(idx_next_ref, t, 1 - slot, PEER_HK // 2, PEER_HK)
            v = lax.bitcast_convert_type(tile_rows(t) & jnp.uint32(0xFFFF0000), F32)
            out_ref[t] = h_ref[t] + jnp.sum(lanes(wide, k).reshape(PEER_HK, sub, 128) * v, axis=0)
        return carry

    lax.fori_loop(0, PEER_TB // unroll, mix_body, 0)

    @pl.when(i == n - 1)
    def _():
        pltpu.make_async_copy(tab_hbm.at[pl.ds(0, rows)], gbuf.at[1 - slot], sem.at[1 - slot]).wait()


def _pack_tables(u_tab, v_tab):
    half = lambda t: lax.bitcast_convert_type(t.astype(BF16), jnp.uint16).astype(jnp.uint32)
    packed = half(u_tab) | (half(v_tab) << 16)
    return packed.reshape(u_tab.shape[0], D_MODEL // 128, 128)


def _peer(idx, gate, xn, h1, u_tab, v_tab):
    n = h1.shape[0]
    tb = PEER_TB
    steps = n // tb
    rows = tb * PEER_HK
    sub = D_MODEL // 128
    tiles = lambda t: t.reshape(n, sub, 128)
    e = np.arange(PEER_HK)
    sel = jnp.asarray(e[:, None] == (np.arange(PEER_HK * sub)[None, :] // sub), BF16)
    smem = lambda f: pl.BlockSpec((tb, PEER_HK), f, memory_space=pltpu.SMEM)
    tok = pl.BlockSpec((tb, sub, 128), lambda i: (i, 0, 0))
    out = pl.pallas_call(
        _peer_kernel,
        grid=(steps,),
        in_specs=[smem(lambda i: (i, 0)),
                  smem(lambda i: (jnp.minimum(i + 1, steps - 1), 0)),
                  pl.BlockSpec((tb, PEER_HK), lambda i: (i, 0)),
                  tok, tok,
                  pl.BlockSpec((PEER_HK, PEER_HK * sub), lambda i: (0, 0)),
                  pl.BlockSpec((PEER_HK * sub, PEER_HK), lambda i: (0, 0)),
                  pl.BlockSpec(memory_space=pl.ANY)],
        out_specs=tok,
        out_shape=jax.ShapeDtypeStruct((n, sub, 128), F32),
        scratch_shapes=[pltpu.VMEM((2, rows, sub, 128), jnp.uint32), pltpu.VMEM((tb, PEER_HK), F32),
                        pltpu.SemaphoreType.DMA((2,))],
        compiler_params=pltpu.CompilerParams(dimension_semantics=("arbitrary",),
                                             vmem_limit_bytes=VMEM_LIMIT),
        name="peer",
    )(idx, idx, gate, tiles(xn), tiles(h1), sel, sel.T, _pack_tables(u_tab, v_tab))
    return out.reshape(n, D_MODEL)


def _constants():
    r = np.arange(WIDTH)
    ph = (r[:, None] // HEAD_DIM == r[None, :] // HEAD_DIM).astype(np.float32) / HEAD_DIM
    s = np.arange(BLK)
    later = (s[:, None] > s[None, :]).astype(np.float32)
    half = np.concatenate([later, np.ones((BLK, BLK), np.float32)], axis=1)
    tri = np.concatenate([half, half], axis=0)
    return jnp.asarray(ph, BF16), jnp.asarray(tri, BF16)


def _layer(x, meta_tokens, norm1_g, w_in, sb_q_norm_g, sb_k_norm_g, dn_conv_w, dn_a_log, dn_dt_bias,
           dn_out_norm_g, w_sb_out, w_dn_out, w_o, norm2_g, peer_w_q, peer_sub_keys, peer_u, peer_v):
    batch, seq, d = x.shape
    lp = seq + BLK
    ph, tri = _constants()

    tail = jnp.concatenate([jnp.zeros((PAD, d), x.dtype), meta_tokens.astype(x.dtype)], axis=0)
    hp = jnp.concatenate([x, jnp.broadcast_to(tail[None], (batch, BLK, d))], axis=1).reshape(batch * lp, d)

    c_ba = 3 * WIDTH + 3 * WIDTH + WIDTH
    w_all = jnp.concatenate([w_in[:, :c_ba], w_in[:, c_ba + 2 * HEADS:], w_in[:, c_ba:c_ba + 2 * HEADS],
                             jnp.zeros((d, 128 - 2 * HEADS), w_in.dtype)], axis=1).astype(BF16)
    gq = (jnp.tile(sb_q_norm_g.astype(F32), HEADS) * (HEAD_DIM ** -0.5))[None]
    gk = jnp.tile(sb_k_norm_g.astype(F32), HEADS)[None]
    lane_pad = lambda t: jnp.pad(t.astype(F32), (HEADS, 128 - 2 * HEADS))[None]
    acoef = lane_pad(-jnp.exp(dn_a_log.astype(F32)))
    dtb = lane_pad(dn_dt_bias)

    sbqkv, dnqkv, z, gates, bg = _inproj(hp, norm1_g.astype(F32)[None], w_all, ph, gq, gk, acoef, dtb)
    o_sb = _sb_attn(sbqkv, tri, batch, lp)
    gn = jnp.tile(dn_out_norm_g.astype(F32), 2)[None]
    o_dn = _deltanet(dnqkv, z, bg, dn_conv_w.astype(F32), gn, batch, lp)
    h1 = _merge(o_sb, o_dn, gates, x.reshape(batch * seq, d), w_sb_out.astype(BF16),
                w_dn_out.astype(BF16), w_o.astype(BF16), batch, seq, lp)
    xn2, idx_t, gate_t = _route(h1, norm2_g.astype(F32)[None], peer_w_q.astype(BF16), peer_sub_keys)
    h2 = _peer(idx_t.T, gate_t.T, xn2, h1, peer_u, peer_v)
    return h2.reshape(batch, seq, d)


def kernel(x, meta_tokens, norm1_g, w_in, sb_q_norm_g, sb_k_norm_g, dn_conv_w, dn_a_log, dn_dt_bias,
           dn_out_norm_g, w_sb_out, w_dn_out, w_o, norm2_g, peer_w_q, peer_sub_keys, peer_u, peer_v):
    assert norm1_g.shape[0] == 1, "one layer"
    return _layer(x, meta_tokens, norm1_g[0], w_in[0], sb_q_norm_g[0], sb_k_norm_g[0], dn_conv_w[0],
                  dn_a_log[0], dn_dt_bias[0], dn_out_norm_g[0], w_sb_out[0], w_dn_out[0], w_o[0],
                  norm2_g[0], peer_w_q[0], peer_sub_keys[0], peer_u[0], peer_v[0])
```

```python
import functools

import numpy as np
import jax
import jax.numpy as jnp
from jax import lax
from jax.experimental import pallas as pl
from jax.experimental.pallas import tpu as pltpu

F32 = jnp.float32
BF16 = jnp.bfloat16

D_MODEL = 1024
N_META = 16
BLK = 128
PAD = BLK - N_META
HEADS = 8
HEAD_DIM = 64
WIDTH = HEADS * HEAD_DIM
CHUNK = 64
PEER_HEADS = 8
PEER_KEYS = 128
PEER_TOPK = 16
PEER_HK = PEER_HEADS * PEER_TOPK
EPS = 1e-6

C_SB = 0
C_DN = 3 * WIDTH
C_Z = C_DN + 3 * WIDTH
C_GATE = C_Z + WIDTH
C_BA = C_GATE + 2 * D_MODEL
C_END = C_BA + 128

ROWS_IN = 256
ROWS_OUT = 256
SB_SPAN = 4
DN_GROUP = 4
DMA_THREADS = 2
PEER_TB = 16
PEER_UNROLL = 4
VMEM_LIMIT = 56 * 1024 * 1024


def _dot(a, b):
    return jnp.dot(a, b, preferred_element_type=F32)


def _dot_nt(a, b):
    return lax.dot_general(a, b, (((1,), (1,)), ((), ())), preferred_element_type=F32)


def _dot_tn(a, b):
    return lax.dot_general(a, b, (((0,), (0,)), ((), ())), preferred_element_type=F32)


def _split(a):
    hi = a.astype(BF16)
    lo = (a - hi.astype(F32)).astype(BF16)
    return hi, lo


def _dot_xr(a, b_exact):
    hi, lo = _split(a)
    return _dot(jnp.concatenate([hi, lo], axis=1), jnp.concatenate([b_exact, b_exact], axis=0))


def _dot_xl(a_exact, b):
    hi, lo = _split(b)
    return _dot(jnp.concatenate([a_exact, a_exact], axis=1), jnp.concatenate([hi, lo], axis=0))


def _dot3(a, b):
    ah, al = _split(a)
    bh, bl = _split(b)
    return _dot(jnp.concatenate([ah, ah, al], axis=1), jnp.concatenate([bh, bl, bh], axis=0))


def _sigmoid(x):
    return 1.0 / (1.0 + jnp.exp(-x))


def _softplus(x):
    return jnp.maximum(x, 0.0) + jnp.log1p(jnp.exp(-jnp.abs(x)))


def _iota(shape, dim):
    return lax.broadcasted_iota(jnp.int32, shape, dim)


def _inproj_kernel(x_ref, g1_ref, w_ref, ph_ref, gq_ref, gk_ref, acoef_ref, dtb_ref,
                   sb_ref, dn_ref, z_ref, gate_ref, bg_ref):
    x = x_ref[...]
    ms = jnp.mean(x * x, axis=-1, keepdims=True)
    xn = (x * lax.rsqrt(ms + EPS) * g1_ref[...]).astype(BF16)

    def proj(c0, c1):
        return _dot(xn, w_ref[:, c0:c1])

    def head_norm(t, g):
        msh = _dot((t * t).astype(BF16), ph_ref[...])
        return (t * lax.rsqrt(msh + EPS) * g).astype(BF16)

    sb_ref[:, 0:WIDTH] = head_norm(proj(C_SB, C_SB + WIDTH), gq_ref[...])
    sb_ref[:, WIDTH:2 * WIDTH] = head_norm(proj(C_SB + WIDTH, C_SB + 2 * WIDTH), gk_ref[...])
    sb_ref[:, 2 * WIDTH:3 * WIDTH] = proj(C_SB + 2 * WIDTH, C_SB + 3 * WIDTH).astype(BF16)
    dn_ref[...] = proj(C_DN, C_DN + 3 * WIDTH)
    z_ref[...] = proj(C_Z, C_Z + WIDTH)
    gate_ref[...] = _sigmoid(proj(C_GATE, C_GATE + 2 * D_MODEL)).astype(BF16)
    ba = proj(C_BA, C_END)
    lane = _iota(ba.shape, 1)
    bg_ref[...] = jnp.where(lane < HEADS, _sigmoid(ba), acoef_ref[...] * _softplus(ba + dtb_ref[...]))


def _inproj(hp, g1, w_all, ph, gq, gk, acoef, dtb):
    n = hp.shape[0]
    rb = ROWS_IN
    const = lambda shape: pl.BlockSpec(shape, lambda i: (0, 0))
    row = lambda c: pl.BlockSpec((rb, c), lambda i: (i, 0))
    return pl.pallas_call(
        _inproj_kernel,
        grid=(n // rb,),
        in_specs=[row(D_MODEL), const((1, D_MODEL)), const((D_MODEL, C_END)), const((WIDTH, WIDTH)),
                  const((1, WIDTH)), const((1, WIDTH)), const((1, 128)), const((1, 128))],
        out_specs=[row(3 * WIDTH), row(3 * WIDTH), row(WIDTH), row(2 * D_MODEL), row(128)],
        out_shape=[jax.ShapeDtypeStruct((n, 3 * WIDTH), BF16),
                   jax.ShapeDtypeStruct((n, 3 * WIDTH), F32),
                   jax.ShapeDtypeStruct((n, WIDTH), F32),
                   jax.ShapeDtypeStruct((n, 2 * D_MODEL), BF16),
                   jax.ShapeDtypeStruct((n, 128), F32)],
        compiler_params=pltpu.CompilerParams(dimension_semantics=("parallel",),
                                             vmem_limit_bytes=VMEM_LIMIT),
        name="inproj",
    )(hp, g1, w_all, ph, gq, gk, acoef, dtb)


def _sb_kernel(q_ref, k_ref, v_ref, tri_ref, o_ref, *, nblk):
    qi = pl.program_id(2)
    q = q_ref[0]
    lane = _iota((BLK, BLK), 1)
    row = _iota((BLK, BLK), 0)
    head_lo = lane < HEAD_DIM
    qf = q.astype(F32)
    q2 = jnp.concatenate([jnp.where(head_lo, qf, 0.0), jnp.where(head_lo, 0.0, qf)], axis=0).astype(BF16)
    tri = tri_ref[...]
    two = 2 * BLK

    cols = lambda t, j: t[:, j * BLK:(j + 1) * BLK]

    def scores(k, nb, vis):
        z = _dot_nt(q2, k)
        neg_abs = lax.bitcast_convert_type(lax.bitcast_convert_type(z, jnp.uint32) | jnp.uint32(0x80000000), F32)
        soft = jnp.log(1.0 + jnp.exp(neg_abs))
        log_beta = jnp.minimum(z, 0.0) - soft
        log_om = log_beta - z
        if vis is not None:
            log_om = jnp.where(vis, log_om, 0.0)
        hi, lo = _split(log_om)
        stacked = jnp.concatenate(
            [jnp.concatenate([cols(hi, j), cols(lo, j)], axis=1) for j in range(nb)], axis=0)
        return log_beta, _dot(stacked, tri)

    def finish(log_beta, rt, nb, carry, acc, v, vis):
        later = carry
        parts = [None] * nb
        for j in reversed(range(nb)):
            rows = rt[j * two:(j + 1) * two]
            parts[j] = cols(log_beta, j) + rows[:, :BLK] + later
            later = later + rows[:, BLK:]
        a = jnp.exp(jnp.concatenate(parts, axis=1))
        if vis is not None:
            a = jnp.where(vis, a, 0.0)
        return later, acc + _dot(a.astype(BF16), v)

    def keys(ref, kphys, nb):
        return ref[0, pl.ds(pl.multiple_of(kphys * BLK, BLK), nb * BLK), :]

    def tile(kphys, nb, state):
        log_beta, rt = scores(keys(k_ref, kphys, nb), nb, None)
        return finish(log_beta, rt, nb, state[0], state[1], keys(v_ref, kphys, nb), None)

    stack2 = lambda m: jnp.concatenate([m, m], axis=0)
    vis_d, vis_0 = stack2(lane < row), stack2(lane >= PAD)
    lb2, rt2 = scores(jnp.concatenate([keys(k_ref, qi, 1), keys(k_ref, nblk - 1, 1)], axis=0), 2,
                      jnp.concatenate([vis_d, vis_0], axis=1))
    zf = jnp.zeros((two, BLK), F32)
    state = finish(cols(lb2, 0), rt2[:two], 1, zf, zf, keys(v_ref, qi, 1), vis_d)

    rem = qi % 4
    four = (qi // 4) % 2
    state = lax.fori_loop(0, rem, lambda it, s: tile(qi - 1 - it, 1, s), state)
    state = lax.fori_loop(0, four, lambda it, s: tile(qi - rem - 4, 4, s), state)
    state = lax.fori_loop(0, qi // 8, lambda it, s: tile(qi - rem - 4 * four - 8 * (it + 1), 8, s), state)
    carry, acc = finish(cols(lb2, 1), rt2[two:], 1, state[0], state[1], keys(v_ref, nblk - 1, 1), vis_0)
    o_ref[0] = jnp.where(head_lo, acc[:BLK], acc[BLK:]).astype(o_ref.dtype)


def _sb_attn(sbqkv, tri, batch, lp):
    nblk = lp // BLK
    nq = nblk - 1
    pairs = WIDTH // BLK
    x3 = sbqkv.reshape(batch, lp, 3 * WIDTH)
    return pl.pallas_call(
        functools.partial(_sb_kernel, nblk=nblk),
        grid=(batch, pairs, nq),
        in_specs=[pl.BlockSpec((1, BLK, BLK), lambda b, p, i: (b, i, p)),
                  pl.BlockSpec((1, lp, BLK), lambda b, p, i: (b, 0, pairs + p)),
                  pl.BlockSpec((1, lp, BLK), lambda b, p, i: (b, 0, 2 * pairs + p)),
                  pl.BlockSpec((2 * BLK, 2 * BLK), lambda b, p, i: (0, 0))],
        out_specs=pl.BlockSpec((1, BLK, BLK), lambda b, p, i: (b, i, p)),
        out_shape=jax.ShapeDtypeStruct((batch, nq * BLK, WIDTH), BF16),
        compiler_params=pltpu.CompilerParams(
            dimension_semantics=("parallel", "parallel", "arbitrary"), vmem_limit_bytes=VMEM_LIMIT),
        name="sb_attn",
    )(x3, x3, x3, tri)


def _dn_kernel(q_ref, k_ref, v_ref, z_ref, bg_ref, cwq_ref, cwk_ref, cwv_ref, gn_ref,
               o_ref, xbuf, s_ref):
    j = pl.program_id(2)

    @pl.when(j == 0)
    def _():
        xbuf[...] = jnp.zeros_like(xbuf)
        s_ref[...] = jnp.zeros_like(s_ref)

    def conv_silu(c, x_ref, cw_ref):
        xbuf[c, 8:8 + BLK, :] = x_ref[0]
        w = cw_ref[...]
        y = (w[3:4] * xbuf[c, 8:8 + BLK, :] + w[2:3] * xbuf[c, 7:7 + BLK, :]
             + w[1:2] * xbuf[c, 6:6 + BLK, :] + w[0:1] * xbuf[c, 5:5 + BLK, :])
        xbuf[c, 0:8, :] = xbuf[c, BLK:BLK + 8, :]
        return y * _sigmoid(y)

    lane = _iota((BLK, BLK), 1)
    row = _iota((BLK, BLK), 0)
    lane_head = lane >> 6
    row_head = row >> 6

    def pattern(cond):
        return jnp.where(cond, 1.0, 0.0).astype(BF16)

    same_head = pattern(lane_head == row_head)
    low_incl = (lane_head == row_head) & (lane <= row)
    low_strict = (lane_head == row_head) & (lane < row)
    cum_incl = pattern(low_incl)

    def l2(t):
        ss = _dot_xr(t * t, same_head)
        return t * lax.rsqrt(ss + EPS)

    q_all = conv_silu(0, q_ref, cwq_ref)
    k_all = conv_silu(1, k_ref, cwk_ref)
    v_all = conv_silu(2, v_ref, cwv_ref)
    shared = dict(j=j, bg=bg_ref[0], lane=lane, row=row, lane_head=lane_head, row_head=row_head,
                  pattern=pattern, same_head=same_head, low_incl=low_incl, low_strict=low_strict,
                  cum_incl=cum_incl, l2=l2)
    for pr in range(DN_GROUP):
        sl = slice(pr * BLK, (pr + 1) * BLK)
        _dn_pair(pr, sl, pl.program_id(1) * DN_GROUP + pr, q_all[:, sl], k_all[:, sl], v_all[:, sl],
                 z_ref, gn_ref, o_ref, s_ref, **shared)


def _dn_pair(pr, sl, hp, q, k, v, z_ref, gn_ref, o_ref, s_ref, *, j, bg, lane, row, lane_head, row_head,
             pattern, same_head, low_incl, low_strict, cum_incl, l2):
    qk_n = l2(jnp.concatenate([q, k], axis=0))
    q = qk_n[:BLK] * (HEAD_DIM ** -0.5)
    k = qk_n[BLK:]
    live = (row >= PAD) | (j > 0)
    picks = jnp.concatenate([pattern(row == 2 * hp + lane_head),
                             pattern(row == HEADS + 2 * hp + lane_head)], axis=1)
    beta_g = _dot_xr(bg, picks)
    beta = jnp.where(live, beta_g[:, :BLK], 0.0)
    g = jnp.where(live, beta_g[:, BLK:], 0.0)
    sums = _dot_xl(jnp.concatenate([cum_incl, same_head], axis=0), g)
    gc = sums[:BLK]
    g_last = sums[BLK:]
    eg = jnp.exp(gc)
    k_beta = k * beta
    v_beta = v * beta
    kbg = k_beta * eg
    q_decay = q * eg
    k_tail = k * jnp.exp(g_last - gc)

    eye = jnp.where(lane == row, 1.0, 0.0).astype(F32)
    gc_heads = _dot_xr(gc, jnp.concatenate([pattern(row == 0), pattern(row == HEAD_DIM)], axis=1))
    k16 = k.astype(BF16)
    rhs_uw = jnp.concatenate([v_beta, kbg], axis=1)
    u_parts, w_parts, a_parts = [], [], []
    for h in range(2):
        mine = lane_head == h
        gch = gc_heads[:, h * BLK:(h + 1) * BLK]
        decay = jnp.exp(jnp.where(low_incl, gch - gch.T, -jnp.inf))
        both = _dot_nt(jnp.concatenate([jnp.where(mine, k_beta, 0.0), jnp.where(mine, q, 0.0)],
                                       axis=0).astype(BF16), k16)
        m = jnp.where(low_strict, both[:BLK] * decay, 0.0)
        a_parts.append(jnp.where(low_incl, both[BLK:] * decay, 0.0))
        a_k = -m
        s_k = eye + a_k
        a_k = _dot3(a_k, a_k)
        for _ in range(4):
            prod = _dot3(a_k, jnp.concatenate([a_k, s_k], axis=1))
            a_k, s_k = prod[:, :BLK], s_k + prod[:, BLK:]
        t_inv = s_k + _dot3(a_k, s_k)
        uw = _dot3(t_inv, rhs_uw)
        u_parts.append(uw[:, :BLK])
        w_parts.append(uw[:, BLK:])
    head0 = lane_head == 0
    u = jnp.where(head0, u_parts[0], u_parts[1])
    w = jnp.where(head0, w_parts[0], w_parts[1])

    block_diag = lane_head == row_head
    state = s_ref[pr]
    vn_parts, inter_parts = [], []
    for c in range(2):
        r = slice(c * CHUNK, (c + 1) * CHUNK)
        ws_qs = _dot(jnp.concatenate([w[r], q_decay[r]], axis=0).astype(BF16), state.astype(BF16))
        vn = u[r] - ws_qs[:CHUNK]
        inter_parts.append(ws_qs[CHUNK:])
        gl = jnp.exp(g_last[c * CHUNK:c * CHUNK + 1, :])
        upd = _dot_tn(k_tail[r].astype(BF16), vn.astype(BF16))
        state = jnp.where(block_diag, state * gl + upd, 0.0)
        vn_parts.append(vn)
    s_ref[pr] = state
    vn = jnp.concatenate(vn_parts, axis=0)
    o = jnp.concatenate(inter_parts, axis=0) + _dot(
        jnp.concatenate(a_parts, axis=1).astype(BF16),
        jnp.concatenate([jnp.where(head0, vn, 0.0), jnp.where(head0, 0.0, vn)], axis=0).astype(BF16))

    ms = _dot_xr(o * o, same_head) * (1.0 / HEAD_DIM)
    zz = z_ref[0, :, sl]
    o_ref[0, :, sl] = (o * lax.rsqrt(ms + EPS) * gn_ref[...] * (zz * _sigmoid(zz))).astype(o_ref.dtype)


def _dn_kernel2(q_ref, k_ref, v_ref, z_ref, bg_ref, cwq_ref, cwk_ref, cwv_ref, gn_ref,
                o_ref, xbuf, s_ref):
    j = pl.program_id(2)
    npair = DN_GROUP
    pairs = range(npair)

    @pl.when(j == 0)
    def _():
        xbuf[...] = jnp.zeros_like(xbuf)
        s_ref[...] = jnp.zeros_like(s_ref)

    def conv_silu(c, x_ref, cw_ref):
        xbuf[c, 8:8 + BLK, :] = x_ref[0]
        w = cw_ref[...]
        y = (w[3:4] * xbuf[c, 8:8 + BLK, :] + w[2:3] * xbuf[c, 7:7 + BLK, :]
             + w[1:2] * xbuf[c, 6:6 + BLK, :] + w[0:1] * xbuf[c, 5:5 + BLK, :])
        xbuf[c, 0:8, :] = xbuf[c, BLK:BLK + 8, :]
        return y * _sigmoid(y)

    lane = _iota((BLK, BLK), 1)
    row = _iota((BLK, BLK), 0)
    lane_head = lane >> 6
    row_head = row >> 6
    pattern = lambda cond: jnp.where(cond, 1.0, 0.0).astype(BF16)
    same_head = pattern(lane_head == row_head)
    block_diag = lane_head == row_head
    low_incl = block_diag & (lane <= row)
    low_strict = block_diag & (lane < row)
    head0 = lane_head == 0
    eye = jnp.where(lane == row, 1.0, 0.0).astype(F32)
    cols = lambda t, p: t[:, p * BLK:(p + 1) * BLK]
    rows = lambda t, p: t[p * BLK:(p + 1) * BLK]
    cat0 = lambda xs: jnp.concatenate(xs, axis=0)
    cat1 = lambda xs: jnp.concatenate(xs, axis=1)

    q_all = conv_silu(0, q_ref, cwq_ref)
    k_all = conv_silu(1, k_ref, cwk_ref)
    v_all = conv_silu(2, v_ref, cwv_ref)

    qk = cat0([cols(q_all, p) for p in pairs] + [cols(k_all, p) for p in pairs])
    qk = qk * lax.rsqrt(_dot_xr(qk * qk, same_head) + EPS)
    q = [rows(qk, p) * (HEAD_DIM ** -0.5) for p in pairs]
    k = [rows(qk, npair + p) for p in pairs]
    v = [cols(v_all, p) for p in pairs]

    live = (row >= PAD) | (j > 0)
    hp = [pl.program_id(1) * npair + p for p in pairs]
    picks = cat1([pattern(row == 2 * hp[p] + lane_head) for p in pairs]
                 + [pattern(row == HEADS + 2 * hp[p] + lane_head) for p in pairs])
    beta_g = _dot_xr(bg_ref[0], picks)
    beta = [jnp.where(live, cols(beta_g, p), 0.0) for p in pairs]
    g = [jnp.where(live, cols(beta_g, npair + p), 0.0) for p in pairs]
    sums = _dot_xl(cat0([pattern(low_incl), same_head]), cat1(g))
    gc = [cols(sums[:BLK], p) for p in pairs]
    g_last = [cols(sums[BLK:], p) for p in pairs]
    gc_heads = _dot_xr(cat0(gc), cat1([pattern(row == 0), pattern(row == HEAD_DIM)]))

    k_beta = [k[p] * beta[p] for p in pairs]
    eg = [jnp.exp(gc[p]) for p in pairs]
    rhs_uw = [cat1([v[p] * beta[p], k_beta[p] * eg[p]]) for p in pairs]
    q_decay = [q[p] * eg[p] for p in pairs]
    k_tail = [k[p] * jnp.exp(g_last[p] - gc[p]) for p in pairs]

    chains = [(p, h) for p in pairs for h in range(2)]
    both = [_dot_nt(cat0([jnp.where(lane_head == h, t, 0.0) for h in range(2) for t in (k_beta[p], q[p])]
                         ).astype(BF16), k[p].astype(BF16)) for p in pairs]
    a_k, a_intra = {}, {}
    for p, h in chains:
        gch = rows(gc_heads, p)[:, h * BLK:(h + 1) * BLK]
        decay = jnp.exp(jnp.where(low_incl, gch - gch.T, -jnp.inf))
        a_k[p, h] = -jnp.where(low_strict, rows(both[p], 2 * h) * decay, 0.0)
        a_intra[p, h] = jnp.where(low_incl, rows(both[p], 2 * h + 1) * decay, 0.0)

    s_k = {c: eye + a_k[c] for c in chains}
    a_k = {c: _dot3(a_k[c], a_k[c]) for c in chains}
    for _ in range(4):
        prod = {c: _dot3(a_k[c], cat1([a_k[c], s_k[c]])) for c in chains}
        a_k = {c: prod[c][:, :BLK] for c in chains}
        s_k = {c: s_k[c] + prod[c][:, BLK:] for c in chains}
    last = {c: _dot3(a_k[c], s_k[c]) for c in chains}
    uw = {c: _dot3(s_k[c] + last[c], rhs_uw[c[0]]) for c in chains}
    u = [jnp.where(head0, uw[p, 0][:, :BLK], uw[p, 1][:, :BLK]) for p in pairs]
    w = [jnp.where(head0, uw[p, 0][:, BLK:], uw[p, 1][:, BLK:]) for p in pairs]

    state = [s_ref[p] for p in pairs]
    vn = [[None, None] for _ in pairs]
    inter = [[None, None] for _ in pairs]
    for c in range(2):
        r = slice(c * CHUNK, (c + 1) * CHUNK)
        ws_qs = [_dot(cat0([w[p][r], q_decay[p][r]]).astype(BF16), state[p].astype(BF16)) for p in pairs]
        for p in pairs:
            vn[p][c] = u[p][r] - ws_qs[p][:CHUNK]
            inter[p][c] = ws_qs[p][CHUNK:]
        upd = [_dot_tn(k_tail[p][r].astype(BF16), vn[p][c].astype(BF16)) for p in pairs]
        for p in pairs:
            gl = jnp.exp(g_last[p][c * CHUNK:c * CHUNK + 1, :])
            state[p] = jnp.where(block_diag, state[p] * gl + upd[p], 0.0)
    o = []
    for p in pairs:
        s_ref[p] = state[p]
        vnp = cat0(vn[p])
        o.append(cat0(inter[p]) + _dot(
            cat1([a_intra[p, 0], a_intra[p, 1]]).astype(BF16),
            cat0([jnp.where(head0, vnp, 0.0), jnp.where(head0, 0.0, vnp)]).astype(BF16)))

    o = cat0(o)
    o = o * lax.rsqrt(_dot_xr(o * o, same_head) * (1.0 / HEAD_DIM) + EPS)
    for p in pairs:
        sl = slice(p * BLK, (p + 1) * BLK)
        zz = z_ref[0, :, sl]
        o_ref[0, :, sl] = (rows(o, p) * gn_ref[...] * (zz * _sigmoid(zz))).astype(o_ref.dtype)


def _deltanet(dnqkv, z, bg, conv_w, gn, batch, lp):
    nblk = lp // BLK
    grp = DN_GROUP * BLK
    pairs = WIDTH // grp
    phys = lambda j: (j + nblk - 1) % nblk
    x3 = dnqkv.reshape(batch, lp, 3 * WIDTH)
    z3 = z.reshape(batch, lp, WIDTH)
    bg3 = bg.reshape(batch, lp, 128)
    col = lambda off: pl.BlockSpec((1, BLK, grp), lambda b, p, j: (b, phys(j), off * pairs + p))
    cw = lambda off: pl.BlockSpec((4, grp), lambda b, p, j: (0, off * pairs + p))
    return pl.pallas_call(
        _dn_kernel2,
        grid=(batch, pairs, nblk),
        in_specs=[col(0), col(1), col(2),
                  pl.BlockSpec((1, BLK, grp), lambda b, p, j: (b, phys(j), p)),
                  pl.BlockSpec((1, BLK, 128), lambda b, p, j: (b, phys(j), 0)),
                  cw(0), cw(1), cw(2),
                  pl.BlockSpec((1, BLK), lambda b, p, j: (0, 0))],
        out_specs=pl.BlockSpec((1, BLK, grp), lambda b, p, j: (b, jnp.maximum(j - 1, 0), p)),
        out_shape=jax.ShapeDtypeStruct((batch, (nblk - 1) * BLK, WIDTH), BF16),
        scratch_shapes=[pltpu.VMEM((3, BLK + 8, grp), F32), pltpu.VMEM((DN_GROUP, BLK, BLK), F32)],
        compiler_params=pltpu.CompilerParams(
            dimension_semantics=("parallel", "parallel", "arbitrary"), vmem_limit_bytes=VMEM_LIMIT),
        name="deltanet",
    )(x3, x3, x3, z3, bg3, conv_w, conv_w, conv_w, gn)


def _merge_kernel(osb_ref, odn_ref, gate_ref, x_ref, wsb_ref, wdn_ref, wo_ref, h_ref):
    gate = gate_ref[0].astype(F32)
    mix = (gate[:, :D_MODEL] * _dot(osb_ref[...], wsb_ref[...])
           + gate[:, D_MODEL:] * _dot(odn_ref[...], wdn_ref[...]))
    h_ref[...] = x_ref[...] + _dot(mix.astype(BF16), wo_ref[...])


def _merge(o_sb, o_dn, gates, x2, wsb, wdn, wo, batch, seq, lp):
    n = batch * seq
    rb = ROWS_OUT
    per = seq // rb
    row = lambda c: pl.BlockSpec((rb, c), lambda i: (i, 0))
    const = lambda r, c: pl.BlockSpec((r, c), lambda i: (0, 0))
    return pl.pallas_call(
        _merge_kernel,
        grid=(n // rb,),
        in_specs=[row(WIDTH), row(WIDTH),
                  pl.BlockSpec((1, rb, 2 * D_MODEL), lambda i: (i // per, i % per, 0)),
                  row(D_MODEL), const(WIDTH, D_MODEL), const(WIDTH, D_MODEL), const(D_MODEL, D_MODEL)],
        out_specs=row(D_MODEL),
        out_shape=jax.ShapeDtypeStruct((n, D_MODEL), F32),
        compiler_params=pltpu.CompilerParams(dimension_semantics=("parallel",),
                                             vmem_limit_bytes=VMEM_LIMIT),
        name="merge",
    )(o_sb.reshape(n, WIDTH), o_dn.reshape(n, WIDTH), gates.reshape(batch, lp, 2 * D_MODEL),
      x2, wsb, wdn, wo)


def _top_rows(s, k, payload=None):
    nrow = s.shape[0]
    rid = _iota(s.shape, 0)
    vals, picks = [], []
    for _ in range(k):
        m = jnp.max(s, axis=0, keepdims=True)
        am = jnp.min(jnp.where(s == m, rid, nrow), axis=0, keepdims=True)
        hit = rid == am
        vals.append(m)
        picks.append(am if payload is None
                     else jnp.max(jnp.where(hit, payload, -1), axis=0, keepdims=True))
        s = jnp.where(hit, -jnp.inf, s)
    return jnp.concatenate(vals, axis=0), jnp.concatenate(picks, axis=0)


def _route_kernel(h_ref, g2_ref, wq_ref, keys_ref, xn_ref, idx_ref, gate_ref):
    x = h_ref[...]
    ms = jnp.mean(x * x, axis=-1, keepdims=True)
    xn = x * lax.rsqrt(ms + EPS) * g2_ref[...]
    xn_ref[...] = xn
    q = _dot(xn.astype(BF16), wq_ref[...])
    keys = (keys_ref[0].astype(BF16), keys_ref[1].astype(BF16))
    idx_rows, gate_rows = [], []
    for h in range(PEER_HEADS):
        tops = []
        for p in range(2):
            c0 = (2 * h + p) * PEER_KEYS
            s = _dot_nt(keys[p], q[:, c0:c0 + PEER_KEYS].astype(BF16))
            tops.append(_top_rows(s, PEER_TOPK))
        (s1, i1), (s2, i2) = tops
        brow = _iota((8, s1.shape[1]), 0)
        cs, ci = [s1[0:1] + s2], [i1[0:1] * PEER_KEYS + i2]
        for a in range(1, 8):
            keep = PEER_TOPK // (a + 1)
            sa = s1[a:a + 1] + s2[0:8]
            cs.append(sa if keep >= 8 else jnp.where(brow < keep, sa, -jnp.inf))
            ci.append(i1[a:a + 1] * PEER_KEYS + i2[0:8])
        cs.append(s1[8:16] + s2[0:1])
        ci.append(i1[8:16] * PEER_KEYS + i2[0:1])
        top_s, top_i = _top_rows(jnp.concatenate(cs, axis=0), PEER_TOPK, jnp.concatenate(ci, axis=0))
        e = jnp.exp(top_s - top_s[0:1])
        idx_rows.append(top_i)
        gate_rows.append(e / jnp.sum(e, axis=0, keepdims=True))
    idx_ref[...] = jnp.concatenate(idx_rows, axis=0)
    gate_ref[...] = jnp.concatenate(gate_rows, axis=0)


def _route(h1, g2, wq, keys):
    n = h1.shape[0]
    rb = ROWS_OUT
    return pl.pallas_call(
        _route_kernel,
        grid=(n // rb,),
        in_specs=[pl.BlockSpec((rb, D_MODEL), lambda i: (i, 0)),
                  pl.BlockSpec((1, D_MODEL), lambda i: (0, 0)),
                  pl.BlockSpec(wq.shape, lambda i: (0, 0)),
                  pl.BlockSpec(keys.shape, lambda i: (0, 0, 0))],
        out_specs=[pl.BlockSpec((rb, D_MODEL), lambda i: (i, 0)),
                   pl.BlockSpec((PEER_HK, rb), lambda i: (0, i)),
                   pl.BlockSpec((PEER_HK, rb), lambda i: (0, i))],
        out_shape=[jax.ShapeDtypeStruct((n, D_MODEL), F32),
                   jax.ShapeDtypeStruct((PEER_HK, n), jnp.int32),
                   jax.ShapeDtypeStruct((PEER_HK, n), F32)],
        compiler_params=pltpu.CompilerParams(dimension_semantics=("parallel",),
                                             vmem_limit_bytes=VMEM_LIMIT),
        name="route",
    )(h1, g2, wq, keys)


def _peer_kernel(idx_ref, idx_next_ref, gate_ref, xn_ref, h_ref, sel_ref, rep_ref, tab_hbm, out_ref,
                 gbuf, act_ref, sem):
    i = pl.program_id(0)
    n = pl.num_programs(0)
    slot = i % 2
    rows = PEER_TB * PEER_HK
    sub = D_MODEL // 128

    def issue_token(src_idx, t, sl, e0=0, e1=PEER_HK):
        base = t * PEER_HK
        for e in range(e0, e1):
            pltpu.make_async_copy(tab_hbm.at[src_idx[t, e]], gbuf.at[sl, base + e], sem.at[sl]).start(
                priority=e % DMA_THREADS)

    @pl.when(i == 0)
    def _():
        def body(t, carry):
            issue_token(idx_ref, t, 0)
            return carry
        lax.fori_loop(0, PEER_TB, body, 0)

    pltpu.make_async_copy(tab_hbm.at[pl.ds(0, rows)], gbuf.at[slot], sem.at[slot]).wait()

    unroll = PEER_UNROLL
    ones = jnp.ones((sub, 2 * 128), BF16)
    lanes = lambda t, k: t[:, k * 128:(k + 1) * 128]

    def tile_rows(t):
        return gbuf[slot, pl.ds(pl.multiple_of(t * PEER_HK, PEER_HK), PEER_HK)]

    def act_body(g, carry):
        prods = []
        for k in range(unroll):
            t = g * unroll + k
            issue_token(idx_next_ref, t, 1 - slot, 0, PEER_HK // 2)
            u = lax.bitcast_convert_type(tile_rows(t) << 16, F32)
            prods.append((u * xn_ref[t][None]).reshape(PEER_HK * sub, 128).astype(BF16))
        part = _dot(sel_ref[...], jnp.concatenate(prods, axis=1))
        hi, lo = _split(jnp.concatenate([lanes(part, k) for k in range(unroll)], axis=0))
        sums = _dot_nt(ones, jnp.concatenate([hi, lo], axis=1))
        for k in range(unroll):
            act_ref[pl.ds(g * unroll + k, 1), :] = lanes(sums, k)[0:1]
        return carry

    lax.fori_loop(0, PEER_TB // unroll, act_body, 0)

    act = act_ref[...]
    wgt = (0.5 * act * (1.0 + lax.erf(act * (2.0 ** -0.5))) * gate_ref[...]).astype(BF16)
    trow = _iota((PEER_TB, unroll * 128), 0)
    lane_tok = _iota((PEER_TB, unroll * 128), 1) >> 7

    def mix_body(g, carry):
        pick = jnp.where(trow == g * unroll + lane_tok, 1.0, 0.0).astype(BF16)
        col = _dot_tn(wgt, pick)
        wide = _dot(rep_ref[...], col.astype(BF16))
        for k in range(unroll):
            t = g * unroll + k
            issue_token(idx_next_ref, t, 1 - slot, PEER_HK // 2, PEER_HK)
            v = lax.bitcast_convert_type(tile_rows(t) & jnp.uint32(0xFFFF0000), F32)
            out_ref[t] = h_ref[t] + jnp.sum(lanes(wide, k).reshape(PEER_HK, sub, 128) * v, axis=0)
        return carry

    lax.fori_loop(0, PEER_TB // unroll, mix_body, 0)

    @pl.when(i == n - 1)
    def _():
        pltpu.make_async_copy(tab_hbm.at[pl.ds(0, rows)], gbuf.at[1 - slot], sem.at[1 - slot]).wait()


def _pack_tables(u_tab, v_tab):
    half = lambda t: lax.bitcast_convert_type(t.astype(BF16), jnp.uint16).astype(jnp.uint32)
    packed = half(u_tab) | (half(v_tab) << 16)
    return packed.reshape(u_tab.shape[0], D_MODEL // 128, 128)


def _peer(idx, gate, xn, h1, u_tab, v_tab):
    n = h1.shape[0]
    tb = PEER_TB
    steps = n // tb
    rows = tb * PEER_HK
    sub = D_MODEL // 128
    tiles = lambda t: t.reshape(n, sub, 128)
    e = np.arange(PEER_HK)
    sel = jnp.asarray(e[:, None] == (np.arange(PEER_HK * sub)[None, :] // sub), BF16)
    smem = lambda f: pl.BlockSpec((tb, PEER_HK), f, memory_space=pltpu.SMEM)
    tok = pl.BlockSpec((tb, sub, 128), lambda i: (i, 0, 0))
    out = pl.pallas_call(
        _peer_kernel,
        grid=(steps,),
        in_specs=[smem(lambda i: (i, 0)),
                  smem(lambda i: (jnp.minimum(i + 1, steps - 1), 0)),
                  pl.BlockSpec((tb, PEER_HK), lambda i: (i, 0)),
                  tok, tok,
                  pl.BlockSpec((PEER_HK, PEER_HK * sub), lambda i: (0, 0)),
                  pl.BlockSpec((PEER_HK * sub, PEER_HK), lambda i: (0, 0)),
                  pl.BlockSpec(memory_space=pl.ANY)],
        out_specs=tok,
        out_shape=jax.ShapeDtypeStruct((n, sub, 128), F32),
        scratch_shapes=[pltpu.VMEM((2, rows, sub, 128), jnp.uint32), pltpu.VMEM((tb, PEER_HK), F32),
                        pltpu.SemaphoreType.DMA((2,))],
        compiler_params=pltpu.CompilerParams(dimension_semantics=("arbitrary",),
                                             vmem_limit_bytes=VMEM_LIMIT),
        name="peer",
    )(idx, idx, gate, tiles(xn), tiles(h1), sel, sel.T, _pack_tables(u_tab, v_tab))
    return out.reshape(n, D_MODEL)


def _constants():
    r = np.arange(WIDTH)
    ph = (r[:, None] // HEAD_DIM == r[None, :] // HEAD_DIM).astype(np.float32) / HEAD_DIM
    s = np.arange(BLK)
    later = (s[:, None] > s[None, :]).astype(np.float32)
    half = np.concatenate([later, np.ones((BLK, BLK), np.float32)], axis=1)
    tri = np.concatenate([half, half], axis=0)
    return jnp.asarray(ph, BF16), jnp.asarray(tri, BF16)


def _layer(x, meta_tokens, norm1_g, w_in, sb_q_norm_g, sb_k_norm_g, dn_conv_w, dn_a_log, dn_dt_bias,
           dn_out_norm_g, w_sb_out, w_dn_out, w_o, norm2_g, peer_w_q, peer_sub_keys, peer_u, peer_v):
    batch, seq, d = x.shape
    lp = seq + BLK
    ph, tri = _constants()

    tail = jnp.concatenate([jnp.zeros((PAD, d), x.dtype), meta_tokens.astype(x.dtype)], axis=0)
    hp = jnp.concatenate([x, jnp.broadcast_to(tail[None], (batch, BLK, d))], axis=1).reshape(batch * lp, d)

    c_ba = 3 * WIDTH + 3 * WIDTH + WIDTH
    w_all = jnp.concatenate([w_in[:, :c_ba], w_in[:, c_ba + 2 * HEADS:], w_in[:, c_ba:c_ba + 2 * HEADS],
                             jnp.zeros((d, 128 - 2 * HEADS), w_in.dtype)], axis=1).astype(BF16)
    gq = (jnp.tile(sb_q_norm_g.astype(F32), HEADS) * (HEAD_DIM ** -0.5))[None]
    gk = jnp.tile(sb_k_norm_g.astype(F32), HEADS)[None]
    lane_pad = lambda t: jnp.pad(t.astype(F32), (HEADS, 128 - 2 * HEADS))[None]
    acoef = lane_pad(-jnp.exp(dn_a_log.astype(F32)))
    dtb = lane_pad(dn_dt_bias)

    sbqkv, dnqkv, z, gates, bg = _inproj(hp, norm1_g.astype(F32)[None], w_all, ph, gq, gk, acoef, dtb)
    o_sb = _sb_attn(sbqkv, tri, batch, lp)
    gn = jnp.tile(dn_out_norm_g.astype(F32), 2)[None]
    o_dn = _deltanet(dnqkv, z, bg, dn_conv_w.astype(F32), gn, batch, lp)
    h1 = _merge(o_sb, o_dn, gates, x.reshape(batch * seq, d), w_sb_out.astype(BF16),
                w_dn_out.astype(BF16), w_o.astype(BF16), batch, seq, lp)
    xn2, idx_t, gate_t = _route(h1, norm2_g.astype(F32)[None], peer_w_q.astype(BF16), peer_sub_keys)
    h2 = _peer(idx_t.T, gate_t.T, xn2, h1, peer_u, peer_v)
    return h2.reshape(batch, seq, d)


def kernel(x, meta_tokens, norm1_g, w_in, sb_q_norm_g, sb_k_norm_g, dn_conv_w, dn_a_log, dn_dt_bias,
           dn_out_norm_g, w_sb_out, w_dn_out, w_o, norm2_g, peer_w_q, peer_sub_keys, peer_u, peer_v):
    assert norm1_g.shape[0] == 1, "one layer"
    return _layer(x, meta_tokens, norm1_g[0], w_in[0], sb_q_norm_g[0], sb_k_norm_g[0], dn_conv_w[0],
                  dn_a_log[0], dn_dt_bias[0], dn_out_norm_g[0], w_sb_out[0], w_dn_out[0], w_o[0],
                  norm2_g[0], peer_w_q[0], peer_sub_keys[0], peer_u[0], peer_v[0])
```

```python
import functools

import numpy as np
import jax
import jax.numpy as jnp
from jax import lax
from jax.experimental import pallas as pl
from jax.experimental.pallas import tpu as pltpu

F32 = jnp.float32
BF16 = jnp.bfloat16

D_MODEL = 1024
N_META = 16
BLK = 128
PAD = BLK - N_META
HEADS = 8
HEAD_DIM = 64
WIDTH = HEADS * HEAD_DIM
CHUNK = 64
PEER_HEADS = 8
PEER_KEYS = 128
PEER_TOPK = 16
PEER_HK = PEER_HEADS * PEER_TOPK
EPS = 1e-6

C_SB = 0
C_DN = 3 * WIDTH
C_Z = C_DN + 3 * WIDTH
C_GATE = C_Z + WIDTH
C_BA = C_GATE + 2 * D_MODEL
C_END = C_BA + 128

ROWS_IN = 256
ROWS_OUT = 256
SB_SPAN = 4
DN_GROUP = 4
DMA_THREADS = 2
PEER_TB = 16
PEER_UNROLL = 4
VMEM_LIMIT = 56 * 1024 * 1024


def _dot(a, b):
    return jnp.dot(a, b, preferred_element_type=F32)


def _dot_nt(a, b):
    return lax.dot_general(a, b, (((1,), (1,)), ((), ())), preferred_element_type=F32)


def _dot_tn(a, b):
    return lax.dot_general(a, b, (((0,), (0,)), ((), ())), preferred_element_type=F32)


def _split(a):
    hi = a.astype(BF16)
    lo = (a - hi.astype(F32)).astype(BF16)
    return hi, lo


def _dot_xr(a, b_exact):
    hi, lo = _split(a)
    return _dot(jnp.concatenate([hi, lo], axis=1), jnp.concatenate([b_exact, b_exact], axis=0))


def _dot_xl(a_exact, b):
    hi, lo = _split(b)
    return _dot(jnp.concatenate([a_exact, a_exact], axis=1), jnp.concatenate([hi, lo], axis=0))


def _dot3(a, b):
    ah, al = _split(a)
    bh, bl = _split(b)
    return _dot(jnp.concatenate([ah, ah, al], axis=1), jnp.concatenate([bh, bl, bh], axis=0))


def _sigmoid(x):
    return 1.0 / (1.0 + jnp.exp(-x))


def _softplus(x):
    return jnp.maximum(x, 0.0) + jnp.log1p(jnp.exp(-jnp.abs(x)))


def _iota(shape, dim):
    return lax.broadcasted_iota(jnp.int32, shape, dim)


def _inproj_kernel(x_ref, g1_ref, w_ref, ph_ref, gq_ref, gk_ref, acoef_ref, dtb_ref,
                   sb_ref, dn_ref, z_ref, gate_ref, bg_ref):
    x = x_ref[...]
    ms = jnp.mean(x * x, axis=-1, keepdims=True)
    xn = (x * lax.rsqrt(ms + EPS) * g1_ref[...]).astype(BF16)

    def proj(c0, c1):
        return _dot(xn, w_ref[:, c0:c1])

    def head_norm(t, g):
        msh = _dot((t * t).astype(BF16), ph_ref[...])
        return (t * lax.rsqrt(msh + EPS) * g).astype(BF16)

    sb_ref[:, 0:WIDTH] = head_norm(proj(C_SB, C_SB + WIDTH), gq_ref[...])
    sb_ref[:, WIDTH:2 * WIDTH] = head_norm(proj(C_SB + WIDTH, C_SB + 2 * WIDTH), gk_ref[...])
    sb_ref[:, 2 * WIDTH:3 * WIDTH] = proj(C_SB + 2 * WIDTH, C_SB + 3 * WIDTH).astype(BF16)
    dn_ref[...] = proj(C_DN, C_DN + 3 * WIDTH)
    z_ref[...] = proj(C_Z, C_Z + WIDTH)
    gate_ref[...] = _sigmoid(proj(C_GATE, C_GATE + 2 * D_MODEL)).astype(BF16)
    ba = proj(C_BA, C_END)
    lane = _iota(ba.shape, 1)
    bg_ref[...] = jnp.where(lane < HEADS, _sigmoid(ba), acoef_ref[...] * _softplus(ba + dtb_ref[...]))


def _inproj(hp, g1, w_all, ph, gq, gk, acoef, dtb):
    n = hp.shape[0]
    rb = ROWS_IN
    const = lambda shape: pl.BlockSpec(shape, lambda i: (0, 0))
    row = lambda c: pl.BlockSpec((rb, c), lambda i: (i, 0))
    return pl.pallas_call(
        _inproj_kernel,
        grid=(n // rb,),
        in_specs=[row(D_MODEL), const((1, D_MODEL)), const((D_MODEL, C_END)), const((WIDTH, WIDTH)),
                  const((1, WIDTH)), const((1, WIDTH)), const((1, 128)), const((1, 128))],
        out_specs=[row(3 * WIDTH), row(3 * WIDTH), row(WIDTH), row(2 * D_MODEL), row(128)],
        out_shape=[jax.ShapeDtypeStruct((n, 3 * WIDTH), BF16),
                   jax.ShapeDtypeStruct((n, 3 * WIDTH), F32),
                   jax.ShapeDtypeStruct((n, WIDTH), F32),
                   jax.ShapeDtypeStruct((n, 2 * D_MODEL), BF16),
                   jax.ShapeDtypeStruct((n, 128), F32)],
        compiler_params=pltpu.CompilerParams(dimension_semantics=("parallel",),
                                             vmem_limit_bytes=VMEM_LIMIT),
        name="inproj",
    )(hp, g1, w_all, ph, gq, gk, acoef, dtb)


def _sb_kernel(q_ref, k_ref, v_ref, tri_ref, o_ref, *, nblk):
    qi = pl.program_id(2)
    q = q_ref[0]
    lane = _iota((BLK, BLK), 1)
    row = _iota((BLK, BLK), 0)
    head_lo = lane < HEAD_DIM
    qf = q.astype(F32)
    q2 = jnp.concatenate([jnp.where(head_lo, qf, 0.0), jnp.where(head_lo, 0.0, qf)], axis=0).astype(BF16)
    tri = tri_ref[...]
    two = 2 * BLK

    cols = lambda t, j: t[:, j * BLK:(j + 1) * BLK]

    def scores(k, nb, vis):
        z = _dot_nt(q2, k)
        neg_abs = lax.bitcast_convert_type(lax.bitcast_convert_type(z, jnp.uint32) | jnp.uint32(0x80000000), F32)
        soft = jnp.log(1.0 + jnp.exp(neg_abs))
        log_beta = jnp.minimum(z, 0.0) - soft
        log_om = log_beta - z
        if vis is not None:
            log_om = jnp.where(vis, log_om, 0.0)
        hi, lo = _split(log_om)
        stacked = jnp.concatenate(
            [jnp.concatenate([cols(hi, j), cols(lo, j)], axis=1) for j in range(nb)], axis=0)
        return log_beta, _dot(stacked, tri)

    def finish(log_beta, rt, nb, carry, acc, v, vis):
        later = carry
        parts = [None] * nb
        for j in reversed(range(nb)):
            rows = rt[j * two:(j + 1) * two]
            parts[j] = cols(log_beta, j) + rows[:, :BLK] + later
            later = later + rows[:, BLK:]
        a = jnp.exp(jnp.concatenate(parts, axis=1))
        if vis is not None:
            a = jnp.where(vis, a, 0.0)
        return later, acc + _dot(a.astype(BF16), v)

    def keys(ref, kphys, nb):
        return ref[0, pl.ds(pl.multiple_of(kphys * BLK, BLK), nb * BLK), :]

    def tile(kphys, nb, state):
        log_beta, rt = scores(keys(k_ref, kphys, nb), nb, None)
        return finish(log_beta, rt, nb, state[0], state[1], keys(v_ref, kphys, nb), None)

    stack2 = lambda m: jnp.concatenate([m, m], axis=0)
    vis_d, vis_0 = stack2(lane < row), stack2(lane >= PAD)
    lb2, rt2 = scores(jnp.concatenate([keys(k_ref, qi, 1), keys(k_ref, nblk - 1, 1)], axis=0), 2,
                      jnp.concatenate([vis_d, vis_0], axis=1))
    zf = jnp.zeros((two, BLK), F32)
    state = finish(cols(lb2, 0), rt2[:two], 1, zf, zf, keys(v_ref, qi, 1), vis_d)

    rem = qi % 4
    four = (qi // 4) % 2
    state = lax.fori_loop(0, rem, lambda it, s: tile(qi - 1 - it, 1, s), state)
    state = lax.fori_loop(0, four, lambda it, s: tile(qi - rem - 4, 4, s), state)
    state = lax.fori_loop(0, qi // 8, lambda it, s: tile(qi - rem - 4 * four - 8 * (it + 1), 8, s), state)
    carry, acc = finish(cols(lb2, 1), rt2[two:], 1, state[0], state[1], keys(v_ref, nblk - 1, 1), vis_0)
    o_ref[0] = jnp.where(head_lo, acc[:BLK], acc[BLK:]).astype(o_ref.dtype)


def _sb_attn(sbqkv, tri, batch, lp):
    nblk = lp // BLK
    nq = nblk - 1
    pairs = WIDTH // BLK
    x3 = sbqkv.reshape(batch, lp, 3 * WIDTH)
    return pl.pallas_call(
        functools.partial(_sb_kernel, nblk=nblk),
        grid=(batch, pairs, nq),
        in_specs=[pl.BlockSpec((1, BLK, BLK), lambda b, p, i: (b, i, p)),
                  pl.BlockSpec((1, lp, BLK), lambda b, p, i: (b, 0, pairs + p)),
                  pl.BlockSpec((1, lp, BLK), lambda b, p, i: (b, 0, 2 * pairs + p)),
                  pl.BlockSpec((2 * BLK, 2 * BLK), lambda b, p, i: (0, 0))],
        out_specs=pl.BlockSpec((1, BLK, BLK), lambda b, p, i: (b, i, p)),
        out_shape=jax.ShapeDtypeStruct((batch, nq * BLK, WIDTH), BF16),
        compiler_params=pltpu.CompilerParams(
            dimension_semantics=("parallel", "parallel", "arbitrary"), vmem_limit_bytes=VMEM_LIMIT),
        name="sb_attn",
    )(x3, x3, x3, tri)


def _dn_kernel(q_ref, k_ref, v_ref, z_ref, bg_ref, cwq_ref, cwk_ref, cwv_ref, gn_ref,
               o_ref, xbuf, s_ref):
    j = pl.program_id(2)

    @pl.when(j == 0)
    def _():
        xbuf[...] = jnp.zeros_like(xbuf)
        s_ref[...] = jnp.zeros_like(s_ref)

    def conv_silu(c, x_ref, cw_ref):
        xbuf[c, 8:8 + BLK, :] = x_ref[0]
        w = cw_ref[...]
        y = (w[3:4] * xbuf[c, 8:8 + BLK, :] + w[2:3] * xbuf[c, 7:7 + BLK, :]
             + w[1:2] * xbuf[c, 6:6 + BLK, :] + w[0:1] * xbuf[c, 5:5 + BLK, :])
        xbuf[c, 0:8, :] = xbuf[c, BLK:BLK + 8, :]
        return y * _sigmoid(y)

    lane = _iota((BLK, BLK), 1)
    row = _iota((BLK, BLK), 0)
    lane_head = lane >> 6
    row_head = row >> 6

    def pattern(cond):
        return jnp.where(cond, 1.0, 0.0).astype(BF16)

    same_head = pattern(lane_head == row_head)
    low_incl = (lane_head == row_head) & (lane <= row)
    low_strict = (lane_head == row_head) & (lane < row)
    cum_incl = pattern(low_incl)

    def l2(t):
        ss = _dot_xr(t * t, same_head)
        return t * lax.rsqrt(ss + EPS)

    q_all = conv_silu(0, q_ref, cwq_ref)
    k_all = conv_silu(1, k_ref, cwk_ref)
    v_all = conv_silu(2, v_ref, cwv_ref)
    shared = dict(j=j, bg=bg_ref[0], lane=lane, row=row, lane_head=lane_head, row_head=row_head,
                  pattern=pattern, same_head=same_head, low_incl=low_incl, low_strict=low_strict,
                  cum_incl=cum_incl, l2=l2)
    for pr in range(DN_GROUP):
        sl = slice(pr * BLK, (pr + 1) * BLK)
        _dn_pair(pr, sl, pl.program_id(1) * DN_GROUP + pr, q_all[:, sl], k_all[:, sl], v_all[:, sl],
                 z_ref, gn_ref, o_ref, s_ref, **shared)


def _dn_pair(pr, sl, hp, q, k, v, z_ref, gn_ref, o_ref, s_ref, *, j, bg, lane, row, lane_head, row_head,
             pattern, same_head, low_incl, low_strict, cum_incl, l2):
    qk_n = l2(jnp.concatenate([q, k], axis=0))
    q = qk_n[:BLK] * (HEAD_DIM ** -0.5)
    k = qk_n[BLK:]
    live = (row >= PAD) | (j > 0)
    picks = jnp.concatenate([pattern(row == 2 * hp + lane_head),
                             pattern(row == HEADS + 2 * hp + lane_head)], axis=1)
    beta_g = _dot_xr(bg, picks)
    beta = jnp.where(live, beta_g[:, :BLK], 0.0)
    g = jnp.where(live, beta_g[:, BLK:], 0.0)
    sums = _dot_xl(jnp.concatenate([cum_incl, same_head], axis=0), g)
    gc = sums[:BLK]
    g_last = sums[BLK:]
    eg = jnp.exp(gc)
    k_beta = k * beta
    v_beta = v * beta
    kbg = k_beta * eg
    q_decay = q * eg
    k_tail = k * jnp.exp(g_last - gc)

    eye = jnp.where(lane == row, 1.0, 0.0).astype(F32)
    gc_heads = _dot_xr(gc, jnp.concatenate([pattern(row == 0), pattern(row == HEAD_DIM)], axis=1))
    k16 = k.astype(BF16)
    rhs_uw = jnp.concatenate([v_beta, kbg], axis=1)
    u_parts, w_parts, a_parts = [], [], []
    for h in range(2):
        mine = lane_head == h
        gch = gc_heads[:, h * BLK:(h + 1) * BLK]
        decay = jnp.exp(jnp.where(low_incl, gch - gch.T, -jnp.inf))
        both = _dot_nt(jnp.concatenate([jnp.where(mine, k_beta, 0.0), jnp.where(mine, q, 0.0)],
                                       axis=0).astype(BF16), k16)
        m = jnp.where(low_strict, both[:BLK] * decay, 0.0)
        a_parts.append(jnp.where(low_incl, both[BLK:] * decay, 0.0))
        a_k = -m
        s_k = eye + a_k
        a_k = _dot3(a_k, a_k)
        for _ in range(4):
            prod = _dot3(a_k, jnp.concatenate([a_k, s_k], axis=1))
            a_k, s_k = prod[:, :BLK], s_k + prod[:, BLK:]
        t_inv = s_k + _dot3(a_k, s_k)
        uw = _dot3(t_inv, rhs_uw)
        u_parts.append(uw[:, :BLK])
        w_parts.append(uw[:, BLK:])
    head0 = lane_head == 0
    u = jnp.where(head0, u_parts[0], u_parts[1])
    w = jnp.where(head0, w_parts[0], w_parts[1])

    block_diag = lane_head == row_head
    state = s_ref[pr]
    vn_parts, inter_parts = [], []
    for c in range(2):
        r = slice(c * CHUNK, (c + 1) * CHUNK)
        ws_qs = _dot(jnp.concatenate([w[r], q_decay[r]], axis=0).astype(BF16), state.astype(BF16))
        vn = u[r] - ws_qs[:CHUNK]
        inter_parts.append(ws_qs[CHUNK:])
        gl = jnp.exp(g_last[c * CHUNK:c * CHUNK + 1, :])
        upd = _dot_tn(k_tail[r].astype(BF16), vn.astype(BF16))
        state = jnp.where(block_diag, state * gl + upd, 0.0)
        vn_parts.append(vn)
    s_ref[pr] = state
    vn = jnp.concatenate(vn_parts, axis=0)
    o = jnp.concatenate(inter_parts, axis=0) + _dot(
        jnp.concatenate(a_parts, axis=1).astype(BF16),
        jnp.concatenate([jnp.where(head0, vn, 0.0), jnp.where(head0, 0.0, vn)], axis=0).astype(BF16))

    ms = _dot_xr(o * o, same_head) * (1.0 / HEAD_DIM)
    zz = z_ref[0, :, sl]
    o_ref[0, :, sl] = (o * lax.rsqrt(ms + EPS) * gn_ref[...] * (zz * _sigmoid(zz))).astype(o_ref.dtype)


def _dn_kernel2(q_ref, k_ref, v_ref, z_ref, bg_ref, cwq_ref, cwk_ref, cwv_ref, gn_ref,
                o_ref, xbuf, s_ref):
    j = pl.program_id(2)
    npair = DN_GROUP
    pairs = range(npair)

    @pl.when(j == 0)
    def _():
        xbuf[...] = jnp.zeros_like(xbuf)
        s_ref[...] = jnp.zeros_like(s_ref)

    def conv_silu(c, x_ref, cw_ref):
        xbuf[c, 8:8 + BLK, :] = x_ref[0]
        w = cw_ref[...]
        y = (w[3:4] * xbuf[c, 8:8 + BLK, :] + w[2:3] * xbuf[c, 7:7 + BLK, :]
             + w[1:2] * xbuf[c, 6:6 + BLK, :] + w[0:1] * xbuf[c, 5:5 + BLK, :])
        xbuf[c, 0:8, :] = xbuf[c, BLK:BLK + 8, :]
        return y * _sigmoid(y)

    lane = _iota((BLK, BLK), 1)
    row = _iota((BLK, BLK), 0)
    lane_head = lane >> 6
    row_head = row >> 6
    pattern = lambda cond: jnp.where(cond, 1.0, 0.0).astype(BF16)
    same_head = pattern(lane_head == row_head)
    block_diag = lane_head == row_head
    low_incl = block_diag & (lane <= row)
    low_strict = block_diag & (lane < row)
    head0 = lane_head == 0
    eye = jnp.where(lane == row, 1.0, 0.0).astype(F32)
    cols = lambda t, p: t[:, p * BLK:(p + 1) * BLK]
    rows = lambda t, p: t[p * BLK:(p + 1) * BLK]
    cat0 = lambda xs: jnp.concatenate(xs, axis=0)
    cat1 = lambda xs: jnp.concatenate(xs, axis=1)

    q_all = conv_silu(0, q_ref, cwq_ref)
    k_all = conv_silu(1, k_ref, cwk_ref)
    v_all = conv_silu(2, v_ref, cwv_ref)

    qk = cat0([cols(q_all, p) for p in pairs] + [cols(k_all, p) for p in pairs])
    qk = qk * lax.rsqrt(_dot_xr(qk * qk, same_head) + EPS)
    q = [rows(qk, p) * (HEAD_DIM ** -0.5) for p in pairs]
    k = [rows(qk, npair + p) for p in pairs]
    v = [cols(v_all, p) for p in pairs]

    live = (row >= PAD) | (j > 0)
    hp = [pl.program_id(1) * npair + p for p in pairs]
    picks = cat1([pattern(row == 2 * hp[p] + lane_head) for p in pairs]
                 + [pattern(row == HEADS + 2 * hp[p] + lane_head) for p in pairs])
    beta_g = _dot_xr(bg_ref[0], picks)
    beta = [jnp.where(live, cols(beta_g, p), 0.0) for p in pairs]
    g = [jnp.where(live, cols(beta_g, npair + p), 0.0) for p in pairs]
    sums = _dot_xl(cat0([pattern(low_incl), same_head]), cat1(g))
    gc = [cols(sums[:BLK], p) for p in pairs]
    g_last = [cols(sums[BLK:], p) for p in pairs]
    gc_heads = _dot_xr(cat0(gc), cat1([pattern(row == 0), pattern(row == HEAD_DIM)]))

    k_beta = [k[p] * beta[p] for p in pairs]
    eg = [jnp.exp(gc[p]) for p in pairs]
    rhs_uw = [cat1([v[p] * beta[p], k_beta[p] * eg[p]]) for p in pairs]
    q_decay = [q[p] * eg[p] for p in pairs]
    k_tail = [k[p] * jnp.exp(g_last[p] - gc[p]) for p in pairs]

    chains = [(p, h) for p in pairs for h in range(2)]
    both = [_dot_nt(cat0([jnp.where(lane_head == h, t, 0.0) for h in range(2) for t in (k_beta[p], q[p])]
                         ).astype(BF16), k[p].astype(BF16)) for p in pairs]
    a_k, a_intra = {}, {}
    for p, h in chains:
        gch = rows(gc_heads, p)[:, h * BLK:(h + 1) * BLK]
        decay = jnp.exp(jnp.where(low_incl, gch - gch.T, -jnp.inf))
        a_k[p, h] = -jnp.where(low_strict, rows(both[p], 2 * h) * decay, 0.0)
        a_intra[p, h] = jnp.where(low_incl, rows(both[p], 2 * h + 1) * decay, 0.0)

    s_k = {c: eye + a_k[c] for c in chains}
    a_k = {c: _dot3(a_k[c], a_k[c]) for c in chains}
    for _ in range(4):
        prod = {c: _dot3(a_k[c], cat1([a_k[c], s_k[c]])) for c in chains}
        a_k = {c: prod[c][:, :BLK] for c in chains}
        s_k = {c: s_k[c] + prod[c][:, BLK:] for c in chains}
    last = {c: _dot3(a_k[c], s_k[c]) for c in chains}
    uw = {c: _dot3(s_k[c] + last[c], rhs_uw[c[0]]) for c in chains}
    u = [jnp.where(head0, uw[p, 0][:, :BLK], uw[p, 1][:, :BLK]) for p in pairs]
    w = [jnp.where(head0, uw[p, 0][:, BLK:], uw[p, 1][:, BLK:]) for p in pairs]

    state = [s_ref[p] for p in pairs]
    vn = [[None, None] for _ in pairs]
    inter = [[None, None] for _ in pairs]
    for c in range(2):
        r = slice(c * CHUNK, (c + 1) * CHUNK)
        ws_qs = [_dot(cat0([w[p][r], q_decay[p][r]]).astype(BF16), state[p].astype(BF16)) for p in pairs]
        for p in pairs:
            vn[p][c] = u[p][r] - ws_qs[p][:CHUNK]
            inter[p][c] = ws_qs[p][CHUNK:]
        upd = [_dot_tn(k_tail[p][r].astype(BF16), vn[p][c].astype(BF16)) for p in pairs]
        for p in pairs:
            gl = jnp.exp(g_last[p][c * CHUNK:c * CHUNK + 1, :])
            state[p] = jnp.where(block_diag, state[p] * gl + upd[p], 0.0)
    o = []
    for p in pairs:
        s_ref[p] = state[p]
        vnp = cat0(vn[p])
        o.append(cat0(inter[p]) + _dot(
            cat1([a_intra[p, 0], a_intra[p, 1]]).astype(BF16),
            cat0([jnp.where(head0, vnp, 0.0), jnp.where(head0, 0.0, vnp)]).astype(BF16)))

    o = cat0(o)
    o = o * lax.rsqrt(_dot_xr(o * o, same_head) * (1.0 / HEAD_DIM) + EPS)
    for p in pairs:
        sl = slice(p * BLK, (p + 1) * BLK)
        zz = z_ref[0, :, sl]
        o_ref[0, :, sl] = (rows(o, p) * gn_ref[...] * (zz * _sigmoid(zz))).astype(o_ref.dtype)


def _deltanet(dnqkv, z, bg, conv_w, gn, batch, lp):
    nblk = lp // BLK
    grp = DN_GROUP * BLK
    pairs = WIDTH // grp
    phys = lambda j: (j + nblk - 1) % nblk
    x3 = dnqkv.reshape(batch, lp, 3 * WIDTH)
    z3 = z.reshape(batch, lp, WIDTH)
    bg3 = bg.reshape(batch, lp, 128)
    col = lambda off: pl.BlockSpec((1, BLK, grp), lambda b, p, j: (b, phys(j), off * pairs + p))
    cw = lambda off: pl.BlockSpec((4, grp), lambda b, p, j: (0, off * pairs + p))
    return pl.pallas_call(
        _dn_kernel2,
        grid=(batch, pairs, nblk),
        in_specs=[col(0), col(1), col(2),
                  pl.BlockSpec((1, BLK, grp), lambda b, p, j: (b, phys(j), p)),
                  pl.BlockSpec((1, BLK, 128), lambda b, p, j: (b, phys(j), 0)),
                  cw(0), cw(1), cw(2),
                  pl.BlockSpec((1, BLK), lambda b, p, j: (0, 0))],
        out_specs=pl.BlockSpec((1, BLK, grp), lambda b, p, j: (b, jnp.maximum(j - 1, 0), p)),
        out_shape=jax.ShapeDtypeStruct((batch, (nblk - 1) * BLK, WIDTH), BF16),
        scratch_shapes=[pltpu.VMEM((3, BLK + 8, grp), F32), pltpu.VMEM((DN_GROUP, BLK, BLK), F32)],
        compiler_params=pltpu.CompilerParams(
            dimension_semantics=("parallel", "parallel", "arbitrary"), vmem_limit_bytes=VMEM_LIMIT),
        name="deltanet",
    )(x3, x3, x3, z3, bg3, conv_w, conv_w, conv_w, gn)


def _merge_kernel(osb_ref, odn_ref, gate_ref, x_ref, wsb_ref, wdn_ref, wo_ref, h_ref):
    gate = gate_ref[0].astype(F32)
    mix = (gate[:, :D_MODEL] * _dot(osb_ref[...], wsb_ref[...])
           + gate[:, D_MODEL:] * _dot(odn_ref[...], wdn_ref[...]))
    h_ref[...] = x_ref[...] + _dot(mix.astype(BF16), wo_ref[...])


def _merge(o_sb, o_dn, gates, x2, wsb, wdn, wo, batch, seq, lp):
    n = batch * seq
    rb = ROWS_OUT
    per = seq // rb
    row = lambda c: pl.BlockSpec((rb, c), lambda i: (i, 0))
    const = lambda r, c: pl.BlockSpec((r, c), lambda i: (0, 0))
    return pl.pallas_call(
        _merge_kernel,
        grid=(n // rb,),
        in_specs=[row(WIDTH), row(WIDTH),
                  pl.BlockSpec((1, rb, 2 * D_MODEL), lambda i: (i // per, i % per, 0)),
                  row(D_MODEL), const(WIDTH, D_MODEL), const(WIDTH, D_MODEL), const(D_MODEL, D_MODEL)],
        out_specs=row(D_MODEL),
        out_shape=jax.ShapeDtypeStruct((n, D_MODEL), F32),
        compiler_params=pltpu.CompilerParams(dimension_semantics=("parallel",),
                                             vmem_limit_bytes=VMEM_LIMIT),
        name="merge",
    )(o_sb.reshape(n, WIDTH), o_dn.reshape(n, WIDTH), gates.reshape(batch, lp, 2 * D_MODEL),
      x2, wsb, wdn, wo)


def _top_rows(s, k, payload=None):
    nrow = s.shape[0]
    rid = _iota(s.shape, 0)
    vals, picks = [], []
    for _ in range(k):
        m = jnp.max(s, axis=0, keepdims=True)
        am = jnp.min(jnp.where(s == m, rid, nrow), axis=0, keepdims=True)
        hit = rid == am
        vals.append(m)
        picks.append(am if payload is None
                     else jnp.max(jnp.where(hit, payload, -1), axis=0, keepdims=True))
        s = jnp.where(hit, -jnp.inf, s)
    return jnp.concatenate(vals, axis=0), jnp.concatenate(picks, axis=0)


def _route_kernel(h_ref, g2_ref, wq_ref, keys_ref, xn_ref, idx_ref, gate_ref):
    x = h_ref[...]
    ms = jnp.mean(x * x, axis=-1, keepdims=True)
    xn = x * lax.rsqrt(ms + EPS) * g2_ref[...]
    xn_ref[...] = xn
    q = _dot(xn.astype(BF16), wq_ref[...])
    keys = (keys_ref[0].astype(BF16), keys_ref[1].astype(BF16))
    idx_rows, gate_rows = [], []
    for h in range(PEER_HEADS):
        tops = []
        for p in range(2):
            c0 = (2 * h + p) * PEER_KEYS
            s = _dot_nt(keys[p], q[:, c0:c0 + PEER_KEYS].astype(BF16))
            tops.append(_top_rows(s, PEER_TOPK))
        (s1, i1), (s2, i2) = tops
        brow = _iota((8, s1.shape[1]), 0)
        cs, ci = [s1[0:1] + s2], [i1[0:1] * PEER_KEYS + i2]
        for a in range(1, 8):
            keep = PEER_TOPK // (a + 1)
            sa = s1[a:a + 1] + s2[0:8]
            cs.append(sa if keep >= 8 else jnp.where(brow < keep, sa, -jnp.inf))
            ci.append(i1[a:a + 1] * PEER_KEYS + i2[0:8])
        cs.append(s1[8:16] + s2[0:1])
        ci.append(i1[8:16] * PEER_KEYS + i2[0:1])
        top_s, top_i = _top_rows(jnp.concatenate(cs, axis=0), PEER_TOPK, jnp.concatenate(ci, axis=0))
        e = jnp.exp(top_s - top_s[0:1])
        idx_rows.append(top_i)
        gate_rows.append(e / jnp.sum(e, axis=0, keepdims=True))
    idx_ref[...] = jnp.concatenate(idx_rows, axis=0).T
    gate_ref[...] = jnp.concatenate(gate_rows, axis=0).T


def _route(h1, g2, wq, keys):
    n = h1.shape[0]
    rb = ROWS_OUT
    return pl.pallas_call(
        _route_kernel,
        grid=(n // rb,),
        in_specs=[pl.BlockSpec((rb, D_MODEL), lambda i: (i, 0)),
                  pl.BlockSpec((1, D_MODEL), lambda i: (0, 0)),
                  pl.BlockSpec(wq.shape, lambda i: (0, 0)),
                  pl.BlockSpec(keys.shape, lambda i: (0, 0, 0))],
        out_specs=[pl.BlockSpec((rb, D_MODEL), lambda i: (i, 0)),
                   pl.BlockSpec((rb, PEER_HK), lambda i: (i, 0)),
                   pl.BlockSpec((rb, PEER_HK), lambda i: (i, 0))],
        out_shape=[jax.ShapeDtypeStruct((n, D_MODEL), F32),
                   jax.ShapeDtypeStruct((n, PEER_HK), jnp.int32),
                   jax.ShapeDtypeStruct((n, PEER_HK), F32)],
        compiler_params=pltpu.CompilerParams(dimension_semantics=("parallel",),
                                             vmem_limit_bytes=VMEM_LIMIT),
        name="route",
    )(h1, g2, wq, keys)


def _peer_kernel(idx_ref, idx_next_ref, gate_ref, xn_ref, h_ref, sel_ref, rep_ref, tab_hbm, out_ref,
                 buf_a, buf_b, act_ref, sem):
    i = pl.program_id(0)
    n = pl.num_programs(0)
    tb = PEER_TB
    rows = tb * PEER_HK
    sub = D_MODEL // 128
    unroll = PEER_UNROLL
    ones = jnp.ones((sub, 2 * 128), BF16)
    lanes = lambda t, k: t[:, k * 128:(k + 1) * 128]
    trow = _iota((tb, unroll * 128), 0)
    lane_tok = _iota((tb, unroll * 128), 1) >> 7

    def issue_token(src_idx, row0, t, buf, sl, e0=0, e1=PEER_HK):
        base = t * PEER_HK
        for e in range(e0, e1):
            pltpu.make_async_copy(tab_hbm.at[src_idx[row0 + t, e]], buf.at[base + e], sem.at[sl]).start(
                priority=e % DMA_THREADS)

    def drain(buf, sl):
        pltpu.make_async_copy(tab_hbm.at[pl.ds(0, rows)], buf, sem.at[sl]).wait()

    def phase(buf, sl, tok0, nxt_buf, nxt_sl, nxt_idx, nxt_row0):
        drain(buf, sl)

        def tile_rows(t):
            return buf[pl.ds(pl.multiple_of(t * PEER_HK, PEER_HK), PEER_HK)]

        def act_body(g, carry):
            prods = []
            for k in range(unroll):
                t = g * unroll + k
                issue_token(nxt_idx, nxt_row0, t, nxt_buf, nxt_sl, 0, PEER_HK // 2)
                u = lax.bitcast_convert_type(tile_rows(t) << 16, F32)
                prods.append((u * xn_ref[tok0 + t][None]).reshape(PEER_HK * sub, 128).astype(BF16))
            part = _dot(sel_ref[...], jnp.concatenate(prods, axis=1))
            hi, lo = _split(jnp.concatenate([lanes(part, k) for k in range(unroll)], axis=0))
            sums = _dot_nt(ones, jnp.concatenate([hi, lo], axis=1))
            for k in range(unroll):
                act_ref[pl.ds(g * unroll + k, 1), :] = lanes(sums, k)[0:1]
            return carry

        lax.fori_loop(0, tb // unroll, act_body, 0)

        act = act_ref[...]
        wgt = (0.5 * act * (1.0 + lax.erf(act * (2.0 ** -0.5))) * gate_ref[tok0:tok0 + tb, :]).astype(BF16)

        def mix_body(g, carry):
            pick = jnp.where(trow == g * unroll + lane_tok, 1.0, 0.0).astype(BF16)
            col = _dot_tn(wgt, pick)
            wide = _dot(rep_ref[...], col.astype(BF16))
            for k in range(unroll):
                t = g * unroll + k
                issue_token(nxt_idx, nxt_row0, t, nxt_buf, nxt_sl, PEER_HK // 2, PEER_HK)
                v = lax.bitcast_convert_type(tile_rows(t) & jnp.uint32(0xFFFF0000), F32)
                out_ref[tok0 + t] = h_ref[tok0 + t] + jnp.sum(
                    lanes(wide, k).reshape(PEER_HK, sub, 128) * v, axis=0)
            return carry

        lax.fori_loop(0, tb // unroll, mix_body, 0)

    @pl.when(i == 0)
    def _():
        def body(t, carry):
            issue_token(idx_ref, 0, t, buf_a, 0)
            return carry
        lax.fori_loop(0, tb, body, 0)

    phase(buf_a, 0, 0, buf_b, 1, idx_ref, tb)
    phase(buf_b, 1, tb, buf_a, 0, idx_next_ref, 0)

    @pl.when(i == n - 1)
    def _():
        drain(buf_a, 0)


def _pack_tables(u_tab, v_tab):
    half = lambda t: lax.bitcast_convert_type(t.astype(BF16), jnp.uint16).astype(jnp.uint32)
    packed = half(u_tab) | (half(v_tab) << 16)
    return packed.reshape(u_tab.shape[0], D_MODEL // 128, 128)


def _peer(idx, gate, xn, h1, u_tab, v_tab):
    n = h1.shape[0]
    tb = 2 * PEER_TB
    steps = n // tb
    rows = PEER_TB * PEER_HK
    sub = D_MODEL // 128
    tiles = lambda t: t.reshape(n, sub, 128)
    e = np.arange(PEER_HK)
    sel = jnp.asarray(e[:, None] == (np.arange(PEER_HK * sub)[None, :] // sub), BF16)
    smem = lambda f: pl.BlockSpec((tb, PEER_HK), f, memory_space=pltpu.SMEM)
    tok = pl.BlockSpec((tb, sub, 128), lambda i: (i, 0, 0))
    out = pl.pallas_call(
        _peer_kernel,
        grid=(steps,),
        in_specs=[smem(lambda i: (i, 0)),
                  smem(lambda i: (jnp.minimum(i + 1, steps - 1), 0)),
                  pl.BlockSpec((tb, PEER_HK), lambda i: (i, 0)),
                  tok, tok,
                  pl.BlockSpec((PEER_HK, PEER_HK * sub), lambda i: (0, 0)),
                  pl.BlockSpec((PEER_HK * sub, PEER_HK), lambda i: (0, 0)),
                  pl.BlockSpec(memory_space=pl.ANY)],
        out_specs=tok,
        out_shape=jax.ShapeDtypeStruct((n, sub, 128), F32),
        scratch_shapes=[pltpu.VMEM((rows, sub, 128), jnp.uint32), pltpu.VMEM((rows, sub, 128), jnp.uint32),
                        pltpu.VMEM((PEER_TB, PEER_HK), F32), pltpu.SemaphoreType.DMA((2,))],
        compiler_params=pltpu.CompilerParams(dimension_semantics=("arbitrary",),
                                             vmem_limit_bytes=VMEM_LIMIT),
        name="peer",
    )(idx, idx, gate, tiles(xn), tiles(h1), sel, sel.T, _pack_tables(u_tab, v_tab))
    return out.reshape(n, D_MODEL)


def _constants():
    r = np.arange(WIDTH)
    ph = (r[:, None] // HEAD_DIM == r[None, :] // HEAD_DIM).astype(np.float32) / HEAD_DIM
    s = np.arange(BLK)
    later = (s[:, None] > s[None, :]).astype(np.float32)
    half = np.concatenate([later, np.ones((BLK, BLK), np.float32)], axis=1)
    tri = np.concatenate([half, half], axis=0)
    return jnp.asarray(ph, BF16), jnp.asarray(tri, BF16)


def _layer(x, meta_tokens, norm1_g, w_in, sb_q_norm_g, sb_k_norm_g, dn_conv_w, dn_a_log, dn_dt_bias,
           dn_out_norm_g, w_sb_out, w_dn_out, w_o, norm2_g, peer_w_q, peer_sub_keys, peer_u, peer_v):
    batch, seq, d = x.shape
    lp = seq + BLK
    ph, tri = _constants()

    tail = jnp.concatenate([jnp.zeros((PAD, d), x.dtype), meta_tokens.astype(x.dtype)], axis=0)
    hp = jnp.concatenate([x, jnp.broadcast_to(tail[None], (batch, BLK, d))], axis=1).reshape(batch * lp, d)

    c_ba = 3 * WIDTH + 3 * WIDTH + WIDTH
    w_all = jnp.concatenate([w_in[:, :c_ba], w_in[:, c_ba + 2 * HEADS:], w_in[:, c_ba:c_ba + 2 * HEADS],
                             jnp.zeros((d, 128 - 2 * HEADS), w_in.dtype)], axis=1).astype(BF16)
    gq = (jnp.tile(sb_q_norm_g.astype(F32), HEADS) * (HEAD_DIM ** -0.5))[None]
    gk = jnp.tile(sb_k_norm_g.astype(F32), HEADS)[None]
    lane_pad = lambda t: jnp.pad(t.astype(F32), (HEADS, 128 - 2 * HEADS))[None]
    acoef = lane_pad(-jnp.exp(dn_a_log.astype(F32)))
    dtb = lane_pad(dn_dt_bias)

    sbqkv, dnqkv, z, gates, bg = _inproj(hp, norm1_g.astype(F32)[None], w_all, ph, gq, gk, acoef, dtb)
    o_sb = _sb_attn(sbqkv, tri, batch, lp)
    gn = jnp.tile(dn_out_norm_g.astype(F32), 2)[None]
    o_dn = _deltanet(dnqkv, z, bg, dn_conv_w.astype(F32), gn, batch, lp)
    h1 = _merge(o_sb, o_dn, gates, x.reshape(batch * seq, d), w_sb_out.astype(BF16),
                w_dn_out.astype(BF16), w_o.astype(BF16), batch, seq, lp)
    xn2, idx, gate = _route(h1, norm2_g.astype(F32)[None], peer_w_q.astype(BF16), peer_sub_keys)
    h2 = _peer(idx, gate, xn2, h1, peer_u, peer_v)
    return h2.reshape(batch, seq, d)


def kernel(x, meta_tokens, norm1_g, w_in, sb_q_norm_g, sb_k_norm_g, dn_conv_w, dn_a_log, dn_dt_bias,
           dn_out_norm_g, w_sb_out, w_dn_out, w_o, norm2_g, peer_w_q, peer_sub_keys, peer_u, peer_v):
    assert norm1_g.shape[0] == 1, "one layer"
    return _layer(x, meta_tokens, norm1_g[0], w_in[0], sb_q_norm_g[0], sb_k_norm_g[0], dn_conv_w[0],
                  dn_a_log[0], dn_dt_bias[0], dn_out_norm_g[0], w_sb_out[0], w_dn_out[0], w_o[0],
                  norm2_g[0], peer_w_q[0], peer_sub_keys[0], peer_u[0], peer_v[0])
```

```python
import functools

import numpy as np
import jax
import jax.numpy as jnp
from jax import lax
from jax.experimental import pallas as pl
from jax.experimental.pallas import tpu as pltpu

F32 = jnp.float32
BF16 = jnp.bfloat16

D_MODEL = 1024
N_META = 16
BLK = 128
PAD = BLK - N_META
HEADS = 8
HEAD_DIM = 64
WIDTH = HEADS * HEAD_DIM
CHUNK = 64
PEER_HEADS = 8
PEER_KEYS = 128
PEER_TOPK = 16
PEER_HK = PEER_HEADS * PEER_TOPK
EPS = 1e-6

C_SB = 0
C_DN = 3 * WIDTH
C_Z = C_DN + 3 * WIDTH
C_GATE = C_Z + WIDTH
C_BA = C_GATE + 2 * D_MODEL
C_END = C_BA + 128

ROWS_IN = 256
ROWS_OUT = 256
SB_SPAN = 4
DN_GROUP = 4
DMA_THREADS = 2
PEER_TB = 16
PEER_BUFS = 4
PEER_AHEAD = 2
PEER_UNROLL = 8
VMEM_LIMIT = 56 * 1024 * 1024


def _dot(a, b):
    return jnp.dot(a, b, preferred_element_type=F32)


def _dot_nt(a, b):
    return lax.dot_general(a, b, (((1,), (1,)), ((), ())), preferred_element_type=F32)


def _dot_tn(a, b):
    return lax.dot_general(a, b, (((0,), (0,)), ((), ())), preferred_element_type=F32)


def _split(a):
    hi = a.astype(BF16)
    lo = (a - hi.astype(F32)).astype(BF16)
    return hi, lo


def _dot_xr(a, b_exact):
    hi, lo = _split(a)
    return _dot(jnp.concatenate([hi, lo], axis=1), jnp.concatenate([b_exact, b_exact], axis=0))


def _dot_xl(a_exact, b):
    hi, lo = _split(b)
    return _dot(jnp.concatenate([a_exact, a_exact], axis=1), jnp.concatenate([hi, lo], axis=0))


def _dot3(a, b):
    ah, al = _split(a)
    bh, bl = _split(b)
    return _dot(jnp.concatenate([ah, ah, al], axis=1), jnp.concatenate([bh, bl, bh], axis=0))


def _sigmoid(x):
    return 1.0 / (1.0 + jnp.exp(-x))


def _softplus(x):
    return jnp.maximum(x, 0.0) + jnp.log1p(jnp.exp(-jnp.abs(x)))


def _iota(shape, dim):
    return lax.broadcasted_iota(jnp.int32, shape, dim)


def _inproj_kernel(x_ref, g1_ref, w_ref, ph_ref, gq_ref, gk_ref, acoef_ref, dtb_ref,
                   sb_ref, dn_ref, z_ref, gate_ref, bg_ref):
    x = x_ref[...]
    ms = jnp.mean(x * x, axis=-1, keepdims=True)
    xn = (x * lax.rsqrt(ms + EPS) * g1_ref[...]).astype(BF16)

    def proj(c0, c1):
        return _dot(xn, w_ref[:, c0:c1])

    def head_norm(t, g):
        msh = _dot((t * t).astype(BF16), ph_ref[...])
        return (t * lax.rsqrt(msh + EPS) * g).astype(BF16)

    sb_ref[:, 0:WIDTH] = head_norm(proj(C_SB, C_SB + WIDTH), gq_ref[...])
    sb_ref[:, WIDTH:2 * WIDTH] = head_norm(proj(C_SB + WIDTH, C_SB + 2 * WIDTH), gk_ref[...])
    sb_ref[:, 2 * WIDTH:3 * WIDTH] = proj(C_SB + 2 * WIDTH, C_SB + 3 * WIDTH).astype(BF16)
    dn_ref[...] = proj(C_DN, C_DN + 3 * WIDTH)
    z_ref[...] = proj(C_Z, C_Z + WIDTH)
    gate_ref[...] = _sigmoid(proj(C_GATE, C_GATE + 2 * D_MODEL)).astype(BF16)
    ba = proj(C_BA, C_END)
    lane = _iota(ba.shape, 1)
    bg_ref[...] = jnp.where(lane < HEADS, _sigmoid(ba), acoef_ref[...] * _softplus(ba + dtb_ref[...]))


def _inproj(hp, g1, w_all, ph, gq, gk, acoef, dtb):
    n = hp.shape[0]
    rb = ROWS_IN
    const = lambda shape: pl.BlockSpec(shape, lambda i: (0, 0))
    row = lambda c: pl.BlockSpec((rb, c), lambda i: (i, 0))
    return pl.pallas_call(
        _inproj_kernel,
        grid=(n // rb,),
        in_specs=[row(D_MODEL), const((1, D_MODEL)), const((D_MODEL, C_END)), const((WIDTH, WIDTH)),
                  const((1, WIDTH)), const((1, WIDTH)), const((1, 128)), const((1, 128))],
        out_specs=[row(3 * WIDTH), row(3 * WIDTH), row(WIDTH), row(2 * D_MODEL), row(128)],
        out_shape=[jax.ShapeDtypeStruct((n, 3 * WIDTH), BF16),
                   jax.ShapeDtypeStruct((n, 3 * WIDTH), F32),
                   jax.ShapeDtypeStruct((n, WIDTH), F32),
                   jax.ShapeDtypeStruct((n, 2 * D_MODEL), BF16),
                   jax.ShapeDtypeStruct((n, 128), F32)],
        compiler_params=pltpu.CompilerParams(dimension_semantics=("parallel",),
                                             vmem_limit_bytes=VMEM_LIMIT),
        name="inproj",
    )(hp, g1, w_all, ph, gq, gk, acoef, dtb)


def _sb_kernel(q_ref, k_ref, v_ref, tri_ref, o_ref, *, nblk):
    qi = pl.program_id(2)
    q = q_ref[0]
    lane = _iota((BLK, BLK), 1)
    row = _iota((BLK, BLK), 0)
    head_lo = lane < HEAD_DIM
    qf = q.astype(F32)
    q2 = jnp.concatenate([jnp.where(head_lo, qf, 0.0), jnp.where(head_lo, 0.0, qf)], axis=0).astype(BF16)
    tri = tri_ref[...]
    two = 2 * BLK

    cols = lambda t, j: t[:, j * BLK:(j + 1) * BLK]

    def scores(k, nb, vis):
        z = _dot_nt(q2, k)
        neg_abs = lax.bitcast_convert_type(lax.bitcast_convert_type(z, jnp.uint32) | jnp.uint32(0x80000000), F32)
        soft = jnp.log(1.0 + jnp.exp(neg_abs))
        log_beta = jnp.minimum(z, 0.0) - soft
        log_om = log_beta - z
        if vis is not None:
            log_om = jnp.where(vis, log_om, 0.0)
        hi, lo = _split(log_om)
        stacked = jnp.concatenate(
            [jnp.concatenate([cols(hi, j), cols(lo, j)], axis=1) for j in range(nb)], axis=0)
        return log_beta, _dot(stacked, tri)

    def finish(log_beta, rt, nb, carry, acc, v, vis):
        later = carry
        parts = [None] * nb
        for j in reversed(range(nb)):
            rows = rt[j * two:(j + 1) * two]
            parts[j] = cols(log_beta, j) + rows[:, :BLK] + later
            later = later + rows[:, BLK:]
        a = jnp.exp(jnp.concatenate(parts, axis=1))
        if vis is not None:
            a = jnp.where(vis, a, 0.0)
        return later, acc + _dot(a.astype(BF16), v)

    def keys(ref, kphys, nb):
        return ref[0, pl.ds(pl.multiple_of(kphys * BLK, BLK), nb * BLK), :]

    def tile(kphys, nb, state):
        log_beta, rt = scores(keys(k_ref, kphys, nb), nb, None)
        return finish(log_beta, rt, nb, state[0], state[1], keys(v_ref, kphys, nb), None)

    stack2 = lambda m: jnp.concatenate([m, m], axis=0)
    vis_d, vis_0 = stack2(lane < row), stack2(lane >= PAD)
    lb2, rt2 = scores(jnp.concatenate([keys(k_ref, qi, 1), keys(k_ref, nblk - 1, 1)], axis=0), 2,
                      jnp.concatenate([vis_d, vis_0], axis=1))
    zf = jnp.zeros((two, BLK), F32)
    state = finish(cols(lb2, 0), rt2[:two], 1, zf, zf, keys(v_ref, qi, 1), vis_d)

    rem = qi % 4
    four = (qi // 4) % 2
    state = lax.fori_loop(0, rem, lambda it, s: tile(qi - 1 - it, 1, s), state)
    state = lax.fori_loop(0, four, lambda it, s: tile(qi - rem - 4, 4, s), state)
    state = lax.fori_loop(0, qi // 8, lambda it, s: tile(qi - rem - 4 * four - 8 * (it + 1), 8, s), state)
    carry, acc = finish(cols(lb2, 1), rt2[two:], 1, state[0], state[1], keys(v_ref, nblk - 1, 1), vis_0)
    o_ref[0] = jnp.where(head_lo, acc[:BLK], acc[BLK:]).astype(o_ref.dtype)


def _sb_attn(sbqkv, tri, batch, lp):
    nblk = lp // BLK
    nq = nblk - 1
    pairs = WIDTH // BLK
    x3 = sbqkv.reshape(batch, lp, 3 * WIDTH)
    return pl.pallas_call(
        functools.partial(_sb_kernel, nblk=nblk),
        grid=(batch, pairs, nq),
        in_specs=[pl.BlockSpec((1, BLK, BLK), lambda b, p, i: (b, i, p)),
                  pl.BlockSpec((1, lp, BLK), lambda b, p, i: (b, 0, pairs + p)),
                  pl.BlockSpec((1, lp, BLK), lambda b, p, i: (b, 0, 2 * pairs + p)),
                  pl.BlockSpec((2 * BLK, 2 * BLK), lambda b, p, i: (0, 0))],
        out_specs=pl.BlockSpec((1, BLK, BLK), lambda b, p, i: (b, i, p)),
        out_shape=jax.ShapeDtypeStruct((batch, nq * BLK, WIDTH), BF16),
        compiler_params=pltpu.CompilerParams(
            dimension_semantics=("parallel", "parallel", "arbitrary"), vmem_limit_bytes=VMEM_LIMIT),
        name="sb_attn",
    )(x3, x3, x3, tri)


def _dn_kernel(q_ref, k_ref, v_ref, z_ref, bg_ref, cwq_ref, cwk_ref, cwv_ref, gn_ref,
               o_ref, xbuf, s_ref):
    j = pl.program_id(2)

    @pl.when(j == 0)
    def _():
        xbuf[...] = jnp.zeros_like(xbuf)
        s_ref[...] = jnp.zeros_like(s_ref)

    def conv_silu(c, x_ref, cw_ref):
        xbuf[c, 8:8 + BLK, :] = x_ref[0]
        w = cw_ref[...]
        y = (w[3:4] * xbuf[c, 8:8 + BLK, :] + w[2:3] * xbuf[c, 7:7 + BLK, :]
             + w[1:2] * xbuf[c, 6:6 + BLK, :] + w[0:1] * xbuf[c, 5:5 + BLK, :])
        xbuf[c, 0:8, :] = xbuf[c, BLK:BLK + 8, :]
        return y * _sigmoid(y)

    lane = _iota((BLK, BLK), 1)
    row = _iota((BLK, BLK), 0)
    lane_head = lane >> 6
    row_head = row >> 6

    def pattern(cond):
        return jnp.where(cond, 1.0, 0.0).astype(BF16)

    same_head = pattern(lane_head == row_head)
    low_incl = (lane_head == row_head) & (lane <= row)
    low_strict = (lane_head == row_head) & (lane < row)
    cum_incl = pattern(low_incl)

    def l2(t):
        ss = _dot_xr(t * t, same_head)
        return t * lax.rsqrt(ss + EPS)

    q_all = conv_silu(0, q_ref, cwq_ref)
    k_all = conv_silu(1, k_ref, cwk_ref)
    v_all = conv_silu(2, v_ref, cwv_ref)
    shared = dict(j=j, bg=bg_ref[0], lane=lane, row=row, lane_head=lane_head, row_head=row_head,
                  pattern=pattern, same_head=same_head, low_incl=low_incl, low_strict=low_strict,
                  cum_incl=cum_incl, l2=l2)
    for pr in range(DN_GROUP):
        sl = slice(pr * BLK, (pr + 1) * BLK)
        _dn_pair(pr, sl, pl.program_id(1) * DN_GROUP + pr, q_all[:, sl], k_all[:, sl], v_all[:, sl],
                 z_ref, gn_ref, o_ref, s_ref, **shared)


def _dn_pair(pr, sl, hp, q, k, v, z_ref, gn_ref, o_ref, s_ref, *, j, bg, lane, row, lane_head, row_head,
             pattern, same_head, low_incl, low_strict, cum_incl, l2):
    qk_n = l2(jnp.concatenate([q, k], axis=0))
    q = qk_n[:BLK] * (HEAD_DIM ** -0.5)
    k = qk_n[BLK:]
    live = (row >= PAD) | (j > 0)
    picks = jnp.concatenate([pattern(row == 2 * hp + lane_head),
                             pattern(row == HEADS + 2 * hp + lane_head)], axis=1)
    beta_g = _dot_xr(bg, picks)
    beta = jnp.where(live, beta_g[:, :BLK], 0.0)
    g = jnp.where(live, beta_g[:, BLK:], 0.0)
    sums = _dot_xl(jnp.concatenate([cum_incl, same_head], axis=0), g)
    gc = sums[:BLK]
    g_last = sums[BLK:]
    eg = jnp.exp(gc)
    k_beta = k * beta
    v_beta = v * beta
    kbg = k_beta * eg
    q_decay = q * eg
    k_tail = k * jnp.exp(g_last - gc)

    eye = jnp.where(lane == row, 1.0, 0.0).astype(F32)
    gc_heads = _dot_xr(gc, jnp.concatenate([pattern(row == 0), pattern(row == HEAD_DIM)], axis=1))
    k16 = k.astype(BF16)
    rhs_uw = jnp.concatenate([v_beta, kbg], axis=1)
    u_parts, w_parts, a_parts = [], [], []
    for h in range(2):
        mine = lane_head == h
        gch = gc_heads[:, h * BLK:(h + 1) * BLK]
        decay = jnp.exp(jnp.where(low_incl, gch - gch.T, -jnp.inf))
        both = _dot_nt(jnp.concatenate([jnp.where(mine, k_beta, 0.0), jnp.where(mine, q, 0.0)],
                                       axis=0).astype(BF16), k16)
        m = jnp.where(low_strict, both[:BLK] * decay, 0.0)
        a_parts.append(jnp.where(low_incl, both[BLK:] * decay, 0.0))
        a_k = -m
        s_k = eye + a_k
        a_k = _dot3(a_k, a_k)
        for _ in range(4):
            prod = _dot3(a_k, jnp.concatenate([a_k, s_k], axis=1))
            a_k, s_k = prod[:, :BLK], s_k + prod[:, BLK:]
        t_inv = s_k + _dot3(a_k, s_k)
        uw = _dot3(t_inv, rhs_uw)
        u_parts.append(uw[:, :BLK])
        w_parts.append(uw[:, BLK:])
    head0 = lane_head == 0
    u = jnp.where(head0, u_parts[0], u_parts[1])
    w = jnp.where(head0, w_parts[0], w_parts[1])

    block_diag = lane_head == row_head
    state = s_ref[pr]
    vn_parts, inter_parts = [], []
    for c in range(2):
        r = slice(c * CHUNK, (c + 1) * CHUNK)
        ws_qs = _dot(jnp.concatenate([w[r], q_decay[r]], axis=0).astype(BF16), state.astype(BF16))
        vn = u[r] - ws_qs[:CHUNK]
        inter_parts.append(ws_qs[CHUNK:])
        gl = jnp.exp(g_last[c * CHUNK:c * CHUNK + 1, :])
        upd = _dot_tn(k_tail[r].astype(BF16), vn.astype(BF16))
        state = jnp.where(block_diag, state * gl + upd, 0.0)
        vn_parts.append(vn)
    s_ref[pr] = state
    vn = jnp.concatenate(vn_parts, axis=0)
    o = jnp.concatenate(inter_parts, axis=0) + _dot(
        jnp.concatenate(a_parts, axis=1).astype(BF16),
        jnp.concatenate([jnp.where(head0, vn, 0.0), jnp.where(head0, 0.0, vn)], axis=0).astype(BF16))

    ms = _dot_xr(o * o, same_head) * (1.0 / HEAD_DIM)
    zz = z_ref[0, :, sl]
    o_ref[0, :, sl] = (o * lax.rsqrt(ms + EPS) * gn_ref[...] * (zz * _sigmoid(zz))).astype(o_ref.dtype)


def _dn_kernel2(q_ref, k_ref, v_ref, z_ref, bg_ref, cwq_ref, cwk_ref, cwv_ref, gn_ref,
                o_ref, xbuf, s_ref):
    j = pl.program_id(2)
    npair = DN_GROUP
    pairs = range(npair)

    @pl.when(j == 0)
    def _():
        xbuf[...] = jnp.zeros_like(xbuf)
        s_ref[...] = jnp.zeros_like(s_ref)

    def conv_silu(c, x_ref, cw_ref):
        xbuf[c, 8:8 + BLK, :] = x_ref[0]
        w = cw_ref[...]
        y = (w[3:4] * xbuf[c, 8:8 + BLK, :] + w[2:3] * xbuf[c, 7:7 + BLK, :]
             + w[1:2] * xbuf[c, 6:6 + BLK, :] + w[0:1] * xbuf[c, 5:5 + BLK, :])
        xbuf[c, 0:8, :] = xbuf[c, BLK:BLK + 8, :]
        return y * _sigmoid(y)

    lane = _iota((BLK, BLK), 1)
    row = _iota((BLK, BLK), 0)
    lane_head = lane >> 6
    row_head = row >> 6
    pattern = lambda cond: jnp.where(cond, 1.0, 0.0).astype(BF16)
    same_head = pattern(lane_head == row_head)
    block_diag = lane_head == row_head
    low_incl = block_diag & (lane <= row)
    low_strict = block_diag & (lane < row)
    head0 = lane_head == 0
    eye = jnp.where(lane == row, 1.0, 0.0).astype(F32)
    cols = lambda t, p: t[:, p * BLK:(p + 1) * BLK]
    rows = lambda t, p: t[p * BLK:(p + 1) * BLK]
    cat0 = lambda xs: jnp.concatenate(xs, axis=0)
    cat1 = lambda xs: jnp.concatenate(xs, axis=1)

    q_all = conv_silu(0, q_ref, cwq_ref)
    k_all = conv_silu(1, k_ref, cwk_ref)
    v_all = conv_silu(2, v_ref, cwv_ref)

    qk = cat0([cols(q_all, p) for p in pairs] + [cols(k_all, p) for p in pairs])
    qk = qk * lax.rsqrt(_dot_xr(qk * qk, same_head) + EPS)
    q = [rows(qk, p) * (HEAD_DIM ** -0.5) for p in pairs]
    k = [rows(qk, npair + p) for p in pairs]
    v = [cols(v_all, p) for p in pairs]

    live = (row >= PAD) | (j > 0)
    hp = [pl.program_id(1) * npair + p for p in pairs]
    picks = cat1([pattern(row == 2 * hp[p] + lane_head) for p in pairs]
                 + [pattern(row == HEADS + 2 * hp[p] + lane_head) for p in pairs])
    beta_g = _dot_xr(bg_ref[0], picks)
    beta = [jnp.where(live, cols(beta_g, p), 0.0) for p in pairs]
    g = [jnp.where(live, cols(beta_g, npair + p), 0.0) for p in pairs]
    sums = _dot_xl(cat0([pattern(low_incl), same_head]), cat1(g))
    gc = [cols(sums[:BLK], p) for p in pairs]
    g_last = [cols(sums[BLK:], p) for p in pairs]
    gc_heads = _dot_xr(cat0(gc), cat1([pattern(row == 0), pattern(row == HEAD_DIM)]))

    k_beta = [k[p] * beta[p] for p in pairs]
    eg = [jnp.exp(gc[p]) for p in pairs]
    rhs_uw = [cat1([v[p] * beta[p], k_beta[p] * eg[p]]) for p in pairs]
    q_decay = [q[p] * eg[p] for p in pairs]
    k_tail = [k[p] * jnp.exp(g_last[p] - gc[p]) for p in pairs]

    chains = [(p, h) for p in pairs for h in range(2)]
    both = [_dot_nt(cat0([jnp.where(lane_head == h, t, 0.0) for h in range(2) for t in (k_beta[p], q[p])]
                         ).astype(BF16), k[p].astype(BF16)) for p in pairs]
    a_k, a_intra = {}, {}
    for p, h in chains:
        gch = rows(gc_heads, p)[:, h * BLK:(h + 1) * BLK]
        decay = jnp.exp(jnp.where(low_incl, gch - gch.T, -jnp.inf))
        a_k[p, h] = -jnp.where(low_strict, rows(both[p], 2 * h) * decay, 0.0)
        a_intra[p, h] = jnp.where(low_incl, rows(both[p], 2 * h + 1) * decay, 0.0)

    s_k = {c: eye + a_k[c] for c in chains}
    a_k = {c: _dot3(a_k[c], a_k[c]) for c in chains}
    for _ in range(4):
        prod = {c: _dot3(a_k[c], cat1([a_k[c], s_k[c]])) for c in chains}
        a_k = {c: prod[c][:, :BLK] for c in chains}
        s_k = {c: s_k[c] + prod[c][:, BLK:] for c in chains}
    last = {c: _dot3(a_k[c], s_k[c]) for c in chains}
    uw = {c: _dot3(s_k[c] + last[c], rhs_uw[c[0]]) for c in chains}
    u = [jnp.where(head0, uw[p, 0][:, :BLK], uw[p, 1][:, :BLK]) for p in pairs]
    w = [jnp.where(head0, uw[p, 0][:, BLK:], uw[p, 1][:, BLK:]) for p in pairs]

    state = [s_ref[p] for p in pairs]
    vn = [[None, None] for _ in pairs]
    inter = [[None, None] for _ in pairs]
    for c in range(2):
        r = slice(c * CHUNK, (c + 1) * CHUNK)
        ws_qs = [_dot(cat0([w[p][r], q_decay[p][r]]).astype(BF16), state[p].astype(BF16)) for p in pairs]
        for p in pairs:
            vn[p][c] = u[p][r] - ws_qs[p][:CHUNK]
            inter[p][c] = ws_qs[p][CHUNK:]
        upd = [_dot_tn(k_tail[p][r].astype(BF16), vn[p][c].astype(BF16)) for p in pairs]
        for p in pairs:
            gl = jnp.exp(g_last[p][c * CHUNK:c * CHUNK + 1, :])
            state[p] = jnp.where(block_diag, state[p] * gl + upd[p], 0.0)
    o = []
    for p in pairs:
        s_ref[p] = state[p]
        vnp = cat0(vn[p])
        o.append(cat0(inter[p]) + _dot(
            cat1([a_intra[p, 0], a_intra[p, 1]]).astype(BF16),
            cat0([jnp.where(head0, vnp, 0.0), jnp.where(head0, 0.0, vnp)]).astype(BF16)))

    o = cat0(o)
    o = o * lax.rsqrt(_dot_xr(o * o, same_head) * (1.0 / HEAD_DIM) + EPS)
    for p in pairs:
        sl = slice(p * BLK, (p + 1) * BLK)
        zz = z_ref[0, :, sl]
        o_ref[0, :, sl] = (rows(o, p) * gn_ref[...] * (zz * _sigmoid(zz))).astype(o_ref.dtype)


def _deltanet(dnqkv, z, bg, conv_w, gn, batch, lp):
    nblk = lp // BLK
    grp = DN_GROUP * BLK
    pairs = WIDTH // grp
    phys = lambda j: (j + nblk - 1) % nblk
    x3 = dnqkv.reshape(batch, lp, 3 * WIDTH)
    z3 = z.reshape(batch, lp, WIDTH)
    bg3 = bg.reshape(batch, lp, 128)
    col = lambda off: pl.BlockSpec((1, BLK, grp), lambda b, p, j: (b, phys(j), off * pairs + p))
    cw = lambda off: pl.BlockSpec((4, grp), lambda b, p, j: (0, off * pairs + p))
    return pl.pallas_call(
        _dn_kernel2,
        grid=(batch, pairs, nblk),
        in_specs=[col(0), col(1), col(2),
                  pl.BlockSpec((1, BLK, grp), lambda b, p, j: (b, phys(j), p)),
                  pl.BlockSpec((1, BLK, 128), lambda b, p, j: (b, phys(j), 0)),
                  cw(0), cw(1), cw(2),
                  pl.BlockSpec((1, BLK), lambda b, p, j: (0, 0))],
        out_specs=pl.BlockSpec((1, BLK, grp), lambda b, p, j: (b, jnp.maximum(j - 1, 0), p)),
        out_shape=jax.ShapeDtypeStruct((batch, (nblk - 1) * BLK, WIDTH), BF16),
        scratch_shapes=[pltpu.VMEM((3, BLK + 8, grp), F32), pltpu.VMEM((DN_GROUP, BLK, BLK), F32)],
        compiler_params=pltpu.CompilerParams(
            dimension_semantics=("parallel", "parallel", "arbitrary"), vmem_limit_bytes=VMEM_LIMIT),
        name="deltanet",
    )(x3, x3, x3, z3, bg3, conv_w, conv_w, conv_w, gn)


def _merge_kernel(osb_ref, odn_ref, gate_ref, x_ref, wsb_ref, wdn_ref, wo_ref, h_ref):
    gate = gate_ref[0].astype(F32)
    mix = (gate[:, :D_MODEL] * _dot(osb_ref[...], wsb_ref[...])
           + gate[:, D_MODEL:] * _dot(odn_ref[...], wdn_ref[...]))
    h_ref[...] = x_ref[...] + _dot(mix.astype(BF16), wo_ref[...])


def _merge(o_sb, o_dn, gates, x2, wsb, wdn, wo, batch, seq, lp):
    n = batch * seq
    rb = ROWS_OUT
    per = seq // rb
    row = lambda c: pl.BlockSpec((rb, c), lambda i: (i, 0))
    const = lambda r, c: pl.BlockSpec((r, c), lambda i: (0, 0))
    return pl.pallas_call(
        _merge_kernel,
        grid=(n // rb,),
        in_specs=[row(WIDTH), row(WIDTH),
                  pl.BlockSpec((1, rb, 2 * D_MODEL), lambda i: (i // per, i % per, 0)),
                  row(D_MODEL), const(WIDTH, D_MODEL), const(WIDTH, D_MODEL), const(D_MODEL, D_MODEL)],
        out_specs=row(D_MODEL),
        out_shape=jax.ShapeDtypeStruct((n, D_MODEL), F32),
        compiler_params=pltpu.CompilerParams(dimension_semantics=("parallel",),
                                             vmem_limit_bytes=VMEM_LIMIT),
        name="merge",
    )(o_sb.reshape(n, WIDTH), o_dn.reshape(n, WIDTH), gates.reshape(batch, lp, 2 * D_MODEL),
      x2, wsb, wdn, wo)


def _top_rows(s, k, payload=None):
    nrow = s.shape[0]
    rid = _iota(s.shape, 0)
    vals, picks = [], []
    for _ in range(k):
        m = jnp.max(s, axis=0, keepdims=True)
        am = jnp.min(jnp.where(s == m, rid, nrow), axis=0, keepdims=True)
        hit = rid == am
        vals.append(m)
        picks.append(am if payload is None
                     else jnp.max(jnp.where(hit, payload, -1), axis=0, keepdims=True))
        s = jnp.where(hit, -jnp.inf, s)
    return jnp.concatenate(vals, axis=0), jnp.concatenate(picks, axis=0)


def _route_kernel(h_ref, g2_ref, wq_ref, keys_ref, xn_ref, idx_ref, gate_ref):
    x = h_ref[...]
    ms = jnp.mean(x * x, axis=-1, keepdims=True)
    xn = x * lax.rsqrt(ms + EPS) * g2_ref[...]
    xn_ref[...] = xn
    q = _dot(xn.astype(BF16), wq_ref[...])
    keys = (keys_ref[0].astype(BF16), keys_ref[1].astype(BF16))
    idx_rows, gate_rows = [], []
    for h in range(PEER_HEADS):
        tops = []
        for p in range(2):
            c0 = (2 * h + p) * PEER_KEYS
            s = _dot_nt(keys[p], q[:, c0:c0 + PEER_KEYS].astype(BF16))
            tops.append(_top_rows(s, PEER_TOPK))
        (s1, i1), (s2, i2) = tops
        brow = _iota((8, s1.shape[1]), 0)
        cs, ci = [s1[0:1] + s2], [i1[0:1] * PEER_KEYS + i2]
        for a in range(1, 8):
            keep = PEER_TOPK // (a + 1)
            sa = s1[a:a + 1] + s2[0:8]
            cs.append(sa if keep >= 8 else jnp.where(brow < keep, sa, -jnp.inf))
            ci.append(i1[a:a + 1] * PEER_KEYS + i2[0:8])
        cs.append(s1[8:16] + s2[0:1])
        ci.append(i1[8:16] * PEER_KEYS + i2[0:1])
        top_s, top_i = _top_rows(jnp.concatenate(cs, axis=0), PEER_TOPK, jnp.concatenate(ci, axis=0))
        e = jnp.exp(top_s - top_s[0:1])
        idx_rows.append(top_i)
        gate_rows.append(e / jnp.sum(e, axis=0, keepdims=True))
    idx_ref[...] = jnp.concatenate(idx_rows, axis=0).T
    gate_ref[...] = jnp.concatenate(gate_rows, axis=0).T


def _route(h1, g2, wq, keys):
    n = h1.shape[0]
    rb = ROWS_OUT
    return pl.pallas_call(
        _route_kernel,
        grid=(n // rb,),
        in_specs=[pl.BlockSpec((rb, D_MODEL), lambda i: (i, 0)),
                  pl.BlockSpec((1, D_MODEL), lambda i: (0, 0)),
                  pl.BlockSpec(wq.shape, lambda i: (0, 0)),
                  pl.BlockSpec(keys.shape, lambda i: (0, 0, 0))],
        out_specs=[pl.BlockSpec((rb, D_MODEL), lambda i: (i, 0)),
                   pl.BlockSpec((rb, PEER_HK), lambda i: (i, 0)),
                   pl.BlockSpec((rb, PEER_HK), lambda i: (i, 0))],
        out_shape=[jax.ShapeDtypeStruct((n, D_MODEL), F32),
                   jax.ShapeDtypeStruct((n, PEER_HK), jnp.int32),
                   jax.ShapeDtypeStruct((n, PEER_HK), F32)],
        compiler_params=pltpu.CompilerParams(dimension_semantics=("parallel",),
                                             vmem_limit_bytes=VMEM_LIMIT),
        name="route",
    )(h1, g2, wq, keys)


def _peer_kernel(idx_ref, idx_next_ref, gate_ref, xn_ref, h_ref, sel_ref, rep_ref, tab_hbm, out_ref,
                 *scratch):
    bufs, act_ref, sem = scratch[:-2], scratch[-2], scratch[-1]
    i = pl.program_id(0)
    n = pl.num_programs(0)
    tb = PEER_TB
    rows = tb * PEER_HK
    sub = D_MODEL // 128
    unroll = PEER_UNROLL
    ones = jnp.ones((sub, 2 * 128), BF16)
    lanes = lambda t, k: t[:, k * 128:(k + 1) * 128]
    trow = _iota((tb, unroll * 128), 0)
    lane_tok = _iota((tb, unroll * 128), 1) >> 7

    def issue_token(src_idx, row0, t, buf, sl, e0=0, e1=PEER_HK):
        base = t * PEER_HK
        for e in range(e0, e1):
            pltpu.make_async_copy(tab_hbm.at[src_idx[row0 + t, e]], buf.at[base + e], sem.at[sl]).start(
                priority=e % DMA_THREADS)

    def drain(buf, sl):
        pltpu.make_async_copy(tab_hbm.at[pl.ds(0, rows)], buf, sem.at[sl]).wait()

    def phase(buf, sl, tok0, nxt_buf, nxt_sl, nxt_idx, nxt_row0):
        drain(buf, sl)

        def tile_rows(t):
            return buf[pl.ds(pl.multiple_of(t * PEER_HK, PEER_HK), PEER_HK)]

        def act_body(g, carry):
            prods = []
            for k in range(unroll):
                t = g * unroll + k
                issue_token(nxt_idx, nxt_row0, t, nxt_buf, nxt_sl, 0, PEER_HK // 2)
                u = lax.bitcast_convert_type(tile_rows(t) << 16, F32)
                prods.append((u * xn_ref[tok0 + t][None]).reshape(PEER_HK * sub, 128).astype(BF16))
            part = _dot(sel_ref[...], jnp.concatenate(prods, axis=1))
            hi, lo = _split(jnp.concatenate([lanes(part, k) for k in range(unroll)], axis=0))
            sums = _dot_nt(ones, jnp.concatenate([hi, lo], axis=1))
            for k in range(unroll):
                act_ref[pl.ds(g * unroll + k, 1), :] = lanes(sums, k)[0:1]
            return carry

        lax.fori_loop(0, tb // unroll, act_body, 0)

        act = act_ref[...]
        wgt = (0.5 * act * (1.0 + lax.erf(act * (2.0 ** -0.5))) * gate_ref[tok0:tok0 + tb, :]).astype(BF16)

        def mix_body(g, carry):
            pick = jnp.where(trow == g * unroll + lane_tok, 1.0, 0.0).astype(BF16)
            col = _dot_tn(wgt, pick)
            wide = _dot(rep_ref[...], col.astype(BF16))
            for k in range(unroll):
                t = g * unroll + k
                issue_token(nxt_idx, nxt_row0, t, nxt_buf, nxt_sl, PEER_HK // 2, PEER_HK)
                v = lax.bitcast_convert_type(tile_rows(t) & jnp.uint32(0xFFFF0000), F32)
                out_ref[tok0 + t] = h_ref[tok0 + t] + jnp.sum(
                    lanes(wide, k).reshape(PEER_HK, sub, 128) * v, axis=0)
            return carry

        lax.fori_loop(0, tb // unroll, mix_body, 0)

    nbuf = len(bufs)
    ahead = PEER_AHEAD

    @pl.when(i == 0)
    def _():
        def body(t, carry):
            for p in range(ahead):
                issue_token(idx_ref, p * tb, t, bufs[p], p)
            return carry
        lax.fori_loop(0, tb, body, 0)

    for p in range(nbuf):
        q = p + ahead
        src, row0 = (idx_ref, q * tb) if q < nbuf else (idx_next_ref, (q - nbuf) * tb)
        phase(bufs[p], p, p * tb, bufs[q % nbuf], q % nbuf, src, row0)

    @pl.when(i == n - 1)
    def _():
        for p in range(ahead):
            drain(bufs[p], p)


def _pack_tables(u_tab, v_tab):
    half = lambda t: lax.bitcast_convert_type(t.astype(BF16), jnp.uint16).astype(jnp.uint32)
    packed = half(u_tab) | (half(v_tab) << 16)
    return packed.reshape(u_tab.shape[0], D_MODEL // 128, 128)


def _peer(idx, gate, xn, h1, u_tab, v_tab):
    n = h1.shape[0]
    tb = PEER_BUFS * PEER_TB
    steps = n // tb
    rows = PEER_TB * PEER_HK
    sub = D_MODEL // 128
    tiles = lambda t: t.reshape(n, sub, 128)
    e = np.arange(PEER_HK)
    sel = jnp.asarray(e[:, None] == (np.arange(PEER_HK * sub)[None, :] // sub), BF16)
    smem = lambda f: pl.BlockSpec((tb, PEER_HK), f, memory_space=pltpu.SMEM)
    tok = pl.BlockSpec((tb, sub, 128), lambda i: (i, 0, 0))
    out = pl.pallas_call(
        _peer_kernel,
        grid=(steps,),
        in_specs=[smem(lambda i: (i, 0)),
                  smem(lambda i: (jnp.minimum(i + 1, steps - 1), 0)),
                  pl.BlockSpec((tb, PEER_HK), lambda i: (i, 0)),
                  tok, tok,
                  pl.BlockSpec((PEER_HK, PEER_HK * sub), lambda i: (0, 0)),
                  pl.BlockSpec((PEER_HK * sub, PEER_HK), lambda i: (0, 0)),
                  pl.BlockSpec(memory_space=pl.ANY)],
        out_specs=tok,
        out_shape=jax.ShapeDtypeStruct((n, sub, 128), F32),
        scratch_shapes=[pltpu.VMEM((rows, sub, 128), jnp.uint32) for _ in range(PEER_BUFS)]
        + [pltpu.VMEM((PEER_TB, PEER_HK), F32), pltpu.SemaphoreType.DMA((PEER_BUFS,))],
        compiler_params=pltpu.CompilerParams(dimension_semantics=("arbitrary",),
                                             vmem_limit_bytes=VMEM_LIMIT),
        name="peer",
    )(idx, idx, gate, tiles(xn), tiles(h1), sel, sel.T, _pack_tables(u_tab, v_tab))
    return out.reshape(n, D_MODEL)


def _constants():
    r = np.arange(WIDTH)
    ph = (r[:, None] // HEAD_DIM == r[None, :] // HEAD_DIM).astype(np.float32) / HEAD_DIM
    s = np.arange(BLK)
    later = (s[:, None] > s[None, :]).astype(np.float32)
    half = np.concatenate([later, np.ones((BLK, BLK), np.float32)], axis=1)
    tri = np.concatenate([half, half], axis=0)
    return jnp.asarray(ph, BF16), jnp.asarray(tri, BF16)


def _layer(x, meta_tokens, norm1_g, w_in, sb_q_norm_g, sb_k_norm_g, dn_conv_w, dn_a_log, dn_dt_bias,
           dn_out_norm_g, w_sb_out, w_dn_out, w_o, norm2_g, peer_w_q, peer_sub_keys, peer_u, peer_v):
    batch, seq, d = x.shape
    lp = seq + BLK
    ph, tri = _constants()

    tail = jnp.concatenate([jnp.zeros((PAD, d), x.dtype), meta_tokens.astype(x.dtype)], axis=0)
    hp = jnp.concatenate([x, jnp.broadcast_to(tail[None], (batch, BLK, d))], axis=1).reshape(batch * lp, d)

    c_ba = 3 * WIDTH + 3 * WIDTH + WIDTH
    w_all = jnp.concatenate([w_in[:, :c_ba], w_in[:, c_ba + 2 * HEADS:], w_in[:, c_ba:c_ba + 2 * HEADS],
                             jnp.zeros((d, 128 - 2 * HEADS), w_in.dtype)], axis=1).astype(BF16)
    gq = (jnp.tile(sb_q_norm_g.astype(F32), HEADS) * (HEAD_DIM ** -0.5))[None]
    gk = jnp.tile(sb_k_norm_g.astype(F32), HEADS)[None]
    lane_pad = lambda t: jnp.pad(t.astype(F32), (HEADS, 128 - 2 * HEADS))[None]
    acoef = lane_pad(-jnp.exp(dn_a_log.astype(F32)))
    dtb = lane_pad(dn_dt_bias)

    sbqkv, dnqkv, z, gates, bg = _inproj(hp, norm1_g.astype(F32)[None], w_all, ph, gq, gk, acoef, dtb)
    o_sb = _sb_attn(sbqkv, tri, batch, lp)
    gn = jnp.tile(dn_out_norm_g.astype(F32), 2)[None]
    o_dn = _deltanet(dnqkv, z, bg, dn_conv_w.astype(F32), gn, batch, lp)
    h1 = _merge(o_sb, o_dn, gates, x.reshape(batch * seq, d), w_sb_out.astype(BF16),
                w_dn_out.astype(BF16), w_o.astype(BF16), batch, seq, lp)
    xn2, idx, gate = _route(h1, norm2_g.astype(F32)[None], peer_w_q.astype(BF16), peer_sub_keys)
    h2 = _peer(idx, gate, xn2, h1, peer_u, peer_v)
    return h2.reshape(batch, seq, d)


def kernel(x, meta_tokens, norm1_g, w_in, sb_q_norm_g, sb_k_norm_g, dn_conv_w, dn_a_log, dn_dt_bias,
           dn_out_norm_g, w_sb_out, w_dn_out, w_o, norm2_g, peer_w_q, peer_sub_keys, peer_u, peer_v):
    assert norm1_g.shape[0] == 1, "one layer"
    return _layer(x, meta_tokens, norm1_g[0], w_in[0], sb_q_norm_g[0], sb_k_norm_g[0], dn_conv_w[0],
                  dn_a_log[0], dn_dt_bias[0], dn_out_norm_g[0], w_sb_out[0], w_dn_out[0], w_o[0],
                  norm2_g[0], peer_w_q[0], peer_sub_keys[0], peer_u[0], peer_v[0])
```

```python
import functools

import numpy as np
import jax
import jax.numpy as jnp
from jax import lax
from jax.experimental import pallas as pl
from jax.experimental.pallas import tpu as pltpu

F32 = jnp.float32
BF16 = jnp.bfloat16

D_MODEL = 1024
N_META = 16
BLK = 128
PAD = BLK - N_META
HEADS = 8
HEAD_DIM = 64
WIDTH = HEADS * HEAD_DIM
CHUNK = 64
PEER_HEADS = 8
PEER_KEYS = 128
PEER_TOPK = 16
PEER_HK = PEER_HEADS * PEER_TOPK
EPS = 1e-6

C_SB = 0
C_DN = 3 * WIDTH
C_Z = C_DN + 3 * WIDTH
C_GATE = C_Z + WIDTH
C_BA = C_GATE + 2 * D_MODEL
C_END = C_BA + 128

ROWS_IN = 256
ROWS_OUT = 256
SB_STEP_HEADS = 4
DN_GROUP = 4
DMA_THREADS = 2
PEER_TB = 16
PEER_BUFS = 4
PEER_AHEAD = 2
PEER_ACT_SHARE = 64
PEER_UNROLL = 8
VMEM_LIMIT = 56 * 1024 * 1024


def _dot(a, b):
    return jnp.dot(a, b, preferred_element_type=F32)


def _dot_nt(a, b):
    return lax.dot_general(a, b, (((1,), (1,)), ((), ())), preferred_element_type=F32)


def _dot_tn(a, b):
    return lax.dot_general(a, b, (((0,), (0,)), ((), ())), preferred_element_type=F32)


def _split(a):
    hi = a.astype(BF16)
    lo = (a - hi.astype(F32)).astype(BF16)
    return hi, lo


def _dot_xr(a, b_exact):
    hi, lo = _split(a)
    return _dot(jnp.concatenate([hi, lo], axis=1), jnp.concatenate([b_exact, b_exact], axis=0))


def _dot_xl(a_exact, b):
    hi, lo = _split(b)
    return _dot(jnp.concatenate([a_exact, a_exact], axis=1), jnp.concatenate([hi, lo], axis=0))


def _dot3(a, b):
    ah, al = _split(a)
    bh, bl = _split(b)
    return _dot(jnp.concatenate([ah, ah, al], axis=1), jnp.concatenate([bh, bl, bh], axis=0))


def _sigmoid(x):
    return 1.0 / (1.0 + jnp.exp(-x))


def _softplus(x):
    return jnp.maximum(x, 0.0) + jnp.log1p(jnp.exp(-jnp.abs(x)))


def _iota(shape, dim):
    return lax.broadcasted_iota(jnp.int32, shape, dim)


def _inproj_kernel(x_ref, g1_ref, w_ref, ph_ref, gq_ref, gk_ref, acoef_ref, dtb_ref,
                   sb_ref, dn_ref, z_ref, gate_ref, bg_ref):
    x = x_ref[...]
    ms = jnp.mean(x * x, axis=-1, keepdims=True)
    xn = (x * lax.rsqrt(ms + EPS) * g1_ref[...]).astype(BF16)

    def proj(c0, c1):
        return _dot(xn, w_ref[:, c0:c1])

    def head_norm(t, g):
        msh = _dot((t * t).astype(BF16), ph_ref[...])
        return (t * lax.rsqrt(msh + EPS) * g).astype(BF16)

    sb_ref[:, 0:WIDTH] = head_norm(proj(C_SB, C_SB + WIDTH), gq_ref[...])
    sb_ref[:, WIDTH:2 * WIDTH] = head_norm(proj(C_SB + WIDTH, C_SB + 2 * WIDTH), gk_ref[...])
    sb_ref[:, 2 * WIDTH:3 * WIDTH] = proj(C_SB + 2 * WIDTH, C_SB + 3 * WIDTH).astype(BF16)
    dn_ref[...] = proj(C_DN, C_DN + 3 * WIDTH)
    z_ref[...] = proj(C_Z, C_Z + WIDTH)
    gate_ref[...] = _sigmoid(proj(C_GATE, C_GATE + 2 * D_MODEL)).astype(BF16)
    ba = proj(C_BA, C_END)
    lane = _iota(ba.shape, 1)
    bg_ref[...] = jnp.where(lane < HEADS, _sigmoid(ba), acoef_ref[...] * _softplus(ba + dtb_ref[...]))


def _inproj(hp, g1, w_all, ph, gq, gk, acoef, dtb):
    n = hp.shape[0]
    rb = ROWS_IN
    const = lambda shape: pl.BlockSpec(shape, lambda i: (0, 0))
    row = lambda c: pl.BlockSpec((rb, c), lambda i: (i, 0))
    return pl.pallas_call(
        _inproj_kernel,
        grid=(n // rb,),
        in_specs=[row(D_MODEL), const((1, D_MODEL)), const((D_MODEL, C_END)), const((WIDTH, WIDTH)),
                  const((1, WIDTH)), const((1, WIDTH)), const((1, 128)), const((1, 128))],
        out_specs=[row(3 * WIDTH), row(3 * WIDTH), row(WIDTH), row(2 * D_MODEL), row(128)],
        out_shape=[jax.ShapeDtypeStruct((n, 3 * WIDTH), BF16),
                   jax.ShapeDtypeStruct((n, 3 * WIDTH), F32),
                   jax.ShapeDtypeStruct((n, WIDTH), F32),
                   jax.ShapeDtypeStruct((n, 2 * D_MODEL), BF16),
                   jax.ShapeDtypeStruct((n, 128), F32)],
        compiler_params=pltpu.CompilerParams(dimension_semantics=("parallel",),
                                             vmem_limit_bytes=VMEM_LIMIT),
        name="inproj",
    )(hp, g1, w_all, ph, gq, gk, acoef, dtb)


def _sb_kernel(q_ref, k_ref, v_ref, tri_ref, o_ref, *, nblk):
    qi = pl.program_id(2)
    nh = SB_STEP_HEADS
    q = q_ref[0]
    lane = _iota((BLK, BLK), 1)
    row = _iota((BLK, BLK), 0)
    lane_head = _iota(q.shape, 1) >> 6
    qf = q.astype(F32)
    q2 = jnp.concatenate([jnp.where(lane_head == r, qf, 0.0) for r in range(nh)], axis=0).astype(BF16)
    tri = tri_ref[...]
    two = nh * BLK

    cols = lambda t, j: t[:, j * BLK:(j + 1) * BLK]

    def scores(k, nb, vis):
        z = _dot_nt(q2, k)
        neg_abs = lax.bitcast_convert_type(lax.bitcast_convert_type(z, jnp.uint32) | jnp.uint32(0x80000000), F32)
        soft = jnp.log(1.0 + jnp.exp(neg_abs))
        log_beta = jnp.minimum(z, 0.0) - soft
        log_om = log_beta - z
        if vis is not None:
            log_om = jnp.where(vis, log_om, 0.0)
        hi, lo = _split(log_om)
        stacked = jnp.concatenate(
            [jnp.concatenate([cols(hi, j), cols(lo, j)], axis=1) for j in range(nb)], axis=0)
        return log_beta, _dot(stacked, tri)

    def finish(log_beta, rt, nb, carry, acc, v, vis):
        later = carry
        parts = [None] * nb
        for j in reversed(range(nb)):
            rows = rt[j * two:(j + 1) * two]
            parts[j] = cols(log_beta, j) + rows[:, :BLK] + later
            later = later + rows[:, BLK:]
        a = jnp.exp(jnp.concatenate(parts, axis=1))
        if vis is not None:
            a = jnp.where(vis, a, 0.0)
        return later, acc + _dot(a.astype(BF16), v)

    def keys(ref, kphys, nb):
        return ref[0, pl.ds(pl.multiple_of(kphys * BLK, BLK), nb * BLK), :]

    def tile(kphys, nb, state):
        log_beta, rt = scores(keys(k_ref, kphys, nb), nb, None)
        return finish(log_beta, rt, nb, state[0], state[1], keys(v_ref, kphys, nb), None)

    stack2 = lambda m: jnp.concatenate([m] * nh, axis=0)
    vis_d, vis_0 = stack2(lane < row), stack2(lane >= PAD)
    lb2, rt2 = scores(jnp.concatenate([keys(k_ref, qi, 1), keys(k_ref, nblk - 1, 1)], axis=0), 2,
                      jnp.concatenate([vis_d, vis_0], axis=1))
    state = finish(cols(lb2, 0), rt2[:two], 1, jnp.zeros((two, BLK), F32), jnp.zeros((two, nh * HEAD_DIM), F32),
                   keys(v_ref, qi, 1), vis_d)

    rem = qi % 4
    four = (qi // 4) % 2
    state = lax.fori_loop(0, rem, lambda it, s: tile(qi - 1 - it, 1, s), state)
    state = lax.fori_loop(0, four, lambda it, s: tile(qi - rem - 4, 4, s), state)
    state = lax.fori_loop(0, qi // 8, lambda it, s: tile(qi - rem - 4 * four - 8 * (it + 1), 8, s), state)
    carry, acc = finish(cols(lb2, 1), rt2[two:], 1, state[0], state[1], keys(v_ref, nblk - 1, 1), vis_0)
    out = acc[:BLK]
    for r in range(1, nh):
        out = jnp.where(lane_head == r, acc[r * BLK:(r + 1) * BLK], out)
    o_ref[0] = out.astype(o_ref.dtype)


def _sb_attn(sbqkv, tri, batch, lp):
    nblk = lp // BLK
    nq = nblk - 1
    wid = SB_STEP_HEADS * HEAD_DIM
    pairs = WIDTH // wid
    x3 = sbqkv.reshape(batch, lp, 3 * WIDTH)
    return pl.pallas_call(
        functools.partial(_sb_kernel, nblk=nblk),
        grid=(batch, pairs, nq),
        in_specs=[pl.BlockSpec((1, BLK, wid), lambda b, p, i: (b, i, p)),
                  pl.BlockSpec((1, lp, wid), lambda b, p, i: (b, 0, pairs + p)),
                  pl.BlockSpec((1, lp, wid), lambda b, p, i: (b, 0, 2 * pairs + p)),
                  pl.BlockSpec((2 * BLK, 2 * BLK), lambda b, p, i: (0, 0))],
        out_specs=pl.BlockSpec((1, BLK, wid), lambda b, p, i: (b, i, p)),
        out_shape=jax.ShapeDtypeStruct((batch, nq * BLK, WIDTH), BF16),
        compiler_params=pltpu.CompilerParams(
            dimension_semantics=("parallel", "parallel", "arbitrary"), vmem_limit_bytes=VMEM_LIMIT),
        name="sb_attn",
    )(x3, x3, x3, tri)


def _dn_kernel(q_ref, k_ref, v_ref, z_ref, bg_ref, cwq_ref, cwk_ref, cwv_ref, gn_ref,
               o_ref, xbuf, s_ref):
    j = pl.program_id(2)

    @pl.when(j == 0)
    def _():
        xbuf[...] = jnp.zeros_like(xbuf)
        s_ref[...] = jnp.zeros_like(s_ref)

    def conv_silu(c, x_ref, cw_ref):
        xbuf[c, 8:8 + BLK, :] = x_ref[0]
        w = cw_ref[...]
        y = (w[3:4] * xbuf[c, 8:8 + BLK, :] + w[2:3] * xbuf[c, 7:7 + BLK, :]
             + w[1:2] * xbuf[c, 6:6 + BLK, :] + w[0:1] * xbuf[c, 5:5 + BLK, :])
        xbuf[c, 0:8, :] = xbuf[c, BLK:BLK + 8, :]
        return y * _sigmoid(y)

    lane = _iota((BLK, BLK), 1)
    row = _iota((BLK, BLK), 0)
    lane_head = lane >> 6
    row_head = row >> 6

    def pattern(cond):
        return jnp.where(cond, 1.0, 0.0).astype(BF16)

    same_head = pattern(lane_head == row_head)
    low_incl = (lane_head == row_head) & (lane <= row)
    low_strict = (lane_head == row_head) & (lane < row)
    cum_incl = pattern(low_incl)

    def l2(t):
        ss = _dot_xr(t * t, same_head)
        return t * lax.rsqrt(ss + EPS)

    q_all = conv_silu(0, q_ref, cwq_ref)
    k_all = conv_silu(1, k_ref, cwk_ref)
    v_all = conv_silu(2, v_ref, cwv_ref)
    shared = dict(j=j, bg=bg_ref[0], lane=lane, row=row, lane_head=lane_head, row_head=row_head,
                  pattern=pattern, same_head=same_head, low_incl=low_incl, low_strict=low_strict,
                  cum_incl=cum_incl, l2=l2)
    for pr in range(DN_GROUP):
        sl = slice(pr * BLK, (pr + 1) * BLK)
        _dn_pair(pr, sl, pl.program_id(1) * DN_GROUP + pr, q_all[:, sl], k_all[:, sl], v_all[:, sl],
                 z_ref, gn_ref, o_ref, s_ref, **shared)


def _dn_pair(pr, sl, hp, q, k, v, z_ref, gn_ref, o_ref, s_ref, *, j, bg, lane, row, lane_head, row_head,
             pattern, same_head, low_incl, low_strict, cum_incl, l2):
    qk_n = l2(jnp.concatenate([q, k], axis=0))
    q = qk_n[:BLK] * (HEAD_DIM ** -0.5)
    k = qk_n[BLK:]
    live = (row >= PAD) | (j > 0)
    picks = jnp.concatenate([pattern(row == 2 * hp + lane_head),
                             pattern(row == HEADS + 2 * hp + lane_head)], axis=1)
    beta_g = _dot_xr(bg, picks)
    beta = jnp.where(live, beta_g[:, :BLK], 0.0)
    g = jnp.where(live, beta_g[:, BLK:], 0.0)
    sums = _dot_xl(jnp.concatenate([cum_incl, same_head], axis=0), g)
    gc = sums[:BLK]
    g_last = sums[BLK:]
    eg = jnp.exp(gc)
    k_beta = k * beta
    v_beta = v * beta
    kbg = k_beta * eg
    q_decay = q * eg
    k_tail = k * jnp.exp(g_last - gc)

    eye = jnp.where(lane == row, 1.0, 0.0).astype(F32)
    gc_heads = _dot_xr(gc, jnp.concatenate([pattern(row == 0), pattern(row == HEAD_DIM)], axis=1))
    k16 = k.astype(BF16)
    rhs_uw = jnp.concatenate([v_beta, kbg], axis=1)
    u_parts, w_parts, a_parts = [], [], []
    for h in range(2):
        mine = lane_head == h
        gch = gc_heads[:, h * BLK:(h + 1) * BLK]
        decay = jnp.exp(jnp.where(low_incl, gch - gch.T, -jnp.inf))
        both = _dot_nt(jnp.concatenate([jnp.where(mine, k_beta, 0.0), jnp.where(mine, q, 0.0)],
                                       axis=0).astype(BF16), k16)
        m = jnp.where(low_strict, both[:BLK] * decay, 0.0)
        a_parts.append(jnp.where(low_incl, both[BLK:] * decay, 0.0))
        a_k = -m
        s_k = eye + a_k
        a_k = _dot3(a_k, a_k)
        for _ in range(4):
            prod = _dot3(a_k, jnp.concatenate([a_k, s_k], axis=1))
            a_k, s_k = prod[:, :BLK], s_k + prod[:, BLK:]
        t_inv = s_k + _dot3(a_k, s_k)
        uw = _dot3(t_inv, rhs_uw)
        u_parts.append(uw[:, :BLK])
        w_parts.append(uw[:, BLK:])
    head0 = lane_head == 0
    u = jnp.where(head0, u_parts[0], u_parts[1])
    w = jnp.where(head0, w_parts[0], w_parts[1])

    block_diag = lane_head == row_head
    state = s_ref[pr]
    vn_parts, inter_parts = [], []
    for c in range(2):
        r = slice(c * CHUNK, (c + 1) * CHUNK)
        ws_qs = _dot(jnp.concatenate([w[r], q_decay[r]], axis=0).astype(BF16), state.astype(BF16))
        vn = u[r] - ws_qs[:CHUNK]
        inter_parts.append(ws_qs[CHUNK:])
        gl = jnp.exp(g_last[c * CHUNK:c * CHUNK + 1, :])
        upd = _dot_tn(k_tail[r].astype(BF16), vn.astype(BF16))
        state = jnp.where(block_diag, state * gl + upd, 0.0)
        vn_parts.append(vn)
    s_ref[pr] = state
    vn = jnp.concatenate(vn_parts, axis=0)
    o = jnp.concatenate(inter_parts, axis=0) + _dot(
        jnp.concatenate(a_parts, axis=1).astype(BF16),
        jnp.concatenate([jnp.where(head0, vn, 0.0), jnp.where(head0, 0.0, vn)], axis=0).astype(BF16))

    ms = _dot_xr(o * o, same_head) * (1.0 / HEAD_DIM)
    zz = z_ref[0, :, sl]
    o_ref[0, :, sl] = (o * lax.rsqrt(ms + EPS) * gn_ref[...] * (zz * _sigmoid(zz))).astype(o_ref.dtype)


def _dn_kernel2(q_ref, k_ref, v_ref, z_ref, bg_ref, cwq_ref, cwk_ref, cwv_ref, gn_ref,
                o_ref, xbuf, s_ref):
    j = pl.program_id(2)
    npair = DN_GROUP
    pairs = range(npair)

    @pl.when(j == 0)
    def _():
        xbuf[...] = jnp.zeros_like(xbuf)
        s_ref[...] = jnp.zeros_like(s_ref)

    def conv_silu(c, x_ref, cw_ref):
        xbuf[c, 8:8 + BLK, :] = x_ref[0]
        w = cw_ref[...]
        y = (w[3:4] * xbuf[c, 8:8 + BLK, :] + w[2:3] * xbuf[c, 7:7 + BLK, :]
             + w[1:2] * xbuf[c, 6:6 + BLK, :] + w[0:1] * xbuf[c, 5:5 + BLK, :])
        xbuf[c, 0:8, :] = xbuf[c, BLK:BLK + 8, :]
        return y * _sigmoid(y)

    lane = _iota((BLK, BLK), 1)
    row = _iota((BLK, BLK), 0)
    lane_head = lane >> 6
    row_head = row >> 6
    pattern = lambda cond: jnp.where(cond, 1.0, 0.0).astype(BF16)
    same_head = pattern(lane_head == row_head)
    block_diag = lane_head == row_head
    low_incl = block_diag & (lane <= row)
    low_strict = block_diag & (lane < row)
    head0 = lane_head == 0
    eye = jnp.where(lane == row, 1.0, 0.0).astype(F32)
    cols = lambda t, p: t[:, p * BLK:(p + 1) * BLK]
    rows = lambda t, p: t[p * BLK:(p + 1) * BLK]
    cat0 = lambda xs: jnp.concatenate(xs, axis=0)
    cat1 = lambda xs: jnp.concatenate(xs, axis=1)

    q_all = conv_silu(0, q_ref, cwq_ref)
    k_all = conv_silu(1, k_ref, cwk_ref)
    v_all = conv_silu(2, v_ref, cwv_ref)

    qk = cat0([cols(q_all, p) for p in pairs] + [cols(k_all, p) for p in pairs])
    qk = qk * lax.rsqrt(_dot_xr(qk * qk, same_head) + EPS)
    q = [rows(qk, p) * (HEAD_DIM ** -0.5) for p in pairs]
    k = [rows(qk, npair + p) for p in pairs]
    v = [cols(v_all, p) for p in pairs]

    live = (row >= PAD) | (j > 0)
    hp = [pl.program_id(1) * npair + p for p in pairs]
    picks = cat1([pattern(row == 2 * hp[p] + lane_head) for p in pairs]
                 + [pattern(row == HEADS + 2 * hp[p] + lane_head) for p in pairs])
    beta_g = _dot_xr(bg_ref[0], picks)
    beta = [jnp.where(live, cols(beta_g, p), 0.0) for p in pairs]
    g = [jnp.where(live, cols(beta_g, npair + p), 0.0) for p in pairs]
    sums = _dot_xl(cat0([pattern(low_incl), same_head]), cat1(g))
    gc = [cols(sums[:BLK], p) for p in pairs]
    g_last = [cols(sums[BLK:], p) for p in pairs]
    gc_heads = _dot_xr(cat0(gc), cat1([pattern(row == 0), pattern(row == HEAD_DIM)]))

    k_beta = [k[p] * beta[p] for p in pairs]
    eg = [jnp.exp(gc[p]) for p in pairs]
    rhs_uw = [cat1([v[p] * beta[p], k_beta[p] * eg[p]]) for p in pairs]
    q_decay = [q[p] * eg[p] for p in pairs]
    k_tail = [k[p] * jnp.exp(g_last[p] - gc[p]) for p in pairs]

    chains = [(p, h) for p in pairs for h in range(2)]
    both = [_dot_nt(cat0([jnp.where(lane_head == h, t, 0.0) for h in range(2) for t in (k_beta[p], q[p])]
                         ).astype(BF16), k[p].astype(BF16)) for p in pairs]
    a_k, a_intra = {}, {}
    for p, h in chains:
        gch = rows(gc_heads, p)[:, h * BLK:(h + 1) * BLK]
        decay = jnp.exp(jnp.where(low_incl, gch - gch.T, -jnp.inf))
        a_k[p, h] = -jnp.where(low_strict, rows(both[p], 2 * h) * decay, 0.0)
        a_intra[p, h] = jnp.where(low_incl, rows(both[p], 2 * h + 1) * decay, 0.0)

    s_k = {c: eye + a_k[c] for c in chains}
    a_k = {c: _dot3(a_k[c], a_k[c]) for c in chains}
    for _ in range(4):
        prod = {c: _dot3(a_k[c], cat1([a_k[c], s_k[c]])) for c in chains}
        a_k = {c: prod[c][:, :BLK] for c in chains}
        s_k = {c: s_k[c] + prod[c][:, BLK:] for c in chains}
    last = {c: _dot3(a_k[c], s_k[c]) for c in chains}
    uw = {c: _dot3(s_k[c] + last[c], rhs_uw[c[0]]) for c in chains}
    u = [jnp.where(head0, uw[p, 0][:, :BLK], uw[p, 1][:, :BLK]) for p in pairs]
    w = [jnp.where(head0, uw[p, 0][:, BLK:], uw[p, 1][:, BLK:]) for p in pairs]

    state = [s_ref[p] for p in pairs]
    vn = [[None, None] for _ in pairs]
    inter = [[None, None] for _ in pairs]
    for c in range(2):
        r = slice(c * CHUNK, (c + 1) * CHUNK)
        ws_qs = [_dot(cat0([w[p][r], q_decay[p][r]]).astype(BF16), state[p].astype(BF16)) for p in pairs]
        for p in pairs:
            vn[p][c] = u[p][r] - ws_qs[p][:CHUNK]
            inter[p][c] = ws_qs[p][CHUNK:]
        upd = [_dot_tn(k_tail[p][r].astype(BF16), vn[p][c].astype(BF16)) for p in pairs]
        for p in pairs:
            gl = jnp.exp(g_last[p][c * CHUNK:c * CHUNK + 1, :])
            state[p] = jnp.where(block_diag, state[p] * gl + upd[p], 0.0)
    o = []
    for p in pairs:
        s_ref[p] = state[p]
        vnp = cat0(vn[p])
        o.append(cat0(inter[p]) + _dot(
            cat1([a_intra[p, 0], a_intra[p, 1]]).astype(BF16),
            cat0([jnp.where(head0, vnp, 0.0), jnp.where(head0, 0.0, vnp)]).astype(BF16)))

    o = cat0(o)
    o = o * lax.rsqrt(_dot_xr(o * o, same_head) * (1.0 / HEAD_DIM) + EPS)
    for p in pairs:
        sl = slice(p * BLK, (p + 1) * BLK)
        zz = z_ref[0, :, sl]
        o_ref[0, :, sl] = (rows(o, p) * gn_ref[...] * (zz * _sigmoid(zz))).astype(o_ref.dtype)


def _deltanet(dnqkv, z, bg, conv_w, gn, batch, lp):
    nblk = lp // BLK
    grp = DN_GROUP * BLK
    pairs = WIDTH // grp
    phys = lambda j: (j + nblk - 1) % nblk
    x3 = dnqkv.reshape(batch, lp, 3 * WIDTH)
    z3 = z.reshape(batch, lp, WIDTH)
    bg3 = bg.reshape(batch, lp, 128)
    col = lambda off: pl.BlockSpec((1, BLK, grp), lambda b, p, j: (b, phys(j), off * pairs + p))
    cw = lambda off: pl.BlockSpec((4, grp), lambda b, p, j: (0, off * pairs + p))
    return pl.pallas_call(
        _dn_kernel2,
        grid=(batch, pairs, nblk),
        in_specs=[col(0), col(1), col(2),
                  pl.BlockSpec((1, BLK, grp), lambda b, p, j: (b, phys(j), p)),
                  pl.BlockSpec((1, BLK, 128), lambda b, p, j: (b, phys(j), 0)),
                  cw(0), cw(1), cw(2),
                  pl.BlockSpec((1, BLK), lambda b, p, j: (0, 0))],
        out_specs=pl.BlockSpec((1, BLK, grp), lambda b, p, j: (b, jnp.maximum(j - 1, 0), p)),
        out_shape=jax.ShapeDtypeStruct((batch, (nblk - 1) * BLK, WIDTH), BF16),
        scratch_shapes=[pltpu.VMEM((3, BLK + 8, grp), F32), pltpu.VMEM((DN_GROUP, BLK, BLK), F32)],
        compiler_params=pltpu.CompilerParams(
            dimension_semantics=("parallel", "parallel", "arbitrary"), vmem_limit_bytes=VMEM_LIMIT),
        name="deltanet",
    )(x3, x3, x3, z3, bg3, conv_w, conv_w, conv_w, gn)


def _merge_kernel(osb_ref, odn_ref, gate_ref, x_ref, wsb_ref, wdn_ref, wo_ref, h_ref):
    gate = gate_ref[0].astype(F32)
    mix = (gate[:, :D_MODEL] * _dot(osb_ref[...], wsb_ref[...])
           + gate[:, D_MODEL:] * _dot(odn_ref[...], wdn_ref[...]))
    h_ref[...] = x_ref[...] + _dot(mix.astype(BF16), wo_ref[...])


def _merge(o_sb, o_dn, gates, x2, wsb, wdn, wo, batch, seq, lp):
    n = batch * seq
    rb = ROWS_OUT
    per = seq // rb
    row = lambda c: pl.BlockSpec((rb, c), lambda i: (i, 0))
    const = lambda r, c: pl.BlockSpec((r, c), lambda i: (0, 0))
    return pl.pallas_call(
        _merge_kernel,
        grid=(n // rb,),
        in_specs=[row(WIDTH), row(WIDTH),
                  pl.BlockSpec((1, rb, 2 * D_MODEL), lambda i: (i // per, i % per, 0)),
                  row(D_MODEL), const(WIDTH, D_MODEL), const(WIDTH, D_MODEL), const(D_MODEL, D_MODEL)],
        out_specs=row(D_MODEL),
        out_shape=jax.ShapeDtypeStruct((n, D_MODEL), F32),
        compiler_params=pltpu.CompilerParams(dimension_semantics=("parallel",),
                                             vmem_limit_bytes=VMEM_LIMIT),
        name="merge",
    )(o_sb.reshape(n, WIDTH), o_dn.reshape(n, WIDTH), gates.reshape(batch, lp, 2 * D_MODEL),
      x2, wsb, wdn, wo)


def _top_rows(s, k, payload=None):
    nrow = s.shape[0]
    rid = _iota(s.shape, 0)
    vals, picks = [], []
    for _ in range(k):
        m = jnp.max(s, axis=0, keepdims=True)
        am = jnp.min(jnp.where(s == m, rid, nrow), axis=0, keepdims=True)
        hit = rid == am
        vals.append(m)
        picks.append(am if payload is None
                     else jnp.max(jnp.where(hit, payload, -1), axis=0, keepdims=True))
        s = jnp.where(hit, -jnp.inf, s)
    return jnp.concatenate(vals, axis=0), jnp.concatenate(picks, axis=0)


def _route_kernel(h_ref, g2_ref, wq_ref, keys_ref, xn_ref, idx_ref, gate_ref):
    x = h_ref[...]
    ms = jnp.mean(x * x, axis=-1, keepdims=True)
    xn = x * lax.rsqrt(ms + EPS) * g2_ref[...]
    xn_ref[...] = xn
    q = _dot(xn.astype(BF16), wq_ref[...])
    keys = (keys_ref[0].astype(BF16), keys_ref[1].astype(BF16))
    idx_rows, gate_rows = [], []
    for h in range(PEER_HEADS):
        tops = []
        for p in range(2):
            c0 = (2 * h + p) * PEER_KEYS
            s = _dot_nt(keys[p], q[:, c0:c0 + PEER_KEYS].astype(BF16))
            tops.append(_top_rows(s, PEER_TOPK))
        (s1, i1), (s2, i2) = tops
        brow = _iota((8, s1.shape[1]), 0)
        cs, ci = [s1[0:1] + s2], [i1[0:1] * PEER_KEYS + i2]
        for a in range(1, 8):
            keep = PEER_TOPK // (a + 1)
            sa = s1[a:a + 1] + s2[0:8]
            cs.append(sa if keep >= 8 else jnp.where(brow < keep, sa, -jnp.inf))
            ci.append(i1[a:a + 1] * PEER_KEYS + i2[0:8])
        cs.append(s1[8:16] + s2[0:1])
        ci.append(i1[8:16] * PEER_KEYS + i2[0:1])
        top_s, top_i = _top_rows(jnp.concatenate(cs, axis=0), PEER_TOPK, jnp.concatenate(ci, axis=0))
        e = jnp.exp(top_s - top_s[0:1])
        idx_rows.append(top_i)
        gate_rows.append(e / jnp.sum(e, axis=0, keepdims=True))
    idx_ref[...] = jnp.concatenate(idx_rows, axis=0).T
    gate_ref[...] = jnp.concatenate(gate_rows, axis=0).T


def _route(h1, g2, wq, keys):
    n = h1.shape[0]
    rb = ROWS_OUT
    return pl.pallas_call(
        _route_kernel,
        grid=(n // rb,),
        in_specs=[pl.BlockSpec((rb, D_MODEL), lambda i: (i, 0)),
                  pl.BlockSpec((1, D_MODEL), lambda i: (0, 0)),
                  pl.BlockSpec(wq.shape, lambda i: (0, 0)),
                  pl.BlockSpec(keys.shape, lambda i: (0, 0, 0))],
        out_specs=[pl.BlockSpec((rb, D_MODEL), lambda i: (i, 0)),
                   pl.BlockSpec((rb, PEER_HK), lambda i: (i, 0)),
                   pl.BlockSpec((rb, PEER_HK), lambda i: (i, 0))],
        out_shape=[jax.ShapeDtypeStruct((n, D_MODEL), F32),
                   jax.ShapeDtypeStruct((n, PEER_HK), jnp.int32),
                   jax.ShapeDtypeStruct((n, PEER_HK), F32)],
        compiler_params=pltpu.CompilerParams(dimension_semantics=("parallel",),
                                             vmem_limit_bytes=VMEM_LIMIT),
        name="route",
    )(h1, g2, wq, keys)


def _peer_kernel(idx_ref, idx_next_ref, gate_ref, xn_ref, h_ref, sel_ref, rep_ref, tab_hbm, out_ref,
                 *scratch):
    bufs, act_ref, sem = scratch[:-2], scratch[-2], scratch[-1]
    i = pl.program_id(0)
    n = pl.num_programs(0)
    tb = PEER_TB
    rows = tb * PEER_HK
    sub = D_MODEL // 128
    unroll = PEER_UNROLL
    ones = jnp.ones((sub, 2 * 128), BF16)
    lanes = lambda t, k: t[:, k * 128:(k + 1) * 128]
    trow = _iota((tb, unroll * 128), 0)
    lane_tok = _iota((tb, unroll * 128), 1) >> 7

    def issue_token(src_idx, row0, t, buf, sl, e0=0, e1=PEER_HK):
        for e in range(e0, e1):
            pltpu.make_async_copy(tab_hbm.at[src_idx[row0 + t, e]], buf.at[t, e], sem.at[sl]).start(
                priority=e % DMA_THREADS)

    def drain(buf, sl):
        for t in range(tb):
            pltpu.make_async_copy(tab_hbm.at[pl.ds(0, PEER_HK)], buf.at[t], sem.at[sl]).wait()

    def phase(buf, sl, tok0, nxt_buf, nxt_sl, nxt_idx, nxt_row0):
        drain(buf, sl)

        def tile_rows(t):
            return buf[t]

        def act_body(g, carry):
            prods = []
            for k in range(unroll):
                t = g * unroll + k
                issue_token(nxt_idx, nxt_row0, t, nxt_buf, nxt_sl, 0, PEER_ACT_SHARE)
                u = lax.bitcast_convert_type(tile_rows(t) << 16, F32)
                prods.append((u * xn_ref[tok0 + t][None]).reshape(PEER_HK * sub, 128).astype(BF16))
            part = _dot(sel_ref[...], jnp.concatenate(prods, axis=1))
            hi, lo = _split(jnp.concatenate([lanes(part, k) for k in range(unroll)], axis=0))
            sums = _dot_nt(ones, jnp.concatenate([hi, lo], axis=1))
            for k in range(unroll):
                act_ref[pl.ds(g * unroll + k, 1), :] = lanes(sums, k)[0:1]
            return carry

        lax.fori_loop(0, tb // unroll, act_body, 0)

        act = act_ref[...]
        wgt = (0.5 * act * (1.0 + lax.erf(act * (2.0 ** -0.5))) * gate_ref[tok0:tok0 + tb, :]).astype(BF16)

        def mix_body(g, carry):
            pick = jnp.where(trow == g * unroll + lane_tok, 1.0, 0.0).astype(BF16)
            col = _dot_tn(wgt, pick)
            wide = _dot(rep_ref[...], col.astype(BF16))
            for k in range(unroll):
                t = g * unroll + k
                issue_token(nxt_idx, nxt_row0, t, nxt_buf, nxt_sl, PEER_ACT_SHARE, PEER_HK)
                v = lax.bitcast_convert_type(tile_rows(t) & jnp.uint32(0xFFFF0000), F32)
                out_ref[tok0 + t] = h_ref[tok0 + t] + jnp.sum(
                    lanes(wide, k).reshape(PEER_HK, sub, 128) * v, axis=0)
            return carry

        lax.fori_loop(0, tb // unroll, mix_body, 0)

    nbuf = len(bufs)
    ahead = PEER_AHEAD

    @pl.when(i == 0)
    def _():
        def body(t, carry):
            for p in range(ahead):
                issue_token(idx_ref, p * tb, t, bufs[p], p)
            return carry
        lax.fori_loop(0, tb, body, 0)

    for p in range(nbuf):
        q = p + ahead
        src, row0 = (idx_ref, q * tb) if q < nbuf else (idx_next_ref, (q - nbuf) * tb)
        phase(bufs[p], p, p * tb, bufs[q % nbuf], q % nbuf, src, row0)

    @pl.when(i == n - 1)
    def _():
        for p in range(ahead):
            drain(bufs[p], p)


def _pack_tables(u_tab, v_tab):
    half = lambda t: lax.bitcast_convert_type(t.astype(BF16), jnp.uint16).astype(jnp.uint32)
    packed = half(u_tab) | (half(v_tab) << 16)
    return packed.reshape(u_tab.shape[0], D_MODEL // 128, 128)


def _peer(idx, gate, xn, h1, u_tab, v_tab):
    n = h1.shape[0]
    tb = PEER_BUFS * PEER_TB
    steps = n // tb
    rows = PEER_TB * PEER_HK
    sub = D_MODEL // 128
    tiles = lambda t: t.reshape(n, sub, 128)
    e = np.arange(PEER_HK)
    sel = jnp.asarray(e[:, None] == (np.arange(PEER_HK * sub)[None, :] // sub), BF16)
    smem = lambda f: pl.BlockSpec((tb, PEER_HK), f, memory_space=pltpu.SMEM)
    tok = pl.BlockSpec((tb, sub, 128), lambda i: (i, 0, 0))
    out = pl.pallas_call(
        _peer_kernel,
        grid=(steps,),
        in_specs=[smem(lambda i: (i, 0)),
                  smem(lambda i: (jnp.minimum(i + 1, steps - 1), 0)),
                  pl.BlockSpec((tb, PEER_HK), lambda i: (i, 0)),
                  tok, tok,
                  pl.BlockSpec((PEER_HK, PEER_HK * sub), lambda i: (0, 0)),
                  pl.BlockSpec((PEER_HK * sub, PEER_HK), lambda i: (0, 0)),
                  pl.BlockSpec(memory_space=pl.ANY)],
        out_specs=tok,
        out_shape=jax.ShapeDtypeStruct((n, sub, 128), F32),
        scratch_shapes=[pltpu.VMEM((PEER_TB, PEER_HK, sub, 128), jnp.uint32) for _ in range(PEER_BUFS)]
        + [pltpu.VMEM((PEER_TB, PEER_HK), F32), pltpu.SemaphoreType.DMA((PEER_BUFS,))],
        compiler_params=pltpu.CompilerParams(dimension_semantics=("arbitrary",),
                                             vmem_limit_bytes=VMEM_LIMIT),
        name="peer",
    )(idx, idx, gate, tiles(xn), tiles(h1), sel, sel.T, _pack_tables(u_tab, v_tab))
    return out.reshape(n, D_MODEL)


def _constants():
    r = np.arange(WIDTH)
    ph = (r[:, None] // HEAD_DIM == r[None, :] // HEAD_DIM).astype(np.float32) / HEAD_DIM
    s = np.arange(BLK)
    later = (s[:, None] > s[None, :]).astype(np.float32)
    half = np.concatenate([later, np.ones((BLK, BLK), np.float32)], axis=1)
    tri = np.concatenate([half, half], axis=0)
    return jnp.asarray(ph, BF16), jnp.asarray(tri, BF16)


def _layer(x, meta_tokens, norm1_g, w_in, sb_q_norm_g, sb_k_norm_g, dn_conv_w, dn_a_log, dn_dt_bias,
           dn_out_norm_g, w_sb_out, w_dn_out, w_o, norm2_g, peer_w_q, peer_sub_keys, peer_u, peer_v):
    batch, seq, d = x.shape
    lp = seq + BLK
    ph, tri = _constants()

    tail = jnp.concatenate([jnp.zeros((PAD, d), x.dtype), meta_tokens.astype(x.dtype)], axis=0)
    hp = jnp.concatenate([x, jnp.broadcast_to(tail[None], (batch, BLK, d))], axis=1).reshape(batch * lp, d)

    c_ba = 3 * WIDTH + 3 * WIDTH + WIDTH
    w_all = jnp.concatenate([w_in[:, :c_ba], w_in[:, c_ba + 2 * HEADS:], w_in[:, c_ba:c_ba + 2 * HEADS],
                             jnp.zeros((d, 128 - 2 * HEADS), w_in.dtype)], axis=1).astype(BF16)
    gq = (jnp.tile(sb_q_norm_g.astype(F32), HEADS) * (HEAD_DIM ** -0.5))[None]
    gk = jnp.tile(sb_k_norm_g.astype(F32), HEADS)[None]
    lane_pad = lambda t: jnp.pad(t.astype(F32), (HEADS, 128 - 2 * HEADS))[None]
    acoef = lane_pad(-jnp.exp(dn_a_log.astype(F32)))
    dtb = lane_pad(dn_dt_bias)

    sbqkv, dnqkv, z, gates, bg = _inproj(hp, norm1_g.astype(F32)[None], w_all, ph, gq, gk, acoef, dtb)
    o_sb = _sb_attn(sbqkv, tri, batch, lp)
    gn = jnp.tile(dn_out_norm_g.astype(F32), 2)[None]
    o_dn = _deltanet(dnqkv, z, bg, dn_conv_w.astype(F32), gn, batch, lp)
    h1 = _merge(o_sb, o_dn, gates, x.reshape(batch * seq, d), w_sb_out.astype(BF16),
                w_dn_out.astype(BF16), w_o.astype(BF16), batch, seq, lp)
    xn2, idx, gate = _route(h1, norm2_g.astype(F32)[None], peer_w_q.astype(BF16), peer_sub_keys)
    h2 = _peer(idx, gate, xn2, h1, peer_u, peer_v)
    return h2.reshape(batch, seq, d)


def kernel(x, meta_tokens, norm1_g, w_in, sb_q_norm_g, sb_k_norm_g, dn_conv_w, dn_a_log, dn_dt_bias,
           dn_out_norm_g, w_sb_out, w_dn_out, w_o, norm2_g, peer_w_q, peer_sub_keys, peer_u, peer_v):
    assert norm1_g.shape[0] == 1, "one layer"
    return _layer(x, meta_tokens, norm1_g[0], w_in[0], sb_q_norm_g[0], sb_k_norm_g[0], dn_conv_w[0],
                  dn_a_log[0], dn_dt_bias[0], dn_out_norm_g[0], w_sb_out[0], w_dn_out[0], w_o[0],
                  norm2_g[0], peer_w_q[0], peer_sub_keys[0], peer_u[0], peer_v[0])
```

```python
import functools

import numpy as np
import jax
import jax.numpy as jnp
from jax import lax
from jax.experimental import pallas as pl
from jax.experimental.pallas import tpu as pltpu

F32 = jnp.float32
BF16 = jnp.bfloat16

D_MODEL = 1024
N_META = 16
BLK = 128
PAD = BLK - N_META
HEADS = 8
HEAD_DIM = 64
WIDTH = HEADS * HEAD_DIM
CHUNK = 64
PEER_HEADS = 8
PEER_KEYS = 128
PEER_TOPK = 16
PEER_HK = PEER_HEADS * PEER_TOPK
EPS = 1e-6

C_SB = 0
C_DN = 3 * WIDTH
C_Z = C_DN + 3 * WIDTH
C_GATE = C_Z + WIDTH
C_BA = C_GATE + 2 * D_MODEL
C_END = C_BA + 128

ROWS_IN = 256
ROWS_OUT = 256
SB_STEP_HEADS = 4
DN_GROUP = 4
DMA_THREADS = 2
PEER_TB = 16
PEER_BUFS = 4
PEER_AHEAD = 2
PEER_ACT_SHARE = 64
PEER_UNROLL = 16
VMEM_LIMIT = 56 * 1024 * 1024


def _dot(a, b):
    return jnp.dot(a, b, preferred_element_type=F32)


def _dot_nt(a, b):
    return lax.dot_general(a, b, (((1,), (1,)), ((), ())), preferred_element_type=F32)


def _dot_tn(a, b):
    return lax.dot_general(a, b, (((0,), (0,)), ((), ())), preferred_element_type=F32)


def _split(a):
    hi = a.astype(BF16)
    lo = (a - hi.astype(F32)).astype(BF16)
    return hi, lo


def _dot_xr(a, b_exact):
    hi, lo = _split(a)
    return _dot(jnp.concatenate([hi, lo], axis=1), jnp.concatenate([b_exact, b_exact], axis=0))


def _dot_xl(a_exact, b):
    hi, lo = _split(b)
    return _dot(jnp.concatenate([a_exact, a_exact], axis=1), jnp.concatenate([hi, lo], axis=0))


def _dot3(a, b):
    ah, al = _split(a)
    bh, bl = _split(b)
    return _dot(jnp.concatenate([ah, ah, al], axis=1), jnp.concatenate([bh, bl, bh], axis=0))


def _sigmoid(x):
    return 1.0 / (1.0 + jnp.exp(-x))


def _softplus(x):
    return jnp.maximum(x, 0.0) + jnp.log1p(jnp.exp(-jnp.abs(x)))


def _iota(shape, dim):
    return lax.broadcasted_iota(jnp.int32, shape, dim)


def _inproj_kernel(x_ref, g1_ref, w_ref, ph_ref, gq_ref, gk_ref, acoef_ref, dtb_ref,
                   sb_ref, dn_ref, z_ref, gate_ref, bg_ref):
    x = x_ref[...]
    ms = jnp.mean(x * x, axis=-1, keepdims=True)
    xn = (x * lax.rsqrt(ms + EPS) * g1_ref[...]).astype(BF16)

    def proj(c0, c1):
        return _dot(xn, w_ref[:, c0:c1])

    def head_norm(t, g):
        msh = _dot((t * t).astype(BF16), ph_ref[...])
        return (t * lax.rsqrt(msh + EPS) * g).astype(BF16)

    sb_ref[:, 0:WIDTH] = head_norm(proj(C_SB, C_SB + WIDTH), gq_ref[...])
    sb_ref[:, WIDTH:2 * WIDTH] = head_norm(proj(C_SB + WIDTH, C_SB + 2 * WIDTH), gk_ref[...])
    sb_ref[:, 2 * WIDTH:3 * WIDTH] = proj(C_SB + 2 * WIDTH, C_SB + 3 * WIDTH).astype(BF16)
    dn_ref[...] = proj(C_DN, C_DN + 3 * WIDTH)
    z_ref[...] = proj(C_Z, C_Z + WIDTH)
    gate_ref[...] = _sigmoid(proj(C_GATE, C_GATE + 2 * D_MODEL)).astype(BF16)
    ba = proj(C_BA, C_END)
    lane = _iota(ba.shape, 1)
    bg_ref[...] = jnp.where(lane < HEADS, _sigmoid(ba), acoef_ref[...] * _softplus(ba + dtb_ref[...]))


def _inproj(hp, g1, w_all, ph, gq, gk, acoef, dtb):
    n = hp.shape[0]
    rb = ROWS_IN
    const = lambda shape: pl.BlockSpec(shape, lambda i: (0, 0))
    row = lambda c: pl.BlockSpec((rb, c), lambda i: (i, 0))
    return pl.pallas_call(
        _inproj_kernel,
        grid=(n // rb,),
        in_specs=[row(D_MODEL), const((1, D_MODEL)), const((D_MODEL, C_END)), const((WIDTH, WIDTH)),
                  const((1, WIDTH)), const((1, WIDTH)), const((1, 128)), const((1, 128))],
        out_specs=[row(3 * WIDTH), row(3 * WIDTH), row(WIDTH), row(2 * D_MODEL), row(128)],
        out_shape=[jax.ShapeDtypeStruct((n, 3 * WIDTH), BF16),
                   jax.ShapeDtypeStruct((n, 3 * WIDTH), F32),
                   jax.ShapeDtypeStruct((n, WIDTH), F32),
                   jax.ShapeDtypeStruct((n, 2 * D_MODEL), BF16),
                   jax.ShapeDtypeStruct((n, 128), F32)],
        compiler_params=pltpu.CompilerParams(dimension_semantics=("parallel",),
                                             vmem_limit_bytes=VMEM_LIMIT),
        name="inproj",
    )(hp, g1, w_all, ph, gq, gk, acoef, dtb)


def _sb_kernel(q_ref, k_ref, v_ref, tri_ref, o_ref, *, nblk):
    qi = pl.program_id(2)
    nh = SB_STEP_HEADS
    q = q_ref[0]
    lane = _iota((BLK, BLK), 1)
    row = _iota((BLK, BLK), 0)
    lane_head = _iota(q.shape, 1) >> 6
    qf = q.astype(F32)
    q2 = jnp.concatenate([jnp.where(lane_head == r, qf, 0.0) for r in range(nh)], axis=0).astype(BF16)
    tri = tri_ref[...]
    two = nh * BLK

    cols = lambda t, j: t[:, j * BLK:(j + 1) * BLK]

    def scores(k, nb, vis):
        z = _dot_nt(q2, k)
        neg_abs = lax.bitcast_convert_type(lax.bitcast_convert_type(z, jnp.uint32) | jnp.uint32(0x80000000), F32)
        soft = jnp.log(1.0 + jnp.exp(neg_abs))
        log_beta = jnp.minimum(z, 0.0) - soft
        log_om = log_beta - z
        if vis is not None:
            log_om = jnp.where(vis, log_om, 0.0)
        hi, lo = _split(log_om)
        stacked = jnp.concatenate(
            [jnp.concatenate([cols(hi, j), cols(lo, j)], axis=1) for j in range(nb)], axis=0)
        return log_beta, _dot(stacked, tri)

    def finish(log_beta, rt, nb, carry, acc, v, vis):
        later = carry
        parts = [None] * nb
        for j in reversed(range(nb)):
            rows = rt[j * two:(j + 1) * two]
            parts[j] = cols(log_beta, j) + rows[:, :BLK] + later
            later = later + rows[:, BLK:]
        a = jnp.exp(jnp.concatenate(parts, axis=1))
        if vis is not None:
            a = jnp.where(vis, a, 0.0)
        return later, acc + _dot(a.astype(BF16), v)

    def keys(ref, kphys, nb):
        return ref[0, pl.ds(pl.multiple_of(kphys * BLK, BLK), nb * BLK), :]

    def tile(kphys, nb, state):
        log_beta, rt = scores(keys(k_ref, kphys, nb), nb, None)
        return finish(log_beta, rt, nb, state[0], state[1], keys(v_ref, kphys, nb), None)

    stack2 = lambda m: jnp.concatenate([m] * nh, axis=0)
    vis_d, vis_0 = stack2(lane < row), stack2(lane >= PAD)
    lb2, rt2 = scores(jnp.concatenate([keys(k_ref, qi, 1), keys(k_ref, nblk - 1, 1)], axis=0), 2,
                      jnp.concatenate([vis_d, vis_0], axis=1))
    state = finish(cols(lb2, 0), rt2[:two], 1, jnp.zeros((two, BLK), F32), jnp.zeros((two, nh * HEAD_DIM), F32),
                   keys(v_ref, qi, 1), vis_d)

    rem = qi % 4
    four = (qi // 4) % 2
    state = lax.fori_loop(0, rem, lambda it, s: tile(qi - 1 - it, 1, s), state)
    state = lax.fori_loop(0, four, lambda it, s: tile(qi - rem - 4, 4, s), state)
    state = lax.fori_loop(0, qi // 8, lambda it, s: tile(qi - rem - 4 * four - 8 * (it + 1), 8, s), state)
    carry, acc = finish(cols(lb2, 1), rt2[two:], 1, state[0], state[1], keys(v_ref, nblk - 1, 1), vis_0)
    out = acc[:BLK]
    for r in range(1, nh):
        out = jnp.where(lane_head == r, acc[r * BLK:(r + 1) * BLK], out)
    o_ref[0] = out.astype(o_ref.dtype)


def _sb_attn(sbqkv, tri, batch, lp):
    nblk = lp // BLK
    nq = nblk - 1
    wid = SB_STEP_HEADS * HEAD_DIM
    pairs = WIDTH // wid
    x3 = sbqkv.reshape(batch, lp, 3 * WIDTH)
    return pl.pallas_call(
        functools.partial(_sb_kernel, nblk=nblk),
        grid=(batch, pairs, nq),
        in_specs=[pl.BlockSpec((1, BLK, wid), lambda b, p, i: (b, i, p)),
                  pl.BlockSpec((1, lp, wid), lambda b, p, i: (b, 0, pairs + p)),
                  pl.BlockSpec((1, lp, wid), lambda b, p, i: (b, 0, 2 * pairs + p)),
                  pl.BlockSpec((2 * BLK, 2 * BLK), lambda b, p, i: (0, 0))],
        out_specs=pl.BlockSpec((1, BLK, wid), lambda b, p, i: (b, i, p)),
        out_shape=jax.ShapeDtypeStruct((batch, nq * BLK, WIDTH), BF16),
        compiler_params=pltpu.CompilerParams(
            dimension_semantics=("parallel", "parallel", "arbitrary"), vmem_limit_bytes=VMEM_LIMIT),
        name="sb_attn",
    )(x3, x3, x3, tri)


def _dn_kernel(q_ref, k_ref, v_ref, z_ref, bg_ref, cwq_ref, cwk_ref, cwv_ref, gn_ref,
               o_ref, xbuf, s_ref):
    j = pl.program_id(2)
    npair = DN_GROUP
    pairs = range(npair)

    @pl.when(j == 0)
    def _():
        xbuf[...] = jnp.zeros_like(xbuf)
        s_ref[...] = jnp.zeros_like(s_ref)

    def conv_silu(c, x_ref, cw_ref):
        xbuf[c, 8:8 + BLK, :] = x_ref[0]
        w = cw_ref[...]
        y = (w[3:4] * xbuf[c, 8:8 + BLK, :] + w[2:3] * xbuf[c, 7:7 + BLK, :]
             + w[1:2] * xbuf[c, 6:6 + BLK, :] + w[0:1] * xbuf[c, 5:5 + BLK, :])
        xbuf[c, 0:8, :] = xbuf[c, BLK:BLK + 8, :]
        return y * _sigmoid(y)

    lane = _iota((BLK, BLK), 1)
    row = _iota((BLK, BLK), 0)
    lane_head = lane >> 6
    row_head = row >> 6
    pattern = lambda cond: jnp.where(cond, 1.0, 0.0).astype(BF16)
    same_head = pattern(lane_head == row_head)
    block_diag = lane_head == row_head
    low_incl = block_diag & (lane <= row)
    low_strict = block_diag & (lane < row)
    head0 = lane_head == 0
    eye = jnp.where(lane == row, 1.0, 0.0).astype(F32)
    cols = lambda t, p: t[:, p * BLK:(p + 1) * BLK]
    rows = lambda t, p: t[p * BLK:(p + 1) * BLK]
    cat0 = lambda xs: jnp.concatenate(xs, axis=0)
    cat1 = lambda xs: jnp.concatenate(xs, axis=1)

    q_all = conv_silu(0, q_ref, cwq_ref)
    k_all = conv_silu(1, k_ref, cwk_ref)
    v_all = conv_silu(2, v_ref, cwv_ref)

    qk = cat0([cols(q_all, p) for p in pairs] + [cols(k_all, p) for p in pairs])
    qk = qk * lax.rsqrt(_dot_xr(qk * qk, same_head) + EPS)
    q = [rows(qk, p) * (HEAD_DIM ** -0.5) for p in pairs]
    k = [rows(qk, npair + p) for p in pairs]
    v = [cols(v_all, p) for p in pairs]

    live = (row >= PAD) | (j > 0)
    hp = [pl.program_id(1) * npair + p for p in pairs]
    picks = cat1([pattern(row == 2 * hp[p] + lane_head) for p in pairs]
                 + [pattern(row == HEADS + 2 * hp[p] + lane_head) for p in pairs])
    beta_g = _dot_xr(bg_ref[0], picks)
    beta = [jnp.where(live, cols(beta_g, p), 0.0) for p in pairs]
    g = [jnp.where(live, cols(beta_g, npair + p), 0.0) for p in pairs]
    sums = _dot_xl(cat0([pattern(low_incl), same_head]), cat1(g))
    gc = [cols(sums[:BLK], p) for p in pairs]
    g_last = [cols(sums[BLK:], p) for p in pairs]
    gc_heads = _dot_xr(cat0(gc), cat1([pattern(row == 0), pattern(row == HEAD_DIM)]))

    k_beta = [k[p] * beta[p] for p in pairs]
    eg = [jnp.exp(gc[p]) for p in pairs]
    rhs_uw = [cat1([v[p] * beta[p], k_beta[p] * eg[p]]) for p in pairs]
    q_decay = [q[p] * eg[p] for p in pairs]
    k_tail = [k[p] * jnp.exp(g_last[p] - gc[p]) for p in pairs]

    chains = [(p, h) for p in pairs for h in range(2)]
    both = [_dot_nt(cat0([jnp.where(lane_head == h, t, 0.0) for h in range(2) for t in (k_beta[p], q[p])]
                         ).astype(BF16), k[p].astype(BF16)) for p in pairs]
    a_k, a_intra = {}, {}
    for p, h in chains:
        gch = rows(gc_heads, p)[:, h * BLK:(h + 1) * BLK]
        decay = jnp.exp(jnp.where(low_incl, gch - gch.T, -jnp.inf))
        a_k[p, h] = -jnp.where(low_strict, rows(both[p], 2 * h) * decay, 0.0)
        a_intra[p, h] = jnp.where(low_incl, rows(both[p], 2 * h + 1) * decay, 0.0)

    s_k = {c: eye + a_k[c] for c in chains}
    a_k = {c: _dot3(a_k[c], a_k[c]) for c in chains}
    for _ in range(4):
        prod = {c: _dot3(a_k[c], cat1([a_k[c], s_k[c]])) for c in chains}
        a_k = {c: prod[c][:, :BLK] for c in chains}
        s_k = {c: s_k[c] + prod[c][:, BLK:] for c in chains}
    last = {c: _dot3(a_k[c], s_k[c]) for c in chains}
    uw = {c: _dot3(s_k[c] + last[c], rhs_uw[c[0]]) for c in chains}
    u = [jnp.where(head0, uw[p, 0][:, :BLK], uw[p, 1][:, :BLK]) for p in pairs]
    w = [jnp.where(head0, uw[p, 0][:, BLK:], uw[p, 1][:, BLK:]) for p in pairs]

    state = [s_ref[p] for p in pairs]
    vn = [[None, None] for _ in pairs]
    inter = [[None, None] for _ in pairs]
    for c in range(2):
        r = slice(c * CHUNK, (c + 1) * CHUNK)
        ws_qs = [_dot(cat0([w[p][r], q_decay[p][r]]).astype(BF16), state[p].astype(BF16)) for p in pairs]
        for p in pairs:
            vn[p][c] = u[p][r] - ws_qs[p][:CHUNK]
            inter[p][c] = ws_qs[p][CHUNK:]
        upd = [_dot_tn(k_tail[p][r].astype(BF16), vn[p][c].astype(BF16)) for p in pairs]
        for p in pairs:
            gl = jnp.exp(g_last[p][c * CHUNK:c * CHUNK + 1, :])
            state[p] = jnp.where(block_diag, state[p] * gl + upd[p], 0.0)
    o = []
    for p in pairs:
        s_ref[p] = state[p]
        vnp = cat0(vn[p])
        o.append(cat0(inter[p]) + _dot(
            cat1([a_intra[p, 0], a_intra[p, 1]]).astype(BF16),
            cat0([jnp.where(head0, vnp, 0.0), jnp.where(head0, 0.0, vnp)]).astype(BF16)))

    o = cat0(o)
    o = o * lax.rsqrt(_dot_xr(o * o, same_head) * (1.0 / HEAD_DIM) + EPS)
    for p in pairs:
        sl = slice(p * BLK, (p + 1) * BLK)
        zz = z_ref[0, :, sl]
        o_ref[0, :, sl] = (rows(o, p) * gn_ref[...] * (zz * _sigmoid(zz))).astype(o_ref.dtype)


def _deltanet(dnqkv, z, bg, conv_w, gn, batch, lp):
    nblk = lp // BLK
    grp = DN_GROUP * BLK
    pairs = WIDTH // grp
    phys = lambda j: (j + nblk - 1) % nblk
    x3 = dnqkv.reshape(batch, lp, 3 * WIDTH)
    z3 = z.reshape(batch, lp, WIDTH)
    bg3 = bg.reshape(batch, lp, 128)
    col = lambda off: pl.BlockSpec((1, BLK, grp), lambda b, p, j: (b, phys(j), off * pairs + p))
    cw = lambda off: pl.BlockSpec((4, grp), lambda b, p, j: (0, off * pairs + p))
    return pl.pallas_call(
        _dn_kernel,
        grid=(batch, pairs, nblk),
        in_specs=[col(0), col(1), col(2),
                  pl.BlockSpec((1, BLK, grp), lambda b, p, j: (b, phys(j), p)),
                  pl.BlockSpec((1, BLK, 128), lambda b, p, j: (b, phys(j), 0)),
                  cw(0), cw(1), cw(2),
                  pl.BlockSpec((1, BLK), lambda b, p, j: (0, 0))],
        out_specs=pl.BlockSpec((1, BLK, grp), lambda b, p, j: (b, jnp.maximum(j - 1, 0), p)),
        out_shape=jax.ShapeDtypeStruct((batch, (nblk - 1) * BLK, WIDTH), BF16),
        scratch_shapes=[pltpu.VMEM((3, BLK + 8, grp), F32), pltpu.VMEM((DN_GROUP, BLK, BLK), F32)],
        compiler_params=pltpu.CompilerParams(
            dimension_semantics=("parallel", "parallel", "arbitrary"), vmem_limit_bytes=VMEM_LIMIT),
        name="deltanet",
    )(x3, x3, x3, z3, bg3, conv_w, conv_w, conv_w, gn)


def _merge_kernel(osb_ref, odn_ref, gate_ref, x_ref, wsb_ref, wdn_ref, wo_ref, h_ref):
    gate = gate_ref[0].astype(F32)
    mix = (gate[:, :D_MODEL] * _dot(osb_ref[...], wsb_ref[...])
           + gate[:, D_MODEL:] * _dot(odn_ref[...], wdn_ref[...]))
    h_ref[...] = x_ref[...] + _dot(mix.astype(BF16), wo_ref[...])


def _merge(o_sb, o_dn, gates, x2, wsb, wdn, wo, batch, seq, lp):
    n = batch * seq
    rb = ROWS_OUT
    per = seq // rb
    row = lambda c: pl.BlockSpec((rb, c), lambda i: (i, 0))
    const = lambda r, c: pl.BlockSpec((r, c), lambda i: (0, 0))
    return pl.pallas_call(
        _merge_kernel,
        grid=(n // rb,),
        in_specs=[row(WIDTH), row(WIDTH),
                  pl.BlockSpec((1, rb, 2 * D_MODEL), lambda i: (i // per, i % per, 0)),
                  row(D_MODEL), const(WIDTH, D_MODEL), const(WIDTH, D_MODEL), const(D_MODEL, D_MODEL)],
        out_specs=row(D_MODEL),
        out_shape=jax.ShapeDtypeStruct((n, D_MODEL), F32),
        compiler_params=pltpu.CompilerParams(dimension_semantics=("parallel",),
                                             vmem_limit_bytes=VMEM_LIMIT),
        name="merge",
    )(o_sb.reshape(n, WIDTH), o_dn.reshape(n, WIDTH), gates.reshape(batch, lp, 2 * D_MODEL),
      x2, wsb, wdn, wo)


def _top_rows(s, k, payload=None):
    nrow = s.shape[0]
    rid = _iota(s.shape, 0)
    vals, picks = [], []
    for _ in range(k):
        m = jnp.max(s, axis=0, keepdims=True)
        am = jnp.min(jnp.where(s == m, rid, nrow), axis=0, keepdims=True)
        hit = rid == am
        vals.append(m)
        picks.append(am if payload is None
                     else jnp.max(jnp.where(hit, payload, -1), axis=0, keepdims=True))
        s = jnp.where(hit, -jnp.inf, s)
    return jnp.concatenate(vals, axis=0), jnp.concatenate(picks, axis=0)


def _route_kernel(h_ref, g2_ref, wq_ref, keys_ref, xn_ref, idx_ref, gate_ref):
    x = h_ref[...]
    ms = jnp.mean(x * x, axis=-1, keepdims=True)
    xn = x * lax.rsqrt(ms + EPS) * g2_ref[...]
    xn_ref[...] = xn
    q = _dot(xn.astype(BF16), wq_ref[...])
    keys = (keys_ref[0].astype(BF16), keys_ref[1].astype(BF16))
    idx_rows, gate_rows = [], []
    for h in range(PEER_HEADS):
        tops = []
        for p in range(2):
            c0 = (2 * h + p) * PEER_KEYS
            s = _dot_nt(keys[p], q[:, c0:c0 + PEER_KEYS].astype(BF16))
            tops.append(_top_rows(s, PEER_TOPK))
        (s1, i1), (s2, i2) = tops
        brow = _iota((8, s1.shape[1]), 0)
        cs, ci = [s1[0:1] + s2], [i1[0:1] * PEER_KEYS + i2]
        for a in range(1, 8):
            keep = PEER_TOPK // (a + 1)
            sa = s1[a:a + 1] + s2[0:8]
            cs.append(sa if keep >= 8 else jnp.where(brow < keep, sa, -jnp.inf))
            ci.append(i1[a:a + 1] * PEER_KEYS + i2[0:8])
        cs.append(s1[8:16] + s2[0:1])
        ci.append(i1[8:16] * PEER_KEYS + i2[0:1])
        top_s, top_i = _top_rows(jnp.concatenate(cs, axis=0), PEER_TOPK, jnp.concatenate(ci, axis=0))
        e = jnp.exp(top_s - top_s[0:1])
        idx_rows.append(top_i)
        gate_rows.append(e / jnp.sum(e, axis=0, keepdims=True))
    idx_ref[...] = jnp.concatenate(idx_rows, axis=0).T
    gate_ref[...] = jnp.concatenate(gate_rows, axis=0).T


def _route(h1, g2, wq, keys):
    n = h1.shape[0]
    rb = ROWS_OUT
    return pl.pallas_call(
        _route_kernel,
        grid=(n // rb,),
        in_specs=[pl.BlockSpec((rb, D_MODEL), lambda i: (i, 0)),
                  pl.BlockSpec((1, D_MODEL), lambda i: (0, 0)),
                  pl.BlockSpec(wq.shape, lambda i: (0, 0)),
                  pl.BlockSpec(keys.shape, lambda i: (0, 0, 0))],
        out_specs=[pl.BlockSpec((rb, D_MODEL), lambda i: (i, 0)),
                   pl.BlockSpec((rb, PEER_HK), lambda i: (i, 0)),
                   pl.BlockSpec((rb, PEER_HK), lambda i: (i, 0))],
        out_shape=[jax.ShapeDtypeStruct((n, D_MODEL), F32),
                   jax.ShapeDtypeStruct((n, PEER_HK), jnp.int32),
                   jax.ShapeDtypeStruct((n, PEER_HK), F32)],
        compiler_params=pltpu.CompilerParams(dimension_semantics=("parallel",),
                                             vmem_limit_bytes=VMEM_LIMIT),
        name="route",
    )(h1, g2, wq, keys)


def _peer_kernel(idx_ref, idx_next_ref, gate_ref, xn_ref, h_ref, sel_ref, rep_ref, tab_hbm, out_ref,
                 *scratch):
    bufs, act_ref, sem = scratch[:-2], scratch[-2], scratch[-1]
    i = pl.program_id(0)
    n = pl.num_programs(0)
    tb = PEER_TB
    sub = D_MODEL // 128
    unroll = PEER_UNROLL
    ones = jnp.ones((sub, 2 * 128), BF16)
    lanes = lambda t, k: t[:, k * 128:(k + 1) * 128]
    trow = _iota((tb, unroll * 128), 0)
    lane_tok = _iota((tb, unroll * 128), 1) >> 7

    def issue_token(src_idx, row0, t, buf, sl, e0=0, e1=PEER_HK):
        for e in range(e0, e1):
            pltpu.make_async_copy(tab_hbm.at[src_idx[row0 + t, e]], buf.at[t, e], sem.at[sl]).start(
                priority=e % DMA_THREADS)

    def drain(buf, sl):
        for t in range(tb):
            pltpu.make_async_copy(tab_hbm.at[pl.ds(0, PEER_HK)], buf.at[t], sem.at[sl]).wait()

    def phase(buf, sl, tok0, nxt_buf, nxt_sl, nxt_idx, nxt_row0):
        drain(buf, sl)

        def tile_rows(t):
            return buf[t]

        def act_body(g, carry):
            prods = []
            for k in range(unroll):
                t = g * unroll + k
                issue_token(nxt_idx, nxt_row0, t, nxt_buf, nxt_sl, 0, PEER_ACT_SHARE)
                u = lax.bitcast_convert_type(tile_rows(t) << 16, F32)
                prods.append((u * xn_ref[tok0 + t][None]).reshape(PEER_HK * sub, 128).astype(BF16))
            part = _dot(sel_ref[...], jnp.concatenate(prods, axis=1))
            hi, lo = _split(jnp.concatenate([lanes(part, k) for k in range(unroll)], axis=0))
            sums = _dot_nt(ones, jnp.concatenate([hi, lo], axis=1))
            for k in range(unroll):
                act_ref[pl.ds(g * unroll + k, 1), :] = lanes(sums, k)[0:1]
            return carry

        lax.fori_loop(0, tb // unroll, act_body, 0)

        act = act_ref[...]
        wgt = (0.5 * act * (1.0 + lax.erf(act * (2.0 ** -0.5))) * gate_ref[tok0:tok0 + tb, :]).astype(BF16)

        def mix_body(g, carry):
            pick = jnp.where(trow == g * unroll + lane_tok, 1.0, 0.0).astype(BF16)
            col = _dot_tn(wgt, pick)
            wide = _dot(rep_ref[...], col.astype(BF16))
            for k in range(unroll):
                t = g * unroll + k
                issue_token(nxt_idx, nxt_row0, t, nxt_buf, nxt_sl, PEER_ACT_SHARE, PEER_HK)
                v = lax.bitcast_convert_type(tile_rows(t) & jnp.uint32(0xFFFF0000), F32)
                out_ref[tok0 + t] = h_ref[tok0 + t] + jnp.sum(
                    lanes(wide, k).reshape(PEER_HK, sub, 128) * v, axis=0)
            return carry

        lax.fori_loop(0, tb // unroll, mix_body, 0)

    nbuf = len(bufs)
    ahead = PEER_AHEAD

    @pl.when(i == 0)
    def _():
        def body(t, carry):
            for p in range(ahead):
                issue_token(idx_ref, p * tb, t, bufs[p], p)
            return carry
        lax.fori_loop(0, tb, body, 0)

    for p in range(nbuf):
        q = p + ahead
        src, row0 = (idx_ref, q * tb) if q < nbuf else (idx_next_ref, (q - nbuf) * tb)
        phase(bufs[p], p, p * tb, bufs[q % nbuf], q % nbuf, src, row0)

    @pl.when(i == n - 1)
    def _():
        for p in range(ahead):
            drain(bufs[p], p)


def _pack_tables(u_tab, v_tab):
    half = lambda t: lax.bitcast_convert_type(t.astype(BF16), jnp.uint16).astype(jnp.uint32)
    packed = half(u_tab) | (half(v_tab) << 16)
    return packed.reshape(u_tab.shape[0], D_MODEL // 128, 128)


def _peer(idx, gate, xn, h1, u_tab, v_tab):
    n = h1.shape[0]
    tb = PEER_BUFS * PEER_TB
    steps = n // tb
    sub = D_MODEL // 128
    tiles = lambda t: t.reshape(n, sub, 128)
    e = np.arange(PEER_HK)
    sel = jnp.asarray(e[:, None] == (np.arange(PEER_HK * sub)[None, :] // sub), BF16)
    smem = lambda f: pl.BlockSpec((tb, PEER_HK), f, memory_space=pltpu.SMEM)
    tok = pl.BlockSpec((tb, sub, 128), lambda i: (i, 0, 0))
    out = pl.pallas_call(
        _peer_kernel,
        grid=(steps,),
        in_specs=[smem(lambda i: (i, 0)),
                  smem(lambda i: (jnp.minimum(i + 1, steps - 1), 0)),
                  pl.BlockSpec((tb, PEER_HK), lambda i: (i, 0)),
                  tok, tok,
                  pl.BlockSpec((PEER_HK, PEER_HK * sub), lambda i: (0, 0)),
                  pl.BlockSpec((PEER_HK * sub, PEER_HK), lambda i: (0, 0)),
                  pl.BlockSpec(memory_space=pl.ANY)],
        out_specs=tok,
        out_shape=jax.ShapeDtypeStruct((n, sub, 128), F32),
        scratch_shapes=[pltpu.VMEM((PEER_TB, PEER_HK, sub, 128), jnp.uint32) for _ in range(PEER_BUFS)]
        + [pltpu.VMEM((PEER_TB, PEER_HK), F32), pltpu.SemaphoreType.DMA((PEER_BUFS,))],
        compiler_params=pltpu.CompilerParams(dimension_semantics=("arbitrary",),
                                             vmem_limit_bytes=VMEM_LIMIT),
        name="peer",
    )(idx, idx, gate, tiles(xn), tiles(h1), sel, sel.T, _pack_tables(u_tab, v_tab))
    return out.reshape(n, D_MODEL)


def _constants():
    r = np.arange(WIDTH)
    ph = (r[:, None] // HEAD_DIM == r[None, :] // HEAD_DIM).astype(np.float32) / HEAD_DIM
    s = np.arange(BLK)
    later = (s[:, None] > s[None, :]).astype(np.float32)
    half = np.concatenate([later, np.ones((BLK, BLK), np.float32)], axis=1)
    tri = np.concatenate([half, half], axis=0)
    return jnp.asarray(ph, BF16), jnp.asarray(tri, BF16)


def _layer(x, meta_tokens, norm1_g, w_in, sb_q_norm_g, sb_k_norm_g, dn_conv_w, dn_a_log, dn_dt_bias,
           dn_out_norm_g, w_sb_out, w_dn_out, w_o, norm2_g, peer_w_q, peer_sub_keys, peer_u, peer_v):
    batch, seq, d = x.shape
    lp = seq + BLK
    ph, tri = _constants()

    tail = jnp.concatenate([jnp.zeros((PAD, d), x.dtype), meta_tokens.astype(x.dtype)], axis=0)
    hp = jnp.concatenate([x, jnp.broadcast_to(tail[None], (batch, BLK, d))], axis=1).reshape(batch * lp, d)

    c_ba = 3 * WIDTH + 3 * WIDTH + WIDTH
    w_all = jnp.concatenate([w_in[:, :c_ba], w_in[:, c_ba + 2 * HEADS:], w_in[:, c_ba:c_ba + 2 * HEADS],
                             jnp.zeros((d, 128 - 2 * HEADS), w_in.dtype)], axis=1).astype(BF16)
    gq = (jnp.tile(sb_q_norm_g.astype(F32), HEADS) * (HEAD_DIM ** -0.5))[None]
    gk = jnp.tile(sb_k_norm_g.astype(F32), HEADS)[None]
    lane_pad = lambda t: jnp.pad(t.astype(F32), (HEADS, 128 - 2 * HEADS))[None]
    acoef = lane_pad(-jnp.exp(dn_a_log.astype(F32)))
    dtb = lane_pad(dn_dt_bias)

    sbqkv, dnqkv, z, gates, bg = _inproj(hp, norm1_g.astype(F32)[None], w_all, ph, gq, gk, acoef, dtb)
    o_sb = _sb_attn(sbqkv, tri, batch, lp)
    gn = jnp.tile(dn_out_norm_g.astype(F32), 2)[None]
    o_dn = _deltanet(dnqkv, z, bg, dn_conv_w.astype(F32), gn, batch, lp)
    h1 = _merge(o_sb, o_dn, gates, x.reshape(batch * seq, d), w_sb_out.astype(BF16),
                w_dn_out.astype(BF16), w_o.astype(BF16), batch, seq, lp)
    xn2, idx, gate = _route(h1, norm2_g.astype(F32)[None], peer_w_q.astype(BF16), peer_sub_keys)
    h2 = _peer(idx, gate, xn2, h1, peer_u, peer_v)
    return h2.reshape(batch, seq, d)


def kernel(x, meta_tokens, norm1_g, w_in, sb_q_norm_g, sb_k_norm_g, dn_conv_w, dn_a_log, dn_dt_bias,
           dn_out_norm_g, w_sb_out, w_dn_out, w_o, norm2_g, peer_w_q, peer_sub_keys, peer_u, peer_v):
    assert norm1_g.shape[0] == 1, "one layer"
    return _layer(x, meta_tokens, norm1_g[0], w_in[0], sb_q_norm_g[0], sb_k_norm_g[0], dn_conv_w[0],
                  dn_a_log[0], dn_dt_bias[0], dn_out_norm_g[0], w_sb_out[0], w_dn_out[0], w_o[0],
                  norm2_g[0], peer_w_q[0], peer_sub_keys[0], peer_u[0], peer_v[0])
```

```python
import functools

import numpy as np
import jax
import jax.numpy as jnp
from jax import lax
from jax.experimental import pallas as pl
from jax.experimental.pallas import tpu as pltpu

F32 = jnp.float32
BF16 = jnp.bfloat16

D_MODEL = 1024
N_META = 16
BLK = 128
PAD = BLK - N_META
HEADS = 8
HEAD_DIM = 64
WIDTH = HEADS * HEAD_DIM
CHUNK = 64
PEER_HEADS = 8
PEER_KEYS = 128
PEER_TOPK = 16
PEER_HK = PEER_HEADS * PEER_TOPK
EPS = 1e-6

C_SB = 0
C_DN = 3 * WIDTH
C_Z = C_DN + 3 * WIDTH
C_GATE = C_Z + WIDTH
C_BA = C_GATE + 2 * D_MODEL
C_END = C_BA + 128

ROWS_IN = 256
ROWS_OUT = 256
SB_STEP_HEADS = 4
DN_GROUP = 4
DMA_THREADS = 2
PEER_TB = 16
PEER_BUFS = 4
PEER_AHEAD = 2
PEER_ACT_SHARE = 64
PEER_UNROLL = 16
VMEM_LIMIT = 56 * 1024 * 1024


def _dot(a, b):
    return jnp.dot(a, b, preferred_element_type=F32)


def _dot_nt(a, b):
    return lax.dot_general(a, b, (((1,), (1,)), ((), ())), preferred_element_type=F32)


def _dot_tn(a, b):
    return lax.dot_general(a, b, (((0,), (0,)), ((), ())), preferred_element_type=F32)


def _split(a):
    hi = a.astype(BF16)
    lo = (a - hi.astype(F32)).astype(BF16)
    return hi, lo


def _dot_xr(a, b_exact):
    hi, lo = _split(a)
    return _dot(jnp.concatenate([hi, lo], axis=1), jnp.concatenate([b_exact, b_exact], axis=0))


def _dot_xl(a_exact, b):
    hi, lo = _split(b)
    return _dot(jnp.concatenate([a_exact, a_exact], axis=1), jnp.concatenate([hi, lo], axis=0))


def _dot3(a, b):
    ah, al = _split(a)
    bh, bl = _split(b)
    return _dot(jnp.concatenate([ah, ah, al], axis=1), jnp.concatenate([bh, bl, bh], axis=0))


def _sigmoid(x):
    return 1.0 / (1.0 + jnp.exp(-x))


def _softplus(x):
    return jnp.maximum(x, 0.0) + jnp.log1p(jnp.exp(-jnp.abs(x)))


def _iota(shape, dim):
    return lax.broadcasted_iota(jnp.int32, shape, dim)


def _inproj_kernel(x_ref, g1_ref, w_ref, ph_ref, gq_ref, gk_ref, acoef_ref, dtb_ref,
                   sb_ref, dn_ref, z_ref, gate_ref, bg_ref):
    x = x_ref[...]
    ms = jnp.mean(x * x, axis=-1, keepdims=True)
    xn = (x * lax.rsqrt(ms + EPS) * g1_ref[...]).astype(BF16)

    def proj(c0, c1):
        return _dot(xn, w_ref[:, c0:c1])

    def head_norm(t, g):
        msh = _dot((t * t).astype(BF16), ph_ref[...])
        return (t * lax.rsqrt(msh + EPS) * g).astype(BF16)

    sb_ref[:, 0:WIDTH] = head_norm(proj(C_SB, C_SB + WIDTH), gq_ref[...])
    sb_ref[:, WIDTH:2 * WIDTH] = head_norm(proj(C_SB + WIDTH, C_SB + 2 * WIDTH), gk_ref[...])
    sb_ref[:, 2 * WIDTH:3 * WIDTH] = proj(C_SB + 2 * WIDTH, C_SB + 3 * WIDTH).astype(BF16)
    dn_ref[...] = proj(C_DN, C_DN + 3 * WIDTH)
    z_ref[...] = proj(C_Z, C_Z + WIDTH)
    gate_ref[...] = _sigmoid(proj(C_GATE, C_GATE + 2 * D_MODEL)).astype(BF16)
    ba = proj(C_BA, C_END)
    lane = _iota(ba.shape, 1)
    bg_ref[...] = jnp.where(lane < HEADS, _sigmoid(ba), acoef_ref[...] * _softplus(ba + dtb_ref[...]))


def _inproj(hp, g1, w_all, ph, gq, gk, acoef, dtb):
    n = hp.shape[0]
    rb = ROWS_IN
    const = lambda shape: pl.BlockSpec(shape, lambda i: (0, 0))
    row = lambda c: pl.BlockSpec((rb, c), lambda i: (i, 0))
    return pl.pallas_call(
        _inproj_kernel,
        grid=(n // rb,),
        in_specs=[row(D_MODEL), const((1, D_MODEL)), const((D_MODEL, C_END)), const((WIDTH, WIDTH)),
                  const((1, WIDTH)), const((1, WIDTH)), const((1, 128)), const((1, 128))],
        out_specs=[row(3 * WIDTH), row(3 * WIDTH), row(WIDTH), row(2 * D_MODEL), row(128)],
        out_shape=[jax.ShapeDtypeStruct((n, 3 * WIDTH), BF16),
                   jax.ShapeDtypeStruct((n, 3 * WIDTH), F32),
                   jax.ShapeDtypeStruct((n, WIDTH), F32),
                   jax.ShapeDtypeStruct((n, 2 * D_MODEL), BF16),
                   jax.ShapeDtypeStruct((n, 128), F32)],
        compiler_params=pltpu.CompilerParams(dimension_semantics=("parallel",),
                                             vmem_limit_bytes=VMEM_LIMIT),
        name="inproj",
    )(hp, g1, w_all, ph, gq, gk, acoef, dtb)


def _sb_kernel(q_ref, k_ref, v_ref, tri_ref, o_ref, carry_ref, acc_ref, *, nblk):
    qi = pl.program_id(2)
    nh = SB_STEP_HEADS
    q = q_ref[0]
    lane = _iota((BLK, BLK), 1)
    row = _iota((BLK, BLK), 0)
    lane_head = _iota(q.shape, 1) >> 6
    qf = q.astype(F32)
    q2 = jnp.concatenate([jnp.where(lane_head == r, qf, 0.0) for r in range(nh)], axis=0).astype(BF16)
    tri = tri_ref[...]
    two = nh * BLK

    cols = lambda t, j: t[:, j * BLK:(j + 1) * BLK]

    def scores(k, nb, vis):
        z = _dot_nt(q2, k)
        neg_abs = lax.bitcast_convert_type(lax.bitcast_convert_type(z, jnp.uint32) | jnp.uint32(0x80000000), F32)
        soft = jnp.log(1.0 + jnp.exp(neg_abs))
        log_beta = jnp.minimum(z, 0.0) - soft
        log_om = log_beta - z
        if vis is not None:
            log_om = jnp.where(vis, log_om, 0.0)
        hi, lo = _split(log_om)
        stacked = jnp.concatenate(
            [jnp.concatenate([cols(hi, j), cols(lo, j)], axis=1) for j in range(nb)], axis=0)
        return log_beta, _dot(stacked, tri)

    def finish(log_beta, rt, nb, v, vis, first=False):
        later = jnp.zeros((two, BLK), F32) if first else carry_ref[...]
        parts = [None] * nb
        for j in reversed(range(nb)):
            rows = rt[j * two:(j + 1) * two]
            parts[j] = cols(log_beta, j) + rows[:, :BLK] + later
            later = later + rows[:, BLK:]
        a = jnp.exp(jnp.concatenate(parts, axis=1))
        if vis is not None:
            a = jnp.where(vis, a, 0.0)
        carry_ref[...] = later
        av = _dot(a.astype(BF16), v)
        acc_ref[...] = av if first else acc_ref[...] + av

    def keys(ref, kphys, nb):
        return ref[0, pl.ds(pl.multiple_of(kphys * BLK, BLK), nb * BLK), :]

    def tile(kphys, nb, state):
        log_beta, rt = scores(keys(k_ref, kphys, nb), nb, None)
        finish(log_beta, rt, nb, keys(v_ref, kphys, nb), None)
        return state

    stack2 = lambda m: jnp.concatenate([m] * nh, axis=0)
    vis_d, vis_0 = stack2(lane < row), stack2(lane >= PAD)
    lb2, rt2 = scores(jnp.concatenate([keys(k_ref, qi, 1), keys(k_ref, nblk - 1, 1)], axis=0), 2,
                      jnp.concatenate([vis_d, vis_0], axis=1))
    finish(cols(lb2, 0), rt2[:two], 1, keys(v_ref, qi, 1), vis_d, first=True)

    rem = qi % 4
    four = (qi // 4) % 2
    lax.fori_loop(0, rem, lambda it, s: tile(qi - 1 - it, 1, s), 0)
    lax.fori_loop(0, four, lambda it, s: tile(qi - rem - 4, 4, s), 0)
    lax.fori_loop(0, qi // 8, lambda it, s: tile(qi - rem - 4 * four - 8 * (it + 1), 8, s), 0)
    finish(cols(lb2, 1), rt2[two:], 1, keys(v_ref, nblk - 1, 1), vis_0)
    acc = acc_ref[...]
    out = acc[:BLK]
    for r in range(1, nh):
        out = jnp.where(lane_head == r, acc[r * BLK:(r + 1) * BLK], out)
    o_ref[0] = out.astype(o_ref.dtype)


def _sb_attn(sbqkv, tri, batch, lp):
    nblk = lp // BLK
    nq = nblk - 1
    wid = SB_STEP_HEADS * HEAD_DIM
    pairs = WIDTH // wid
    x3 = sbqkv.reshape(batch, lp, 3 * WIDTH)
    return pl.pallas_call(
        functools.partial(_sb_kernel, nblk=nblk),
        grid=(batch, pairs, nq),
        in_specs=[pl.BlockSpec((1, BLK, wid), lambda b, p, i: (b, i, p)),
                  pl.BlockSpec((1, lp, wid), lambda b, p, i: (b, 0, pairs + p)),
                  pl.BlockSpec((1, lp, wid), lambda b, p, i: (b, 0, 2 * pairs + p)),
                  pl.BlockSpec((2 * BLK, 2 * BLK), lambda b, p, i: (0, 0))],
        out_specs=pl.BlockSpec((1, BLK, wid), lambda b, p, i: (b, i, p)),
        out_shape=jax.ShapeDtypeStruct((batch, nq * BLK, WIDTH), BF16),
        scratch_shapes=[pltpu.VMEM((SB_STEP_HEADS * BLK, BLK), F32), pltpu.VMEM((SB_STEP_HEADS * BLK, wid), F32)],
        compiler_params=pltpu.CompilerParams(
            dimension_semantics=("parallel", "parallel", "arbitrary"), vmem_limit_bytes=VMEM_LIMIT),
        name="sb_attn",
    )(x3, x3, x3, tri)


def _dn_kernel(q_ref, k_ref, v_ref, z_ref, bg_ref, cwq_ref, cwk_ref, cwv_ref, gn_ref,
               o_ref, xbuf, s_ref):
    j = pl.program_id(2)
    npair = DN_GROUP
    pairs = range(npair)

    @pl.when(j == 0)
    def _():
        xbuf[...] = jnp.zeros_like(xbuf)
        s_ref[...] = jnp.zeros_like(s_ref)

    def conv_silu(c, x_ref, cw_ref):
        xbuf[c, 8:8 + BLK, :] = x_ref[0]
        w = cw_ref[...]
        y = (w[3:4] * xbuf[c, 8:8 + BLK, :] + w[2:3] * xbuf[c, 7:7 + BLK, :]
             + w[1:2] * xbuf[c, 6:6 + BLK, :] + w[0:1] * xbuf[c, 5:5 + BLK, :])
        xbuf[c, 0:8, :] = xbuf[c, BLK:BLK + 8, :]
        return y * _sigmoid(y)

    lane = _iota((BLK, BLK), 1)
    row = _iota((BLK, BLK), 0)
    lane_head = lane >> 6
    row_head = row >> 6
    pattern = lambda cond: jnp.where(cond, 1.0, 0.0).astype(BF16)
    same_head = pattern(lane_head == row_head)
    block_diag = lane_head == row_head
    low_incl = block_diag & (lane <= row)
    low_strict = block_diag & (lane < row)
    head0 = lane_head == 0
    eye = jnp.where(lane == row, 1.0, 0.0).astype(F32)
    cols = lambda t, p: t[:, p * BLK:(p + 1) * BLK]
    rows = lambda t, p: t[p * BLK:(p + 1) * BLK]
    cat0 = lambda xs: jnp.concatenate(xs, axis=0)
    cat1 = lambda xs: jnp.concatenate(xs, axis=1)

    q_all = conv_silu(0, q_ref, cwq_ref)
    k_all = conv_silu(1, k_ref, cwk_ref)
    v_all = conv_silu(2, v_ref, cwv_ref)

    qk = cat0([cols(q_all, p) for p in pairs] + [cols(k_all, p) for p in pairs])
    qk = qk * lax.rsqrt(_dot_xr(qk * qk, same_head) + EPS)
    q = [rows(qk, p) * (HEAD_DIM ** -0.5) for p in pairs]
    k = [rows(qk, npair + p) for p in pairs]
    v = [cols(v_all, p) for p in pairs]

    live = (row >= PAD) | (j > 0)
    hp = [pl.program_id(1) * npair + p for p in pairs]
    picks = cat1([pattern(row == 2 * hp[p] + lane_head) for p in pairs]
                 + [pattern(row == HEADS + 2 * hp[p] + lane_head) for p in pairs])
    beta_g = _dot_xr(bg_ref[0], picks)
    beta = [jnp.where(live, cols(beta_g, p), 0.0) for p in pairs]
    g = [jnp.where(live, cols(beta_g, npair + p), 0.0) for p in pairs]
    sums = _dot_xl(cat0([pattern(low_incl), same_head]), cat1(g))
    gc = [cols(sums[:BLK], p) for p in pairs]
    g_last = [cols(sums[BLK:], p) for p in pairs]
    gc_heads = _dot_xr(cat0(gc), cat1([pattern(row == 0), pattern(row == HEAD_DIM)]))

    k_beta = [k[p] * beta[p] for p in pairs]
    eg = [jnp.exp(gc[p]) for p in pairs]
    rhs_uw = [cat1([v[p] * beta[p], k_beta[p] * eg[p]]) for p in pairs]
    q_decay = [q[p] * eg[p] for p in pairs]
    k_tail = [k[p] * jnp.exp(g_last[p] - gc[p]) for p in pairs]

    chains = [(p, h) for p in pairs for h in range(2)]
    both = [_dot_nt(cat0([jnp.where(lane_head == h, t, 0.0) for h in range(2) for t in (k_beta[p], q[p])]
                         ).astype(BF16), k[p].astype(BF16)) for p in pairs]
    a_k, a_intra = {}, {}
    for p, h in chains:
        gch = rows(gc_heads, p)[:, h * BLK:(h + 1) * BLK]
        decay = jnp.exp(jnp.where(low_incl, gch - gch.T, -jnp.inf))
        a_k[p, h] = -jnp.where(low_strict, rows(both[p], 2 * h) * decay, 0.0)
        a_intra[p, h] = jnp.where(low_incl, rows(both[p], 2 * h + 1) * decay, 0.0)

    s_k = {c: eye + a_k[c] for c in chains}
    a_k = {c: _dot3(a_k[c], a_k[c]) for c in chains}
    for _ in range(4):
        prod = {c: _dot3(a_k[c], cat1([a_k[c], s_k[c]])) for c in chains}
        a_k = {c: prod[c][:, :BLK] for c in chains}
        s_k = {c: s_k[c] + prod[c][:, BLK:] for c in chains}
    last = {c: _dot3(a_k[c], s_k[c]) for c in chains}
    uw = {c: _dot3(s_k[c] + last[c], rhs_uw[c[0]]) for c in chains}
    u = [jnp.where(head0, uw[p, 0][:, :BLK], uw[p, 1][:, :BLK]) for p in pairs]
    w = [jnp.where(head0, uw[p, 0][:, BLK:], uw[p, 1][:, BLK:]) for p in pairs]

    state = [s_ref[p] for p in pairs]
    vn = [[None, None] for _ in pairs]
    inter = [[None, None] for _ in pairs]
    for c in range(2):
        r = slice(c * CHUNK, (c + 1) * CHUNK)
        ws_qs = [_dot(cat0([w[p][r], q_decay[p][r]]).astype(BF16), state[p].astype(BF16)) for p in pairs]
        for p in pairs:
            vn[p][c] = u[p][r] - ws_qs[p][:CHUNK]
            inter[p][c] = ws_qs[p][CHUNK:]
        upd = [_dot_tn(k_tail[p][r].astype(BF16), vn[p][c].astype(BF16)) for p in pairs]
        for p in pairs:
            gl = jnp.exp(g_last[p][c * CHUNK:c * CHUNK + 1, :])
            state[p] = jnp.where(block_diag, state[p] * gl + upd[p], 0.0)
    o = []
    for p in pairs:
        s_ref[p] = state[p]
        vnp = cat0(vn[p])
        o.append(cat0(inter[p]) + _dot(
            cat1([a_intra[p, 0], a_intra[p, 1]]).astype(BF16),
            cat0([jnp.where(head0, vnp, 0.0), jnp.where(head0, 0.0, vnp)]).astype(BF16)))

    o = cat0(o)
    o = o * lax.rsqrt(_dot_xr(o * o, same_head) * (1.0 / HEAD_DIM) + EPS)
    for p in pairs:
        sl = slice(p * BLK, (p + 1) * BLK)
        zz = z_ref[0, :, sl]
        o_ref[0, :, sl] = (rows(o, p) * gn_ref[...] * (zz * _sigmoid(zz))).astype(o_ref.dtype)


def _deltanet(dnqkv, z, bg, conv_w, gn, batch, lp):
    nblk = lp // BLK
    grp = DN_GROUP * BLK
    pairs = WIDTH // grp
    phys = lambda j: (j + nblk - 1) % nblk
    x3 = dnqkv.reshape(batch, lp, 3 * WIDTH)
    z3 = z.reshape(batch, lp, WIDTH)
    bg3 = bg.reshape(batch, lp, 128)
    col = lambda off: pl.BlockSpec((1, BLK, grp), lambda b, p, j: (b, phys(j), off * pairs + p))
    cw = lambda off: pl.BlockSpec((4, grp), lambda b, p, j: (0, off * pairs + p))
    return pl.pallas_call(
        _dn_kernel,
        grid=(batch, pairs, nblk),
        in_specs=[col(0), col(1), col(2),
                  pl.BlockSpec((1, BLK, grp), lambda b, p, j: (b, phys(j), p)),
                  pl.BlockSpec((1, BLK, 128), lambda b, p, j: (b, phys(j), 0)),
                  cw(0), cw(1), cw(2),
                  pl.BlockSpec((1, BLK), lambda b, p, j: (0, 0))],
        out_specs=pl.BlockSpec((1, BLK, grp), lambda b, p, j: (b, jnp.maximum(j - 1, 0), p)),
        out_shape=jax.ShapeDtypeStruct((batch, (nblk - 1) * BLK, WIDTH), BF16),
        scratch_shapes=[pltpu.VMEM((3, BLK + 8, grp), F32), pltpu.VMEM((DN_GROUP, BLK, BLK), F32)],
        compiler_params=pltpu.CompilerParams(
            dimension_semantics=("parallel", "parallel", "arbitrary"), vmem_limit_bytes=VMEM_LIMIT),
        name="deltanet",
    )(x3, x3, x3, z3, bg3, conv_w, conv_w, conv_w, gn)


def _merge_kernel(osb_ref, odn_ref, gate_ref, x_ref, wsb_ref, wdn_ref, wo_ref, h_ref):
    gate = gate_ref[0].astype(F32)
    mix = (gate[:, :D_MODEL] * _dot(osb_ref[...], wsb_ref[...])
           + gate[:, D_MODEL:] * _dot(odn_ref[...], wdn_ref[...]))
    h_ref[...] = x_ref[...] + _dot(mix.astype(BF16), wo_ref[...])


def _merge(o_sb, o_dn, gates, x2, wsb, wdn, wo, batch, seq, lp):
    n = batch * seq
    rb = ROWS_OUT
    per = seq // rb
    row = lambda c: pl.BlockSpec((rb, c), lambda i: (i, 0))
    const = lambda r, c: pl.BlockSpec((r, c), lambda i: (0, 0))
    return pl.pallas_call(
        _merge_kernel,
        grid=(n // rb,),
        in_specs=[row(WIDTH), row(WIDTH),
                  pl.BlockSpec((1, rb, 2 * D_MODEL), lambda i: (i // per, i % per, 0)),
                  row(D_MODEL), const(WIDTH, D_MODEL), const(WIDTH, D_MODEL), const(D_MODEL, D_MODEL)],
        out_specs=row(D_MODEL),
        out_shape=jax.ShapeDtypeStruct((n, D_MODEL), F32),
        compiler_params=pltpu.CompilerParams(dimension_semantics=("parallel",),
                                             vmem_limit_bytes=VMEM_LIMIT),
        name="merge",
    )(o_sb.reshape(n, WIDTH), o_dn.reshape(n, WIDTH), gates.reshape(batch, lp, 2 * D_MODEL),
      x2, wsb, wdn, wo)


def _top_rows(s, k, payload=None):
    nrow = s.shape[0]
    rid = _iota(s.shape, 0)
    vals, picks = [], []
    for _ in range(k):
        m = jnp.max(s, axis=0, keepdims=True)
        am = jnp.min(jnp.where(s == m, rid, nrow), axis=0, keepdims=True)
        hit = rid == am
        vals.append(m)
        picks.append(am if payload is None
                     else jnp.max(jnp.where(hit, payload, -1), axis=0, keepdims=True))
        s = jnp.where(hit, -jnp.inf, s)
    return jnp.concatenate(vals, axis=0), jnp.concatenate(picks, axis=0)


def _route_kernel(h_ref, g2_ref, wq_ref, keys_ref, xn_ref, idx_ref, gate_ref):
    x = h_ref[...]
    ms = jnp.mean(x * x, axis=-1, keepdims=True)
    xn = x * lax.rsqrt(ms + EPS) * g2_ref[...]
    xn_ref[...] = xn
    q = _dot(xn.astype(BF16), wq_ref[...])
    keys = (keys_ref[0].astype(BF16), keys_ref[1].astype(BF16))
    idx_rows, gate_rows = [], []
    for h in range(PEER_HEADS):
        tops = []
        for p in range(2):
            c0 = (2 * h + p) * PEER_KEYS
            s = _dot_nt(keys[p], q[:, c0:c0 + PEER_KEYS].astype(BF16))
            tops.append(_top_rows(s, PEER_TOPK))
        (s1, i1), (s2, i2) = tops
        brow = _iota((8, s1.shape[1]), 0)
        cs, ci = [s1[0:1] + s2], [i1[0:1] * PEER_KEYS + i2]
        for a in range(1, 8):
            keep = PEER_TOPK // (a + 1)
            sa = s1[a:a + 1] + s2[0:8]
            cs.append(sa if keep >= 8 else jnp.where(brow < keep, sa, -jnp.inf))
            ci.append(i1[a:a + 1] * PEER_KEYS + i2[0:8])
        cs.append(s1[8:16] + s2[0:1])
        ci.append(i1[8:16] * PEER_KEYS + i2[0:1])
        top_s, top_i = _top_rows(jnp.concatenate(cs, axis=0), PEER_TOPK, jnp.concatenate(ci, axis=0))
        e = jnp.exp(top_s - top_s[0:1])
        idx_rows.append(top_i)
        gate_rows.append(e / jnp.sum(e, axis=0, keepdims=True))
    idx_ref[...] = jnp.concatenate(idx_rows, axis=0).T
    gate_ref[...] = jnp.concatenate(gate_rows, axis=0).T


def _route(h1, g2, wq, keys):
    n = h1.shape[0]
    rb = ROWS_OUT
    return pl.pallas_call(
        _route_kernel,
        grid=(n // rb,),
        in_specs=[pl.BlockSpec((rb, D_MODEL), lambda i: (i, 0)),
                  pl.BlockSpec((1, D_MODEL), lambda i: (0, 0)),
                  pl.BlockSpec(wq.shape, lambda i: (0, 0)),
                  pl.BlockSpec(keys.shape, lambda i: (0, 0, 0))],
        out_specs=[pl.BlockSpec((rb, D_MODEL), lambda i: (i, 0)),
                   pl.BlockSpec((rb, PEER_HK), lambda i: (i, 0)),
                   pl.BlockSpec((rb, PEER_HK), lambda i: (i, 0))],
        out_shape=[jax.ShapeDtypeStruct((n, D_MODEL), F32),
                   jax.ShapeDtypeStruct((n, PEER_HK), jnp.int32),
                   jax.ShapeDtypeStruct((n, PEER_HK), F32)],
        compiler_params=pltpu.CompilerParams(dimension_semantics=("parallel",),
                                             vmem_limit_bytes=VMEM_LIMIT),
        name="route",
    )(h1, g2, wq, keys)


def _peer_kernel(idx_ref, idx_next_ref, gate_ref, xn_ref, h_ref, sel_ref, rep_ref, tab_hbm, out_ref,
                 *scratch):
    bufs, act_ref, sem = scratch[:-2], scratch[-2], scratch[-1]
    i = pl.program_id(0)
    n = pl.num_programs(0)
    tb = PEER_TB
    sub = D_MODEL // 128
    unroll = PEER_UNROLL
    ones = jnp.ones((sub, 2 * 128), BF16)
    lanes = lambda t, k: t[:, k * 128:(k + 1) * 128]
    trow = _iota((tb, unroll * 128), 0)
    lane_tok = _iota((tb, unroll * 128), 1) >> 7

    def issue_token(src_idx, row0, t, buf, sl, e0=0, e1=PEER_HK):
        for e in range(e0, e1):
            pltpu.make_async_copy(tab_hbm.at[src_idx[row0 + t, e]], buf.at[t, e], sem.at[sl]).start(
                priority=e % DMA_THREADS)

    def drain(buf, sl):
        for t in range(tb):
            pltpu.make_async_copy(tab_hbm.at[pl.ds(0, PEER_HK)], buf.at[t], sem.at[sl]).wait()

    def phase(buf, sl, tok0, nxt_buf, nxt_sl, nxt_idx, nxt_row0):
        drain(buf, sl)

        def tile_rows(t):
            return buf[t]

        def act_body(g, carry):
            prods = []
            for k in range(unroll):
                t = g * unroll + k
                issue_token(nxt_idx, nxt_row0, t, nxt_buf, nxt_sl, 0, PEER_ACT_SHARE)
                u = lax.bitcast_convert_type(tile_rows(t) << 16, F32)
                prods.append((u * xn_ref[tok0 + t][None]).reshape(PEER_HK * sub, 128).astype(BF16))
            part = _dot(sel_ref[...], jnp.concatenate(prods, axis=1))
            hi, lo = _split(jnp.concatenate([lanes(part, k) for k in range(unroll)], axis=0))
            sums = _dot_nt(ones, jnp.concatenate([hi, lo], axis=1))
            for k in range(unroll):
                act_ref[pl.ds(g * unroll + k, 1), :] = lanes(sums, k)[0:1]
            return carry

        lax.fori_loop(0, tb // unroll, act_body, 0)

        act = act_ref[...]
        wgt = (0.5 * act * (1.0 + lax.erf(act * (2.0 ** -0.5))) * gate_ref[tok0:tok0 + tb, :]).astype(BF16)

        def mix_body(g, carry):
            pick = jnp.where(trow == g * unroll + lane_tok, 1.0, 0.0).astype(BF16)
            col = _dot_tn(wgt, pick)
            wide = _dot(rep_ref[...], col.astype(BF16))
            for k in range(unroll):
                t = g * unroll + k
                issue_token(nxt_idx, nxt_row0, t, nxt_buf, nxt_sl, PEER_ACT_SHARE, PEER_HK)
                v = lax.bitcast_convert_type(tile_rows(t) & jnp.uint32(0xFFFF0000), F32)
                out_ref[tok0 + t] = h_ref[tok0 + t] + jnp.sum(
                    lanes(wide, k).reshape(PEER_HK, sub, 128) * v, axis=0)
            return carry

        lax.fori_loop(0, tb // unroll, mix_body, 0)

    nbuf = len(bufs)
    ahead = PEER_AHEAD

    @pl.when(i == 0)
    def _():
        def body(t, carry):
            for p in range(ahead):
                issue_token(idx_ref, p * tb, t, bufs[p], p)
            return carry
        lax.fori_loop(0, tb, body, 0)

    for p in range(nbuf):
        q = p + ahead
        src, row0 = (idx_ref, q * tb) if q < nbuf else (idx_next_ref, (q - nbuf) * tb)
        phase(bufs[p], p, p * tb, bufs[q % nbuf], q % nbuf, src, row0)

    @pl.when(i == n - 1)
    def _():
        for p in range(ahead):
            drain(bufs[p], p)


def _pack_tables(u_tab, v_tab):
    half = lambda t: lax.bitcast_convert_type(t.astype(BF16), jnp.uint16).astype(jnp.uint32)
    packed = half(u_tab) | (half(v_tab) << 16)
    return packed.reshape(u_tab.shape[0], D_MODEL // 128, 128)


def _peer(idx, gate, xn, h1, u_tab, v_tab):
    n = h1.shape[0]
    tb = PEER_BUFS * PEER_TB
    steps = n // tb
    sub = D_MODEL // 128
    tiles = lambda t: t.reshape(n, sub, 128)
    e = np.arange(PEER_HK)
    sel = jnp.asarray(e[:, None] == (np.arange(PEER_HK * sub)[None, :] // sub), BF16)
    smem = lambda f: pl.BlockSpec((tb, PEER_HK), f, memory_space=pltpu.SMEM)
    tok = pl.BlockSpec((tb, sub, 128), lambda i: (i, 0, 0))
    out = pl.pallas_call(
        _peer_kernel,
        grid=(steps,),
        in_specs=[smem(lambda i: (i, 0)),
                  smem(lambda i: (jnp.minimum(i + 1, steps - 1), 0)),
                  pl.BlockSpec((tb, PEER_HK), lambda i: (i, 0)),
                  tok, tok,
                  pl.BlockSpec((PEER_HK, PEER_HK * sub), lambda i: (0, 0)),
                  pl.BlockSpec((PEER_HK * sub, PEER_HK), lambda i: (0, 0)),
                  pl.BlockSpec(memory_space=pl.ANY)],
        out_specs=tok,
        out_shape=jax.ShapeDtypeStruct((n, sub, 128), F32),
        scratch_shapes=[pltpu.VMEM((PEER_TB, PEER_HK, sub, 128), jnp.uint32) for _ in range(PEER_BUFS)]
        + [pltpu.VMEM((PEER_TB, PEER_HK), F32), pltpu.SemaphoreType.DMA((PEER_BUFS,))],
        compiler_params=pltpu.CompilerParams(dimension_semantics=("arbitrary",),
                                             vmem_limit_bytes=VMEM_LIMIT),
        name="peer",
    )(idx, idx, gate, tiles(xn), tiles(h1), sel, sel.T, _pack_tables(u_tab, v_tab))
    return out.reshape(n, D_MODEL)


def _constants():
    r = np.arange(WIDTH)
    ph = (r[:, None] // HEAD_DIM == r[None, :] // HEAD_DIM).astype(np.float32) / HEAD_DIM
    s = np.arange(BLK)
    later = (s[:, None] > s[None, :]).astype(np.float32)
    half = np.concatenate([later, np.ones((BLK, BLK), np.float32)], axis=1)
    tri = np.concatenate([half, half], axis=0)
    return jnp.asarray(ph, BF16), jnp.asarray(tri, BF16)


def _layer(x, meta_tokens, norm1_g, w_in, sb_q_norm_g, sb_k_norm_g, dn_conv_w, dn_a_log, dn_dt_bias,
           dn_out_norm_g, w_sb_out, w_dn_out, w_o, norm2_g, peer_w_q, peer_sub_keys, peer_u, peer_v):
    batch, seq, d = x.shape
    lp = seq + BLK
    ph, tri = _constants()

    tail = jnp.concatenate([jnp.zeros((PAD, d), x.dtype), meta_tokens.astype(x.dtype)], axis=0)
    hp = jnp.concatenate([x, jnp.broadcast_to(tail[None], (batch, BLK, d))], axis=1).reshape(batch * lp, d)

    c_ba = 3 * WIDTH + 3 * WIDTH + WIDTH
    w_all = jnp.concatenate([w_in[:, :c_ba], w_in[:, c_ba + 2 * HEADS:], w_in[:, c_ba:c_ba + 2 * HEADS],
                             jnp.zeros((d, 128 - 2 * HEADS), w_in.dtype)], axis=1).astype(BF16)
    gq = (jnp.tile(sb_q_norm_g.astype(F32), HEADS) * (HEAD_DIM ** -0.5))[None]
    gk = jnp.tile(sb_k_norm_g.astype(F32), HEADS)[None]
    lane_pad = lambda t: jnp.pad(t.astype(F32), (HEADS, 128 - 2 * HEADS))[None]
    acoef = lane_pad(-jnp.exp(dn_a_log.astype(F32)))
    dtb = lane_pad(dn_dt_bias)

    sbqkv, dnqkv, z, gates, bg = _inproj(hp, norm1_g.astype(F32)[None], w_all, ph, gq, gk, acoef, dtb)
    o_sb = _sb_attn(sbqkv, tri, batch, lp)
    gn = jnp.tile(dn_out_norm_g.astype(F32), 2)[None]
    o_dn = _deltanet(dnqkv, z, bg, dn_conv_w.astype(F32), gn, batch, lp)
    h1 = _merge(o_sb, o_dn, gates, x.reshape(batch * seq, d), w_sb_out.astype(BF16),
                w_dn_out.astype(BF16), w_o.astype(BF16), batch, seq, lp)
    xn2, idx, gate = _route(h1, norm2_g.astype(F32)[None], peer_w_q.astype(BF16), peer_sub_keys)
    h2 = _peer(idx, gate, xn2, h1, peer_u, peer_v)
    return h2.reshape(batch, seq, d)


def kernel(x, meta_tokens, norm1_g, w_in, sb_q_norm_g, sb_k_norm_g, dn_conv_w, dn_a_log, dn_dt_bias,
           dn_out_norm_g, w_sb_out, w_dn_out, w_o, norm2_g, peer_w_q, peer_sub_keys, peer_u, peer_v):
    assert norm1_g.shape[0] == 1, "one layer"
    return _layer(x, meta_tokens, norm1_g[0], w_in[0], sb_q_norm_g[0], sb_k_norm_g[0], dn_conv_w[0],
                  dn_a_log[0], dn_dt_bias[0], dn_out_norm_g[0], w_sb_out[0], w_dn_out[0], w_o[0],
                  norm2_g[0], peer_w_q[0], peer_sub_keys[0], peer_u[0], peer_v[0])
```

```python
import functools

import numpy as np
import jax
import jax.numpy as jnp
from jax import lax
from jax.experimental import pallas as pl
from jax.experimental.pallas import tpu as pltpu

F32 = jnp.float32
BF16 = jnp.bfloat16

D_MODEL = 1024
N_META = 16
BLK = 128
PAD = BLK - N_META
HEADS = 8
HEAD_DIM = 64
WIDTH = HEADS * HEAD_DIM
CHUNK = 64
PEER_HEADS = 8
PEER_KEYS = 128
PEER_TOPK = 16
PEER_HK = PEER_HEADS * PEER_TOPK
EPS = 1e-6

C_SB = 0
C_DN = 3 * WIDTH
C_Z = C_DN + 3 * WIDTH
C_GATE = C_Z + WIDTH
C_BA = C_GATE + 2 * D_MODEL
C_END = C_BA + 128

ROWS_IN = 256
ROWS_OUT = 512
SB_STEP_HEADS = 4
DN_GROUP = 4
DMA_THREADS = 2
PEER_TB = 16
PEER_BUFS = 4
PEER_AHEAD = 2
PEER_ACT_SHARE = 64
PEER_UNROLL = 16
VMEM_LIMIT = 56 * 1024 * 1024


def _dot(a, b):
    return jnp.dot(a, b, preferred_element_type=F32)


def _dot_nt(a, b):
    return lax.dot_general(a, b, (((1,), (1,)), ((), ())), preferred_element_type=F32)


def _dot_tn(a, b):
    return lax.dot_general(a, b, (((0,), (0,)), ((), ())), preferred_element_type=F32)


def _split(a):
    hi = a.astype(BF16)
    lo = (a - hi.astype(F32)).astype(BF16)
    return hi, lo


def _dot_xr(a, b_exact):
    hi, lo = _split(a)
    return _dot(jnp.concatenate([hi, lo], axis=1), jnp.concatenate([b_exact, b_exact], axis=0))


def _dot_xl(a_exact, b):
    hi, lo = _split(b)
    return _dot(jnp.concatenate([a_exact, a_exact], axis=1), jnp.concatenate([hi, lo], axis=0))


def _dot3(a, b):
    ah, al = _split(a)
    bh, bl = _split(b)
    return _dot(jnp.concatenate([ah, ah, al], axis=1), jnp.concatenate([bh, bl, bh], axis=0))


def _sigmoid(x):
    return 1.0 / (1.0 + jnp.exp(-x))


def _softplus(x):
    return jnp.maximum(x, 0.0) + jnp.log1p(jnp.exp(-jnp.abs(x)))


def _iota(shape, dim):
    return lax.broadcasted_iota(jnp.int32, shape, dim)


def _inproj_kernel(x_ref, g1_ref, w_ref, ph_ref, gq_ref, gk_ref, acoef_ref, dtb_ref,
                   sb_ref, dn_ref, z_ref, gate_ref, bg_ref):
    x = x_ref[...]
    ms = jnp.mean(x * x, axis=-1, keepdims=True)
    xn = (x * lax.rsqrt(ms + EPS) * g1_ref[...]).astype(BF16)

    def proj(c0, c1):
        return _dot(xn, w_ref[:, c0:c1])

    def head_norm(t, g):
        msh = _dot((t * t).astype(BF16), ph_ref[...])
        return (t * lax.rsqrt(msh + EPS) * g).astype(BF16)

    sb_ref[:, 0:WIDTH] = head_norm(proj(C_SB, C_SB + WIDTH), gq_ref[...])
    sb_ref[:, WIDTH:2 * WIDTH] = head_norm(proj(C_SB + WIDTH, C_SB + 2 * WIDTH), gk_ref[...])
    sb_ref[:, 2 * WIDTH:3 * WIDTH] = proj(C_SB + 2 * WIDTH, C_SB + 3 * WIDTH).astype(BF16)
    dn_ref[...] = proj(C_DN, C_DN + 3 * WIDTH)
    z_ref[...] = proj(C_Z, C_Z + WIDTH)
    gate_ref[...] = _sigmoid(proj(C_GATE, C_GATE + 2 * D_MODEL)).astype(BF16)
    ba = proj(C_BA, C_END)
    lane = _iota(ba.shape, 1)
    bg_ref[...] = jnp.where(lane < HEADS, _sigmoid(ba), acoef_ref[...] * _softplus(ba + dtb_ref[...]))


def _inproj(hp, g1, w_all, ph, gq, gk, acoef, dtb):
    n = hp.shape[0]
    rb = ROWS_IN
    const = lambda shape: pl.BlockSpec(shape, lambda i: (0, 0))
    row = lambda c: pl.BlockSpec((rb, c), lambda i: (i, 0))
    return pl.pallas_call(
        _inproj_kernel,
        grid=(n // rb,),
        in_specs=[row(D_MODEL), const((1, D_MODEL)), const((D_MODEL, C_END)), const((WIDTH, WIDTH)),
                  const((1, WIDTH)), const((1, WIDTH)), const((1, 128)), const((1, 128))],
        out_specs=[row(3 * WIDTH), row(3 * WIDTH), row(WIDTH), row(2 * D_MODEL), row(128)],
        out_shape=[jax.ShapeDtypeStruct((n, 3 * WIDTH), BF16),
                   jax.ShapeDtypeStruct((n, 3 * WIDTH), F32),
                   jax.ShapeDtypeStruct((n, WIDTH), F32),
                   jax.ShapeDtypeStruct((n, 2 * D_MODEL), BF16),
                   jax.ShapeDtypeStruct((n, 128), F32)],
        compiler_params=pltpu.CompilerParams(dimension_semantics=("parallel",),
                                             vmem_limit_bytes=VMEM_LIMIT),
        name="inproj",
    )(hp, g1, w_all, ph, gq, gk, acoef, dtb)


def _sb_kernel(q_ref, k_ref, v_ref, tri_ref, o_ref, carry_ref, acc_ref, *, nblk):
    qi = pl.program_id(2)
    nh = SB_STEP_HEADS
    q = q_ref[0]
    lane = _iota((BLK, BLK), 1)
    row = _iota((BLK, BLK), 0)
    lane_head = _iota(q.shape, 1) >> 6
    qf = q.astype(F32)
    q2 = jnp.concatenate([jnp.where(lane_head == r, qf, 0.0) for r in range(nh)], axis=0).astype(BF16)
    tri = tri_ref[...]
    srows = nh * BLK

    cols = lambda t, j: t[:, j * BLK:(j + 1) * BLK]

    def scores(k, nb, vis):
        z = _dot_nt(q2, k)
        neg_abs = lax.bitcast_convert_type(lax.bitcast_convert_type(z, jnp.uint32) | jnp.uint32(0x80000000), F32)
        soft = jnp.log(1.0 + jnp.exp(neg_abs))
        log_beta = jnp.minimum(z, 0.0) - soft
        log_om = log_beta - z
        if vis is not None:
            log_om = jnp.where(vis, log_om, 0.0)
        hi, lo = _split(log_om)
        stacked = jnp.concatenate(
            [jnp.concatenate([cols(hi, j), cols(lo, j)], axis=1) for j in range(nb)], axis=0)
        return log_beta, _dot(stacked, tri)

    def finish(log_beta, rt, nb, v, vis, first=False):
        later = jnp.zeros((srows, BLK), F32) if first else carry_ref[...]
        parts = [None] * nb
        for j in reversed(range(nb)):
            rows = rt[j * srows:(j + 1) * srows]
            parts[j] = cols(log_beta, j) + rows[:, :BLK] + later
            later = later + rows[:, BLK:]
        a = jnp.exp(jnp.concatenate(parts, axis=1))
        if vis is not None:
            a = jnp.where(vis, a, 0.0)
        carry_ref[...] = later
        av = _dot(a.astype(BF16), v)
        acc_ref[...] = av if first else acc_ref[...] + av

    def keys(ref, kphys, nb):
        return ref[0, pl.ds(pl.multiple_of(kphys * BLK, BLK), nb * BLK), :]

    def tile(kphys, nb, state):
        log_beta, rt = scores(keys(k_ref, kphys, nb), nb, None)
        finish(log_beta, rt, nb, keys(v_ref, kphys, nb), None)
        return state

    stack2 = lambda m: jnp.concatenate([m] * nh, axis=0)
    vis_d, vis_0 = stack2(lane < row), stack2(lane >= PAD)
    lb2, rt2 = scores(jnp.concatenate([keys(k_ref, qi, 1), keys(k_ref, nblk - 1, 1)], axis=0), 2,
                      jnp.concatenate([vis_d, vis_0], axis=1))
    finish(cols(lb2, 0), rt2[:srows], 1, keys(v_ref, qi, 1), vis_d, first=True)

    rem = qi % 4
    four = (qi // 4) % 2
    lax.fori_loop(0, rem, lambda it, s: tile(qi - 1 - it, 1, s), 0)
    lax.fori_loop(0, four, lambda it, s: tile(qi - rem - 4, 4, s), 0)
    lax.fori_loop(0, qi // 8, lambda it, s: tile(qi - rem - 4 * four - 8 * (it + 1), 8, s), 0)
    finish(cols(lb2, 1), rt2[srows:], 1, keys(v_ref, nblk - 1, 1), vis_0)
    acc = acc_ref[...]
    out = acc[:BLK]
    for r in range(1, nh):
        out = jnp.where(lane_head == r, acc[r * BLK:(r + 1) * BLK], out)
    o_ref[0] = out.astype(o_ref.dtype)


def _sb_attn(sbqkv, tri, batch, lp):
    nblk = lp // BLK
    nq = nblk - 1
    wid = SB_STEP_HEADS * HEAD_DIM
    pairs = WIDTH // wid
    x3 = sbqkv.reshape(batch, lp, 3 * WIDTH)
    return pl.pallas_call(
        functools.partial(_sb_kernel, nblk=nblk),
        grid=(batch, pairs, nq),
        in_specs=[pl.BlockSpec((1, BLK, wid), lambda b, p, i: (b, i, p)),
                  pl.BlockSpec((1, lp, wid), lambda b, p, i: (b, 0, pairs + p)),
                  pl.BlockSpec((1, lp, wid), lambda b, p, i: (b, 0, 2 * pairs + p)),
                  pl.BlockSpec((2 * BLK, 2 * BLK), lambda b, p, i: (0, 0))],
        out_specs=pl.BlockSpec((1, BLK, wid), lambda b, p, i: (b, i, p)),
        out_shape=jax.ShapeDtypeStruct((batch, nq * BLK, WIDTH), BF16),
        scratch_shapes=[pltpu.VMEM((SB_STEP_HEADS * BLK, BLK), F32), pltpu.VMEM((SB_STEP_HEADS * BLK, wid), F32)],
        compiler_params=pltpu.CompilerParams(
            dimension_semantics=("parallel", "parallel", "arbitrary"), vmem_limit_bytes=VMEM_LIMIT),
        name="sb_attn",
    )(x3, x3, x3, tri)


def _dn_kernel(q_ref, k_ref, v_ref, z_ref, bg_ref, cwq_ref, cwk_ref, cwv_ref, gn_ref,
               o_ref, xbuf, s_ref):
    j = pl.program_id(2)
    npair = DN_GROUP
    pairs = range(npair)

    @pl.when(j == 0)
    def _():
        xbuf[...] = jnp.zeros_like(xbuf)
        s_ref[...] = jnp.zeros_like(s_ref)

    def conv_silu(c, x_ref, cw_ref):
        xbuf[c, 8:8 + BLK, :] = x_ref[0]
        w = cw_ref[...]
        y = (w[3:4] * xbuf[c, 8:8 + BLK, :] + w[2:3] * xbuf[c, 7:7 + BLK, :]
             + w[1:2] * xbuf[c, 6:6 + BLK, :] + w[0:1] * xbuf[c, 5:5 + BLK, :])
        xbuf[c, 0:8, :] = xbuf[c, BLK:BLK + 8, :]
        return y * _sigmoid(y)

    lane = _iota((BLK, BLK), 1)
    row = _iota((BLK, BLK), 0)
    lane_head = lane >> 6
    row_head = row >> 6
    pattern = lambda cond: jnp.where(cond, 1.0, 0.0).astype(BF16)
    same_head = pattern(lane_head == row_head)
    block_diag = lane_head == row_head
    low_incl = block_diag & (lane <= row)
    low_strict = block_diag & (lane < row)
    head0 = lane_head == 0
    eye = jnp.where(lane == row, 1.0, 0.0).astype(F32)
    cols = lambda t, p: t[:, p * BLK:(p + 1) * BLK]
    rows = lambda t, p: t[p * BLK:(p + 1) * BLK]
    cat0 = lambda xs: jnp.concatenate(xs, axis=0)
    cat1 = lambda xs: jnp.concatenate(xs, axis=1)

    q_all = conv_silu(0, q_ref, cwq_ref)
    k_all = conv_silu(1, k_ref, cwk_ref)
    v_all = conv_silu(2, v_ref, cwv_ref)

    qk = cat0([cols(q_all, p) for p in pairs] + [cols(k_all, p) for p in pairs])
    qk = qk * lax.rsqrt(_dot_xr(qk * qk, same_head) + EPS)
    q = [rows(qk, p) * (HEAD_DIM ** -0.5) for p in pairs]
    k = [rows(qk, npair + p) for p in pairs]
    v = [cols(v_all, p) for p in pairs]

    live = (row >= PAD) | (j > 0)
    hp = [pl.program_id(1) * npair + p for p in pairs]
    picks = cat1([pattern(row == 2 * hp[p] + lane_head) for p in pairs]
                 + [pattern(row == HEADS + 2 * hp[p] + lane_head) for p in pairs])
    beta_g = _dot_xr(bg_ref[0], picks)
    beta = [jnp.where(live, cols(beta_g, p), 0.0) for p in pairs]
    g = [jnp.where(live, cols(beta_g, npair + p), 0.0) for p in pairs]
    sums = _dot_xl(cat0([pattern(low_incl), same_head]), cat1(g))
    gc = [cols(sums[:BLK], p) for p in pairs]
    g_last = [cols(sums[BLK:], p) for p in pairs]
    gc_heads = _dot_xr(cat0(gc), cat1([pattern(row == 0), pattern(row == HEAD_DIM)]))

    k_beta = [k[p] * beta[p] for p in pairs]
    eg = [jnp.exp(gc[p]) for p in pairs]
    rhs_uw = [cat1([v[p] * beta[p], k_beta[p] * eg[p]]) for p in pairs]
    q_decay = [q[p] * eg[p] for p in pairs]
    k_tail = [k[p] * jnp.exp(g_last[p] - gc[p]) for p in pairs]

    chains = [(p, h) for p in pairs for h in range(2)]
    both = [_dot_nt(cat0([jnp.where(lane_head == h, t, 0.0) for h in range(2) for t in (k_beta[p], q[p])]
                         ).astype(BF16), k[p].astype(BF16)) for p in pairs]
    a_k, a_intra = {}, {}
    for p, h in chains:
        gch = rows(gc_heads, p)[:, h * BLK:(h + 1) * BLK]
        decay = jnp.exp(jnp.where(low_incl, gch - gch.T, -jnp.inf))
        a_k[p, h] = -jnp.where(low_strict, rows(both[p], 2 * h) * decay, 0.0)
        a_intra[p, h] = jnp.where(low_incl, rows(both[p], 2 * h + 1) * decay, 0.0)

    s_k = {c: eye + a_k[c] for c in chains}
    a_k = {c: _dot3(a_k[c], a_k[c]) for c in chains}
    for _ in range(4):
        prod = {c: _dot3(a_k[c], cat1([a_k[c], s_k[c]])) for c in chains}
        a_k = {c: prod[c][:, :BLK] for c in chains}
        s_k = {c: s_k[c] + prod[c][:, BLK:] for c in chains}
    last = {c: _dot3(a_k[c], s_k[c]) for c in chains}
    uw = {c: _dot3(s_k[c] + last[c], rhs_uw[c[0]]) for c in chains}
    u = [jnp.where(head0, uw[p, 0][:, :BLK], uw[p, 1][:, :BLK]) for p in pairs]
    w = [jnp.where(head0, uw[p, 0][:, BLK:], uw[p, 1][:, BLK:]) for p in pairs]

    state = [s_ref[p] for p in pairs]
    vn = [[None, None] for _ in pairs]
    inter = [[None, None] for _ in pairs]
    for c in range(2):
        r = slice(c * CHUNK, (c + 1) * CHUNK)
        ws_qs = [_dot(cat0([w[p][r], q_decay[p][r]]).astype(BF16), state[p].astype(BF16)) for p in pairs]
        for p in pairs:
            vn[p][c] = u[p][r] - ws_qs[p][:CHUNK]
            inter[p][c] = ws_qs[p][CHUNK:]
        upd = [_dot_tn(k_tail[p][r].astype(BF16), vn[p][c].astype(BF16)) for p in pairs]
        for p in pairs:
            gl = jnp.exp(g_last[p][c * CHUNK:c * CHUNK + 1, :])
            state[p] = jnp.where(block_diag, state[p] * gl + upd[p], 0.0)
    o = []
    for p in pairs:
        s_ref[p] = state[p]
        vnp = cat0(vn[p])
        o.append(cat0(inter[p]) + _dot(
            cat1([a_intra[p, 0], a_intra[p, 1]]).astype(BF16),
            cat0([jnp.where(head0, vnp, 0.0), jnp.where(head0, 0.0, vnp)]).astype(BF16)))

    o = cat0(o)
    o = o * lax.rsqrt(_dot_xr(o * o, same_head) * (1.0 / HEAD_DIM) + EPS)
    for p in pairs:
        sl = slice(p * BLK, (p + 1) * BLK)
        zz = z_ref[0, :, sl]
        o_ref[0, :, sl] = (rows(o, p) * gn_ref[...] * (zz * _sigmoid(zz))).astype(o_ref.dtype)


def _deltanet(dnqkv, z, bg, conv_w, gn, batch, lp):
    nblk = lp // BLK
    grp = DN_GROUP * BLK
    pairs = WIDTH // grp
    phys = lambda j: (j + nblk - 1) % nblk
    x3 = dnqkv.reshape(batch, lp, 3 * WIDTH)
    z3 = z.reshape(batch, lp, WIDTH)
    bg3 = bg.reshape(batch, lp, 128)
    col = lambda off: pl.BlockSpec((1, BLK, grp), lambda b, p, j: (b, phys(j), off * pairs + p))
    cw = lambda off: pl.BlockSpec((4, grp), lambda b, p, j: (0, off * pairs + p))
    return pl.pallas_call(
        _dn_kernel,
        grid=(batch, pairs, nblk),
        in_specs=[col(0), col(1), col(2),
                  pl.BlockSpec((1, BLK, grp), lambda b, p, j: (b, phys(j), p)),
                  pl.BlockSpec((1, BLK, 128), lambda b, p, j: (b, phys(j), 0)),
                  cw(0), cw(1), cw(2),
                  pl.BlockSpec((1, BLK), lambda b, p, j: (0, 0))],
        out_specs=pl.BlockSpec((1, BLK, grp), lambda b, p, j: (b, jnp.maximum(j - 1, 0), p)),
        out_shape=jax.ShapeDtypeStruct((batch, (nblk - 1) * BLK, WIDTH), BF16),
        scratch_shapes=[pltpu.VMEM((3, BLK + 8, grp), F32), pltpu.VMEM((DN_GROUP, BLK, BLK), F32)],
        compiler_params=pltpu.CompilerParams(
            dimension_semantics=("parallel", "parallel", "arbitrary"), vmem_limit_bytes=VMEM_LIMIT),
        name="deltanet",
    )(x3, x3, x3, z3, bg3, conv_w, conv_w, conv_w, gn)


def _merge_kernel(osb_ref, odn_ref, gate_ref, x_ref, wsb_ref, wdn_ref, wo_ref, h_ref):
    gate = gate_ref[0].astype(F32)
    mix = (gate[:, :D_MODEL] * _dot(osb_ref[...], wsb_ref[...])
           + gate[:, D_MODEL:] * _dot(odn_ref[...], wdn_ref[...]))
    h_ref[...] = x_ref[...] + _dot(mix.astype(BF16), wo_ref[...])


def _merge(o_sb, o_dn, gates, x2, wsb, wdn, wo, batch, seq, lp):
    n = batch * seq
    rb = ROWS_OUT
    assert seq % rb == 0, "row blocks must not straddle batches"
    per = seq // rb
    row = lambda c: pl.BlockSpec((rb, c), lambda i: (i, 0))
    const = lambda r, c: pl.BlockSpec((r, c), lambda i: (0, 0))
    return pl.pallas_call(
        _merge_kernel,
        grid=(n // rb,),
        in_specs=[row(WIDTH), row(WIDTH),
                  pl.BlockSpec((1, rb, 2 * D_MODEL), lambda i: (i // per, i % per, 0)),
                  row(D_MODEL), const(WIDTH, D_MODEL), const(WIDTH, D_MODEL), const(D_MODEL, D_MODEL)],
        out_specs=row(D_MODEL),
        out_shape=jax.ShapeDtypeStruct((n, D_MODEL), F32),
        compiler_params=pltpu.CompilerParams(dimension_semantics=("parallel",),
                                             vmem_limit_bytes=VMEM_LIMIT),
        name="merge",
    )(o_sb.reshape(n, WIDTH), o_dn.reshape(n, WIDTH), gates.reshape(batch, lp, 2 * D_MODEL),
      x2, wsb, wdn, wo)


def _top_rows(s, k, payload=None):
    nrow = s.shape[0]
    rid = _iota(s.shape, 0)
    vals, picks = [], []
    for _ in range(k):
        m = jnp.max(s, axis=0, keepdims=True)
        am = jnp.min(jnp.where(s == m, rid, nrow), axis=0, keepdims=True)
        hit = rid == am
        vals.append(m)
        picks.append(am if payload is None
                     else jnp.max(jnp.where(hit, payload, -1), axis=0, keepdims=True))
        s = jnp.where(hit, -jnp.inf, s)
    return jnp.concatenate(vals, axis=0), jnp.concatenate(picks, axis=0)


def _route_kernel(h_ref, g2_ref, wq_ref, keys_ref, xn_ref, idx_ref, gate_ref):
    x = h_ref[...]
    ms = jnp.mean(x * x, axis=-1, keepdims=True)
    xn = x * lax.rsqrt(ms + EPS) * g2_ref[...]
    xn_ref[...] = xn
    q = _dot(xn.astype(BF16), wq_ref[...])
    keys = (keys_ref[0].astype(BF16), keys_ref[1].astype(BF16))
    idx_rows, gate_rows = [], []
    for h in range(PEER_HEADS):
        tops = []
        for p in range(2):
            c0 = (2 * h + p) * PEER_KEYS
            s = _dot_nt(keys[p], q[:, c0:c0 + PEER_KEYS].astype(BF16))
            tops.append(_top_rows(s, PEER_TOPK))
        (s1, i1), (s2, i2) = tops
        brow = _iota((8, s1.shape[1]), 0)
        cs, ci = [s1[0:1] + s2], [i1[0:1] * PEER_KEYS + i2]
        for a in range(1, 8):
            keep = PEER_TOPK // (a + 1)
            sa = s1[a:a + 1] + s2[0:8]
            cs.append(sa if keep >= 8 else jnp.where(brow < keep, sa, -jnp.inf))
            ci.append(i1[a:a + 1] * PEER_KEYS + i2[0:8])
        cs.append(s1[8:16] + s2[0:1])
        ci.append(i1[8:16] * PEER_KEYS + i2[0:1])
        top_s, top_i = _top_rows(jnp.concatenate(cs, axis=0), PEER_TOPK, jnp.concatenate(ci, axis=0))
        e = jnp.exp(top_s - top_s[0:1])
        idx_rows.append(top_i)
        gate_rows.append(e / jnp.sum(e, axis=0, keepdims=True))
    idx_ref[...] = jnp.concatenate(idx_rows, axis=0).T
    gate_ref[...] = jnp.concatenate(gate_rows, axis=0).T


def _route(h1, g2, wq, keys):
    n = h1.shape[0]
    rb = ROWS_OUT
    return pl.pallas_call(
        _route_kernel,
        grid=(n // rb,),
        in_specs=[pl.BlockSpec((rb, D_MODEL), lambda i: (i, 0)),
                  pl.BlockSpec((1, D_MODEL), lambda i: (0, 0)),
                  pl.BlockSpec(wq.shape, lambda i: (0, 0)),
                  pl.BlockSpec(keys.shape, lambda i: (0, 0, 0))],
        out_specs=[pl.BlockSpec((rb, D_MODEL), lambda i: (i, 0)),
                   pl.BlockSpec((rb, PEER_HK), lambda i: (i, 0)),
                   pl.BlockSpec((rb, PEER_HK), lambda i: (i, 0))],
        out_shape=[jax.ShapeDtypeStruct((n, D_MODEL), F32),
                   jax.ShapeDtypeStruct((n, PEER_HK), jnp.int32),
                   jax.ShapeDtypeStruct((n, PEER_HK), F32)],
        compiler_params=pltpu.CompilerParams(dimension_semantics=("parallel",),
                                             vmem_limit_bytes=VMEM_LIMIT),
        name="route",
    )(h1, g2, wq, keys)


def _peer_kernel(idx_ref, idx_next_ref, gate_ref, xn_ref, h_ref, sel_ref, rep_ref, tab_hbm, out_ref,
                 *scratch):
    bufs, act_ref, sem = scratch[:-2], scratch[-2], scratch[-1]
    i = pl.program_id(0)
    n = pl.num_programs(0)
    tb = PEER_TB
    sub = D_MODEL // 128
    unroll = PEER_UNROLL
    ones = jnp.ones((sub, 2 * 128), BF16)
    lanes = lambda t, k: t[:, k * 128:(k + 1) * 128]
    trow = _iota((tb, unroll * 128), 0)
    lane_tok = _iota((tb, unroll * 128), 1) >> 7

    def issue_token(src_idx, row0, t, buf, sl, e0=0, e1=PEER_HK):
        for e in range(e0, e1):
            pltpu.make_async_copy(tab_hbm.at[src_idx[row0 + t, e]], buf.at[t, e], sem.at[sl]).start(
                priority=e % DMA_THREADS)

    def drain(buf, sl):
        for t in range(tb):
            pltpu.make_async_copy(tab_hbm.at[pl.ds(0, PEER_HK)], buf.at[t], sem.at[sl]).wait()

    def phase(buf, sl, tok0, nxt_buf, nxt_sl, nxt_idx, nxt_row0):
        drain(buf, sl)

        def tile_rows(t):
            return buf[t]

        def act_body(g, carry):
            prods = []
            for k in range(unroll):
                t = g * unroll + k
                issue_token(nxt_idx, nxt_row0, t, nxt_buf, nxt_sl, 0, PEER_ACT_SHARE)
                u = lax.bitcast_convert_type(tile_rows(t) << 16, F32)
                prods.append((u * xn_ref[tok0 + t][None]).reshape(PEER_HK * sub, 128).astype(BF16))
            part = _dot(sel_ref[...], jnp.concatenate(prods, axis=1))
            hi, lo = _split(jnp.concatenate([lanes(part, k) for k in range(unroll)], axis=0))
            sums = _dot_nt(ones, jnp.concatenate([hi, lo], axis=1))
            for k in range(unroll):
                act_ref[pl.ds(g * unroll + k, 1), :] = lanes(sums, k)[0:1]
            return carry

        lax.fori_loop(0, tb // unroll, act_body, 0)

        act = act_ref[...]
        wgt = (0.5 * act * (1.0 + lax.erf(act * (2.0 ** -0.5))) * gate_ref[tok0:tok0 + tb, :]).astype(BF16)

        def mix_body(g, carry):
            pick = jnp.where(trow == g * unroll + lane_tok, 1.0, 0.0).astype(BF16)
            col = _dot_tn(wgt, pick)
            wide = _dot(rep_ref[...], col.astype(BF16))
            for k in range(unroll):
                t = g * unroll + k
                issue_token(nxt_idx, nxt_row0, t, nxt_buf, nxt_sl, PEER_ACT_SHARE, PEER_HK)
                v = lax.bitcast_convert_type(tile_rows(t) & jnp.uint32(0xFFFF0000), F32)
                out_ref[tok0 + t] = h_ref[tok0 + t] + jnp.sum(
                    lanes(wide, k).reshape(PEER_HK, sub, 128) * v, axis=0)
            return carry

        lax.fori_loop(0, tb // unroll, mix_body, 0)

    nbuf = len(bufs)
    ahead = PEER_AHEAD

    @pl.when(i == 0)
    def _():
        def body(t, carry):
            for p in range(ahead):
                issue_token(idx_ref, p * tb, t, bufs[p], p)
            return carry
        lax.fori_loop(0, tb, body, 0)

    for p in range(nbuf):
        q = p + ahead
        src, row0 = (idx_ref, q * tb) if q < nbuf else (idx_next_ref, (q - nbuf) * tb)
        phase(bufs[p], p, p * tb, bufs[q % nbuf], q % nbuf, src, row0)

    @pl.when(i == n - 1)
    def _():
        for p in range(ahead):
            drain(bufs[p], p)


def _pack_tables(u_tab, v_tab):
    half = lambda t: lax.bitcast_convert_type(t.astype(BF16), jnp.uint16).astype(jnp.uint32)
    packed = half(u_tab) | (half(v_tab) << 16)
    return packed.reshape(u_tab.shape[0], D_MODEL // 128, 128)


def _peer(idx, gate, xn, h1, u_tab, v_tab):
    n = h1.shape[0]
    tb = PEER_BUFS * PEER_TB
    steps = n // tb
    sub = D_MODEL // 128
    tiles = lambda t: t.reshape(n, sub, 128)
    e = np.arange(PEER_HK)
    sel = jnp.asarray(e[:, None] == (np.arange(PEER_HK * sub)[None, :] // sub), BF16)
    smem = lambda f: pl.BlockSpec((tb, PEER_HK), f, memory_space=pltpu.SMEM)
    tok = pl.BlockSpec((tb, sub, 128), lambda i: (i, 0, 0))
    out = pl.pallas_call(
        _peer_kernel,
        grid=(steps,),
        in_specs=[smem(lambda i: (i, 0)),
                  smem(lambda i: (jnp.minimum(i + 1, steps - 1), 0)),
                  pl.BlockSpec((tb, PEER_HK), lambda i: (i, 0)),
                  tok, tok,
                  pl.BlockSpec((PEER_HK, PEER_HK * sub), lambda i: (0, 0)),
                  pl.BlockSpec((PEER_HK * sub, PEER_HK), lambda i: (0, 0)),
                  pl.BlockSpec(memory_space=pl.ANY)],
        out_specs=tok,
        out_shape=jax.ShapeDtypeStruct((n, sub, 128), F32),
        scratch_shapes=[pltpu.VMEM((PEER_TB, PEER_HK, sub, 128), jnp.uint32) for _ in range(PEER_BUFS)]
        + [pltpu.VMEM((PEER_TB, PEER_HK), F32), pltpu.SemaphoreType.DMA((PEER_BUFS,))],
        compiler_params=pltpu.CompilerParams(dimension_semantics=("arbitrary",),
                                             vmem_limit_bytes=VMEM_LIMIT),
        name="peer",
    )(idx, idx, gate, tiles(xn), tiles(h1), sel, sel.T, _pack_tables(u_tab, v_tab))
    return out.reshape(n, D_MODEL)


def _constants():
    r = np.arange(WIDTH)
    ph = (r[:, None] // HEAD_DIM == r[None, :] // HEAD_DIM).astype(np.float32) / HEAD_DIM
    s = np.arange(BLK)
    later = (s[:, None] > s[None, :]).astype(np.float32)
    half = np.concatenate([later, np.ones((BLK, BLK), np.float32)], axis=1)
    tri = np.concatenate([half, half], axis=0)
    return jnp.asarray(ph, BF16), jnp.asarray(tri, BF16)


def _layer(x, meta_tokens, norm1_g, w_in, sb_q_norm_g, sb_k_norm_g, dn_conv_w, dn_a_log, dn_dt_bias,
           dn_out_norm_g, w_sb_out, w_dn_out, w_o, norm2_g, peer_w_q, peer_sub_keys, peer_u, peer_v):
    batch, seq, d = x.shape
    lp = seq + BLK
    ph, tri = _constants()

    tail = jnp.concatenate([jnp.zeros((PAD, d), x.dtype), meta_tokens.astype(x.dtype)], axis=0)
    hp = jnp.concatenate([x, jnp.broadcast_to(tail[None], (batch, BLK, d))], axis=1).reshape(batch * lp, d)

    c_ba = 3 * WIDTH + 3 * WIDTH + WIDTH
    w_all = jnp.concatenate([w_in[:, :c_ba], w_in[:, c_ba + 2 * HEADS:], w_in[:, c_ba:c_ba + 2 * HEADS],
                             jnp.zeros((d, 128 - 2 * HEADS), w_in.dtype)], axis=1).astype(BF16)
    gq = (jnp.tile(sb_q_norm_g.astype(F32), HEADS) * (HEAD_DIM ** -0.5))[None]
    gk = jnp.tile(sb_k_norm_g.astype(F32), HEADS)[None]
    lane_pad = lambda t: jnp.pad(t.astype(F32), (HEADS, 128 - 2 * HEADS))[None]
    acoef = lane_pad(-jnp.exp(dn_a_log.astype(F32)))
    dtb = lane_pad(dn_dt_bias)

    sbqkv, dnqkv, z, gates, bg = _inproj(hp, norm1_g.astype(F32)[None], w_all, ph, gq, gk, acoef, dtb)
    o_sb = _sb_attn(sbqkv, tri, batch, lp)
    gn = jnp.tile(dn_out_norm_g.astype(F32), 2)[None]
    o_dn = _deltanet(dnqkv, z, bg, dn_conv_w.astype(F32), gn, batch, lp)
    h1 = _merge(o_sb, o_dn, gates, x.reshape(batch * seq, d), w_sb_out.astype(BF16),
                w_dn_out.astype(BF16), w_o.astype(BF16), batch, seq, lp)
    xn2, idx, gate = _route(h1, norm2_g.astype(F32)[None], peer_w_q.astype(BF16), peer_sub_keys)
    h2 = _peer(idx, gate, xn2, h1, peer_u, peer_v)
    return h2.reshape(batch, seq, d)


def kernel(x, meta_tokens, norm1_g, w_in, sb_q_norm_g, sb_k_norm_g, dn_conv_w, dn_a_log, dn_dt_bias,
           dn_out_norm_g, w_sb_out, w_dn_out, w_o, norm2_g, peer_w_q, peer_sub_keys, peer_u, peer_v):
    assert norm1_g.shape[0] == 1, "one layer"
    return _layer(x, meta_tokens, norm1_g[0], w_in[0], sb_q_norm_g[0], sb_k_norm_g[0], dn_conv_w[0],
                  dn_a_log[0], dn_dt_bias[0], dn_out_norm_g[0], w_sb_out[0], w_dn_out[0], w_o[0],
                  norm2_g[0], peer_w_q[0], peer_sub_keys[0], peer_u[0], peer_v[0])
```

```python
import functools

import numpy as np
import jax
import jax.numpy as jnp
from jax import lax
from jax.experimental import pallas as pl
from jax.experimental.pallas import tpu as pltpu

F32 = jnp.float32
BF16 = jnp.bfloat16

D_MODEL = 1024
N_META = 16
BLK = 128
PAD = BLK - N_META
HEADS = 8
HEAD_DIM = 64
WIDTH = HEADS * HEAD_DIM
CHUNK = 64
PEER_HEADS = 8
PEER_KEYS = 128
PEER_TOPK = 16
PEER_HK = PEER_HEADS * PEER_TOPK
EPS = 1e-6

C_SB = 0
C_DN = 3 * WIDTH
C_Z = C_DN + 3 * WIDTH
C_GATE = C_Z + WIDTH
C_BA = C_GATE + 2 * D_MODEL
C_END = C_BA + 128

ROWS_IN = 256
ROWS_OUT = 512
SB_STEP_HEADS = 4
DN_GROUP = 4
DMA_THREADS = 1
PEER_TB = 16
PEER_BUFS = 4
PEER_AHEAD = 2
PEER_ACT_SHARE = 64
PEER_UNROLL = 16
VMEM_LIMIT = 56 * 1024 * 1024


def _dot(a, b):
    return jnp.dot(a, b, preferred_element_type=F32)


def _dot_nt(a, b):
    return lax.dot_general(a, b, (((1,), (1,)), ((), ())), preferred_element_type=F32)


def _dot_tn(a, b):
    return lax.dot_general(a, b, (((0,), (0,)), ((), ())), preferred_element_type=F32)


def _split(a):
    hi = a.astype(BF16)
    lo = (a - hi.astype(F32)).astype(BF16)
    return hi, lo


def _dot_xr(a, b_exact):
    hi, lo = _split(a)
    return _dot(jnp.concatenate([hi, lo], axis=1), jnp.concatenate([b_exact, b_exact], axis=0))


def _dot_xl(a_exact, b):
    hi, lo = _split(b)
    return _dot(jnp.concatenate([a_exact, a_exact], axis=1), jnp.concatenate([hi, lo], axis=0))


def _dot3(a, b):
    ah, al = _split(a)
    bh, bl = _split(b)
    return _dot(jnp.concatenate([ah, ah, al], axis=1), jnp.concatenate([bh, bl, bh], axis=0))


def _sigmoid(x):
    return 1.0 / (1.0 + jnp.exp(-x))


def _softplus(x):
    return jnp.maximum(x, 0.0) + jnp.log1p(jnp.exp(-jnp.abs(x)))


def _iota(shape, dim):
    return lax.broadcasted_iota(jnp.int32, shape, dim)


def _inproj_kernel(x_ref, g1_ref, w_ref, ph_ref, gq_ref, gk_ref, acoef_ref, dtb_ref,
                   sb_ref, dn_ref, z_ref, gate_ref, bg_ref):
    x = x_ref[...]
    ms = jnp.mean(x * x, axis=-1, keepdims=True)
    xn = (x * lax.rsqrt(ms + EPS) * g1_ref[...]).astype(BF16)

    def proj(c0, c1):
        return _dot(xn, w_ref[:, c0:c1])

    def head_norm(t, g):
        msh = _dot((t * t).astype(BF16), ph_ref[...])
        return (t * lax.rsqrt(msh + EPS) * g).astype(BF16)

    sb_ref[:, 0:WIDTH] = head_norm(proj(C_SB, C_SB + WIDTH), gq_ref[...])
    sb_ref[:, WIDTH:2 * WIDTH] = head_norm(proj(C_SB + WIDTH, C_SB + 2 * WIDTH), gk_ref[...])
    sb_ref[:, 2 * WIDTH:3 * WIDTH] = proj(C_SB + 2 * WIDTH, C_SB + 3 * WIDTH).astype(BF16)
    dn_ref[...] = proj(C_DN, C_DN + 3 * WIDTH)
    z_ref[...] = proj(C_Z, C_Z + WIDTH)
    gate_ref[...] = _sigmoid(proj(C_GATE, C_GATE + 2 * D_MODEL)).astype(BF16)
    ba = proj(C_BA, C_END)
    lane = _iota(ba.shape, 1)
    bg_ref[...] = jnp.where(lane < HEADS, _sigmoid(ba), acoef_ref[...] * _softplus(ba + dtb_ref[...]))


def _inproj(hp, g1, w_all, ph, gq, gk, acoef, dtb):
    n = hp.shape[0]
    rb = ROWS_IN
    const = lambda shape: pl.BlockSpec(shape, lambda i: (0, 0))
    row = lambda c: pl.BlockSpec((rb, c), lambda i: (i, 0))
    return pl.pallas_call(
        _inproj_kernel,
        grid=(n // rb,),
        in_specs=[row(D_MODEL), const((1, D_MODEL)), const((D_MODEL, C_END)), const((WIDTH, WIDTH)),
                  const((1, WIDTH)), const((1, WIDTH)), const((1, 128)), const((1, 128))],
        out_specs=[row(3 * WIDTH), row(3 * WIDTH), row(WIDTH), row(2 * D_MODEL), row(128)],
        out_shape=[jax.ShapeDtypeStruct((n, 3 * WIDTH), BF16),
                   jax.ShapeDtypeStruct((n, 3 * WIDTH), F32),
                   jax.ShapeDtypeStruct((n, WIDTH), F32),
                   jax.ShapeDtypeStruct((n, 2 * D_MODEL), BF16),
                   jax.ShapeDtypeStruct((n, 128), F32)],
        compiler_params=pltpu.CompilerParams(dimension_semantics=("parallel",),
                                             vmem_limit_bytes=VMEM_LIMIT),
        name="inproj",
    )(hp, g1, w_all, ph, gq, gk, acoef, dtb)


def _sb_kernel(q_ref, k_ref, v_ref, tri_ref, o_ref, carry_ref, acc_ref, *, nblk):
    qi = pl.program_id(2)
    nh = SB_STEP_HEADS
    q = q_ref[0]
    lane = _iota((BLK, BLK), 1)
    row = _iota((BLK, BLK), 0)
    lane_head = _iota(q.shape, 1) >> 6
    qf = q.astype(F32)
    q2 = jnp.concatenate([jnp.where(lane_head == r, qf, 0.0) for r in range(nh)], axis=0).astype(BF16)
    tri = tri_ref[...]
    srows = nh * BLK

    cols = lambda t, j: t[:, j * BLK:(j + 1) * BLK]

    def scores(k, nb, vis):
        z = _dot_nt(q2, k)
        neg_abs = lax.bitcast_convert_type(lax.bitcast_convert_type(z, jnp.uint32) | jnp.uint32(0x80000000), F32)
        soft = jnp.log(1.0 + jnp.exp(neg_abs))
        log_beta = jnp.minimum(z, 0.0) - soft
        log_om = log_beta - z
        if vis is not None:
            log_om = jnp.where(vis, log_om, 0.0)
        hi, lo = _split(log_om)
        stacked = jnp.concatenate(
            [jnp.concatenate([cols(hi, j), cols(lo, j)], axis=1) for j in range(nb)], axis=0)
        return log_beta, _dot(stacked, tri)

    def finish(log_beta, rt, nb, v, vis, first=False):
        later = jnp.zeros((srows, BLK), F32) if first else carry_ref[...]
        parts = [None] * nb
        for j in reversed(range(nb)):
            rows = rt[j * srows:(j + 1) * srows]
            parts[j] = cols(log_beta, j) + rows[:, :BLK] + later
            later = later + rows[:, BLK:]
        a = jnp.exp(jnp.concatenate(parts, axis=1))
        if vis is not None:
            a = jnp.where(vis, a, 0.0)
        carry_ref[...] = later
        av = _dot(a.astype(BF16), v)
        acc_ref[...] = av if first else acc_ref[...] + av

    def keys(ref, kphys, nb):
        return ref[0, pl.ds(pl.multiple_of(kphys * BLK, BLK), nb * BLK), :]

    def tile(kphys, nb, state):
        log_beta, rt = scores(keys(k_ref, kphys, nb), nb, None)
        finish(log_beta, rt, nb, keys(v_ref, kphys, nb), None)
        return state

    stack2 = lambda m: jnp.concatenate([m] * nh, axis=0)
    vis_d, vis_0 = stack2(lane < row), stack2(lane >= PAD)
    lb2, rt2 = scores(jnp.concatenate([keys(k_ref, qi, 1), keys(k_ref, nblk - 1, 1)], axis=0), 2,
                      jnp.concatenate([vis_d, vis_0], axis=1))
    finish(cols(lb2, 0), rt2[:srows], 1, keys(v_ref, qi, 1), vis_d, first=True)

    rem = qi % 4
    four = (qi // 4) % 2
    lax.fori_loop(0, rem, lambda it, s: tile(qi - 1 - it, 1, s), 0)
    lax.fori_loop(0, four, lambda it, s: tile(qi - rem - 4, 4, s), 0)
    lax.fori_loop(0, qi // 8, lambda it, s: tile(qi - rem - 4 * four - 8 * (it + 1), 8, s), 0)
    finish(cols(lb2, 1), rt2[srows:], 1, keys(v_ref, nblk - 1, 1), vis_0)
    acc = acc_ref[...]
    out = acc[:BLK]
    for r in range(1, nh):
        out = jnp.where(lane_head == r, acc[r * BLK:(r + 1) * BLK], out)
    o_ref[0] = out.astype(o_ref.dtype)


def _sb_attn(sbqkv, tri, batch, lp):
    nblk = lp // BLK
    nq = nblk - 1
    wid = SB_STEP_HEADS * HEAD_DIM
    pairs = WIDTH // wid
    x3 = sbqkv.reshape(batch, lp, 3 * WIDTH)
    return pl.pallas_call(
        functools.partial(_sb_kernel, nblk=nblk),
        grid=(batch, pairs, nq),
        in_specs=[pl.BlockSpec((1, BLK, wid), lambda b, p, i: (b, i, p)),
                  pl.BlockSpec((1, lp, wid), lambda b, p, i: (b, 0, pairs + p)),
                  pl.BlockSpec((1, lp, wid), lambda b, p, i: (b, 0, 2 * pairs + p)),
                  pl.BlockSpec((2 * BLK, 2 * BLK), lambda b, p, i: (0, 0))],
        out_specs=pl.BlockSpec((1, BLK, wid), lambda b, p, i: (b, i, p)),
        out_shape=jax.ShapeDtypeStruct((batch, nq * BLK, WIDTH), BF16),
        scratch_shapes=[pltpu.VMEM((SB_STEP_HEADS * BLK, BLK), F32), pltpu.VMEM((SB_STEP_HEADS * BLK, wid), F32)],
        compiler_params=pltpu.CompilerParams(
            dimension_semantics=("parallel", "parallel", "arbitrary"), vmem_limit_bytes=VMEM_LIMIT),
        name="sb_attn",
    )(x3, x3, x3, tri)


def _dn_kernel(q_ref, k_ref, v_ref, z_ref, bg_ref, cwq_ref, cwk_ref, cwv_ref, gn_ref,
               o_ref, xbuf, s_ref):
    j = pl.program_id(2)
    npair = DN_GROUP
    pairs = range(npair)

    @pl.when(j == 0)
    def _():
        xbuf[...] = jnp.zeros_like(xbuf)
        s_ref[...] = jnp.zeros_like(s_ref)

    def conv_silu(c, x_ref, cw_ref):
        xbuf[c, 8:8 + BLK, :] = x_ref[0]
        w = cw_ref[...]
        y = (w[3:4] * xbuf[c, 8:8 + BLK, :] + w[2:3] * xbuf[c, 7:7 + BLK, :]
             + w[1:2] * xbuf[c, 6:6 + BLK, :] + w[0:1] * xbuf[c, 5:5 + BLK, :])
        xbuf[c, 0:8, :] = xbuf[c, BLK:BLK + 8, :]
        return y * _sigmoid(y)

    lane = _iota((BLK, BLK), 1)
    row = _iota((BLK, BLK), 0)
    lane_head = lane >> 6
    row_head = row >> 6
    pattern = lambda cond: jnp.where(cond, 1.0, 0.0).astype(BF16)
    same_head = pattern(lane_head == row_head)
    block_diag = lane_head == row_head
    low_incl = block_diag & (lane <= row)
    low_strict = block_diag & (lane < row)
    head0 = lane_head == 0
    eye = jnp.where(lane == row, 1.0, 0.0).astype(F32)
    cols = lambda t, p: t[:, p * BLK:(p + 1) * BLK]
    rows = lambda t, p: t[p * BLK:(p + 1) * BLK]
    cat0 = lambda xs: jnp.concatenate(xs, axis=0)
    cat1 = lambda xs: jnp.concatenate(xs, axis=1)

    q_all = conv_silu(0, q_ref, cwq_ref)
    k_all = conv_silu(1, k_ref, cwk_ref)
    v_all = conv_silu(2, v_ref, cwv_ref)

    qk = cat0([cols(q_all, p) for p in pairs] + [cols(k_all, p) for p in pairs])
    qk = qk * lax.rsqrt(_dot_xr(qk * qk, same_head) + EPS)
    q = [rows(qk, p) * (HEAD_DIM ** -0.5) for p in pairs]
    k = [rows(qk, npair + p) for p in pairs]
    v = [cols(v_all, p) for p in pairs]

    live = (row >= PAD) | (j > 0)
    hp = [pl.program_id(1) * npair + p for p in pairs]
    picks = cat1([pattern(row == 2 * hp[p] + lane_head) for p in pairs]
                 + [pattern(row == HEADS + 2 * hp[p] + lane_head) for p in pairs])
    beta_g = _dot_xr(bg_ref[0], picks)
    beta = [jnp.where(live, cols(beta_g, p), 0.0) for p in pairs]
    g = [jnp.where(live, cols(beta_g, npair + p), 0.0) for p in pairs]
    sums = _dot_xl(cat0([pattern(low_incl), same_head]), cat1(g))
    gc = [cols(sums[:BLK], p) for p in pairs]
    g_last = [cols(sums[BLK:], p) for p in pairs]
    gc_heads = _dot_xr(cat0(gc), cat1([pattern(row == 0), pattern(row == HEAD_DIM)]))

    k_beta = [k[p] * beta[p] for p in pairs]
    eg = [jnp.exp(gc[p]) for p in pairs]
    rhs_uw = [cat1([v[p] * beta[p], k_beta[p] * eg[p]]) for p in pairs]
    q_decay = [q[p] * eg[p] for p in pairs]
    k_tail = [k[p] * jnp.exp(g_last[p] - gc[p]) for p in pairs]

    chains = [(p, h) for p in pairs for h in range(2)]
    both = [_dot_nt(cat0([jnp.where(lane_head == h, t, 0.0) for h in range(2) for t in (k_beta[p], q[p])]
                         ).astype(BF16), k[p].astype(BF16)) for p in pairs]
    a_k, a_intra = {}, {}
    for p, h in chains:
        gch = rows(gc_heads, p)[:, h * BLK:(h + 1) * BLK]
        decay = jnp.exp(jnp.where(low_incl, gch - gch.T, -jnp.inf))
        a_k[p, h] = -jnp.where(low_strict, rows(both[p], 2 * h) * decay, 0.0)
        a_intra[p, h] = jnp.where(low_incl, rows(both[p], 2 * h + 1) * decay, 0.0)

    s_k = {c: eye + a_k[c] for c in chains}
    a_k = {c: _dot3(a_k[c], a_k[c]) for c in chains}
    for _ in range(4):
        prod = {c: _dot3(a_k[c], cat1([a_k[c], s_k[c]])) for c in chains}
        a_k = {c: prod[c][:, :BLK] for c in chains}
        s_k = {c: s_k[c] + prod[c][:, BLK:] for c in chains}
    last = {c: _dot3(a_k[c], s_k[c]) for c in chains}
    uw = {c: _dot3(s_k[c] + last[c], rhs_uw[c[0]]) for c in chains}
    u = [jnp.where(head0, uw[p, 0][:, :BLK], uw[p, 1][:, :BLK]) for p in pairs]
    w = [jnp.where(head0, uw[p, 0][:, BLK:], uw[p, 1][:, BLK:]) for p in pairs]

    state = [s_ref[p] for p in pairs]
    vn = [[None, None] for _ in pairs]
    inter = [[None, None] for _ in pairs]
    for c in range(2):
        r = slice(c * CHUNK, (c + 1) * CHUNK)
        ws_qs = [_dot(cat0([w[p][r], q_decay[p][r]]).astype(BF16), state[p].astype(BF16)) for p in pairs]
        for p in pairs:
            vn[p][c] = u[p][r] - ws_qs[p][:CHUNK]
            inter[p][c] = ws_qs[p][CHUNK:]
        upd = [_dot_tn(k_tail[p][r].astype(BF16), vn[p][c].astype(BF16)) for p in pairs]
        for p in pairs:
            gl = jnp.exp(g_last[p][c * CHUNK:c * CHUNK + 1, :])
            state[p] = jnp.where(block_diag, state[p] * gl + upd[p], 0.0)
    o = []
    for p in pairs:
        s_ref[p] = state[p]
        vnp = cat0(vn[p])
        o.append(cat0(inter[p]) + _dot(
            cat1([a_intra[p, 0], a_intra[p, 1]]).astype(BF16),
            cat0([jnp.where(head0, vnp, 0.0), jnp.where(head0, 0.0, vnp)]).astype(BF16)))

    o = cat0(o)
    o = o * lax.rsqrt(_dot_xr(o * o, same_head) * (1.0 / HEAD_DIM) + EPS)
    for p in pairs:
        sl = slice(p * BLK, (p + 1) * BLK)
        zz = z_ref[0, :, sl]
        o_ref[0, :, sl] = (rows(o, p) * gn_ref[...] * (zz * _sigmoid(zz))).astype(o_ref.dtype)


def _deltanet(dnqkv, z, bg, conv_w, gn, batch, lp):
    nblk = lp // BLK
    grp = DN_GROUP * BLK
    pairs = WIDTH // grp
    phys = lambda j: (j + nblk - 1) % nblk
    x3 = dnqkv.reshape(batch, lp, 3 * WIDTH)
    z3 = z.reshape(batch, lp, WIDTH)
    bg3 = bg.reshape(batch, lp, 128)
    col = lambda off: pl.BlockSpec((1, BLK, grp), lambda b, p, j: (b, phys(j), off * pairs + p))
    cw = lambda off: pl.BlockSpec((4, grp), lambda b, p, j: (0, off * pairs + p))
    return pl.pallas_call(
        _dn_kernel,
        grid=(batch, pairs, nblk),
        in_specs=[col(0), col(1), col(2),
                  pl.BlockSpec((1, BLK, grp), lambda b, p, j: (b, phys(j), p)),
                  pl.BlockSpec((1, BLK, 128), lambda b, p, j: (b, phys(j), 0)),
                  cw(0), cw(1), cw(2),
                  pl.BlockSpec((1, BLK), lambda b, p, j: (0, 0))],
        out_specs=pl.BlockSpec((1, BLK, grp), lambda b, p, j: (b, jnp.maximum(j - 1, 0), p)),
        out_shape=jax.ShapeDtypeStruct((batch, (nblk - 1) * BLK, WIDTH), BF16),
        scratch_shapes=[pltpu.VMEM((3, BLK + 8, grp), F32), pltpu.VMEM((DN_GROUP, BLK, BLK), F32)],
        compiler_params=pltpu.CompilerParams(
            dimension_semantics=("parallel", "parallel", "arbitrary"), vmem_limit_bytes=VMEM_LIMIT),
        name="deltanet",
    )(x3, x3, x3, z3, bg3, conv_w, conv_w, conv_w, gn)


def _merge_kernel(osb_ref, odn_ref, gate_ref, x_ref, wsb_ref, wdn_ref, wo_ref, h_ref):
    gate = gate_ref[0].astype(F32)
    mix = (gate[:, :D_MODEL] * _dot(osb_ref[...], wsb_ref[...])
           + gate[:, D_MODEL:] * _dot(odn_ref[...], wdn_ref[...]))
    h_ref[...] = x_ref[...] + _dot(mix.astype(BF16), wo_ref[...])


def _merge(o_sb, o_dn, gates, x2, wsb, wdn, wo, batch, seq, lp):
    n = batch * seq
    rb = ROWS_OUT
    assert seq % rb == 0, "row blocks must not straddle batches"
    per = seq // rb
    row = lambda c: pl.BlockSpec((rb, c), lambda i: (i, 0))
    const = lambda r, c: pl.BlockSpec((r, c), lambda i: (0, 0))
    return pl.pallas_call(
        _merge_kernel,
        grid=(n // rb,),
        in_specs=[row(WIDTH), row(WIDTH),
                  pl.BlockSpec((1, rb, 2 * D_MODEL), lambda i: (i // per, i % per, 0)),
                  row(D_MODEL), const(WIDTH, D_MODEL), const(WIDTH, D_MODEL), const(D_MODEL, D_MODEL)],
        out_specs=row(D_MODEL),
        out_shape=jax.ShapeDtypeStruct((n, D_MODEL), F32),
        compiler_params=pltpu.CompilerParams(dimension_semantics=("parallel",),
                                             vmem_limit_bytes=VMEM_LIMIT),
        name="merge",
    )(o_sb.reshape(n, WIDTH), o_dn.reshape(n, WIDTH), gates.reshape(batch, lp, 2 * D_MODEL),
      x2, wsb, wdn, wo)


def _top_rows(s, k, payload=None):
    nrow = s.shape[0]
    rid = _iota(s.shape, 0)
    vals, picks = [], []
    for _ in range(k):
        m = jnp.max(s, axis=0, keepdims=True)
        am = jnp.min(jnp.where(s == m, rid, nrow), axis=0, keepdims=True)
        hit = rid == am
        vals.append(m)
        picks.append(am if payload is None
                     else jnp.max(jnp.where(hit, payload, -1), axis=0, keepdims=True))
        s = jnp.where(hit, -jnp.inf, s)
    return jnp.concatenate(vals, axis=0), jnp.concatenate(picks, axis=0)


def _route_kernel(h_ref, g2_ref, wq_ref, keys_ref, xn_ref, idx_ref, gate_ref):
    x = h_ref[...]
    ms = jnp.mean(x * x, axis=-1, keepdims=True)
    xn = x * lax.rsqrt(ms + EPS) * g2_ref[...]
    xn_ref[...] = xn
    q = _dot(xn.astype(BF16), wq_ref[...])
    keys = (keys_ref[0].astype(BF16), keys_ref[1].astype(BF16))
    idx_rows, gate_rows = [], []
    for h in range(PEER_HEADS):
        tops = []
        for p in range(2):
            c0 = (2 * h + p) * PEER_KEYS
            s = _dot_nt(keys[p], q[:, c0:c0 + PEER_KEYS].astype(BF16))
            tops.append(_top_rows(s, PEER_TOPK))
        (s1, i1), (s2, i2) = tops
        brow = _iota((8, s1.shape[1]), 0)
        cs, ci = [s1[0:1] + s2], [i1[0:1] * PEER_KEYS + i2]
        for a in range(1, 8):
            keep = PEER_TOPK // (a + 1)
            sa = s1[a:a + 1] + s2[0:8]
            cs.append(sa if keep >= 8 else jnp.where(brow < keep, sa, -jnp.inf))
            ci.append(i1[a:a + 1] * PEER_KEYS + i2[0:8])
        cs.append(s1[8:16] + s2[0:1])
        ci.append(i1[8:16] * PEER_KEYS + i2[0:1])
        top_s, top_i = _top_rows(jnp.concatenate(cs, axis=0), PEER_TOPK, jnp.concatenate(ci, axis=0))
        e = jnp.exp(top_s - top_s[0:1])
        idx_rows.append(top_i)
        gate_rows.append(e / jnp.sum(e, axis=0, keepdims=True))
    idx_ref[...] = jnp.concatenate(idx_rows, axis=0).T
    gate_ref[...] = jnp.concatenate(gate_rows, axis=0).T


def _route(h1, g2, wq, keys):
    n = h1.shape[0]
    rb = ROWS_OUT
    return pl.pallas_call(
        _route_kernel,
        grid=(n // rb,),
        in_specs=[pl.BlockSpec((rb, D_MODEL), lambda i: (i, 0)),
                  pl.BlockSpec((1, D_MODEL), lambda i: (0, 0)),
                  pl.BlockSpec(wq.shape, lambda i: (0, 0)),
                  pl.BlockSpec(keys.shape, lambda i: (0, 0, 0))],
        out_specs=[pl.BlockSpec((rb, D_MODEL), lambda i: (i, 0)),
                   pl.BlockSpec((rb, PEER_HK), lambda i: (i, 0)),
                   pl.BlockSpec((rb, PEER_HK), lambda i: (i, 0))],
        out_shape=[jax.ShapeDtypeStruct((n, D_MODEL), F32),
                   jax.ShapeDtypeStruct((n, PEER_HK), jnp.int32),
                   jax.ShapeDtypeStruct((n, PEER_HK), F32)],
        compiler_params=pltpu.CompilerParams(dimension_semantics=("parallel",),
                                             vmem_limit_bytes=VMEM_LIMIT),
        name="route",
    )(h1, g2, wq, keys)


def _peer_kernel(idx_ref, idx_next_ref, gate_ref, xn_ref, h_ref, sel_ref, rep_ref, tab_hbm, out_ref,
                 *scratch):
    bufs, act_ref, sem = scratch[:-2], scratch[-2], scratch[-1]
    i = pl.program_id(0)
    n = pl.num_programs(0)
    tb = PEER_TB
    sub = D_MODEL // 128
    unroll = PEER_UNROLL
    ones = jnp.ones((sub, 2 * 128), BF16)
    lanes = lambda t, k: t[:, k * 128:(k + 1) * 128]
    trow = _iota((tb, unroll * 128), 0)
    lane_tok = _iota((tb, unroll * 128), 1) >> 7

    def issue_token(src_idx, row0, t, buf, sl, e0=0, e1=PEER_HK):
        for e in range(e0, e1):
            pltpu.make_async_copy(tab_hbm.at[src_idx[row0 + t, e]], buf.at[t, e], sem.at[sl]).start(
                priority=e % DMA_THREADS)

    def drain(buf, sl):
        for t in range(tb):
            pltpu.make_async_copy(tab_hbm.at[pl.ds(0, PEER_HK)], buf.at[t], sem.at[sl]).wait()

    def phase(buf, sl, tok0, nxt_buf, nxt_sl, nxt_idx, nxt_row0):
        drain(buf, sl)

        def tile_rows(t):
            return buf[t]

        def act_body(g, carry):
            prods = []
            for k in range(unroll):
                t = g * unroll + k
                issue_token(nxt_idx, nxt_row0, t, nxt_buf, nxt_sl, 0, PEER_ACT_SHARE)
                u = lax.bitcast_convert_type(tile_rows(t) << 16, F32)
                prods.append((u * xn_ref[tok0 + t][None]).reshape(PEER_HK * sub, 128).astype(BF16))
            part = _dot(sel_ref[...], jnp.concatenate(prods, axis=1))
            hi, lo = _split(jnp.concatenate([lanes(part, k) for k in range(unroll)], axis=0))
            sums = _dot_nt(ones, jnp.concatenate([hi, lo], axis=1))
            for k in range(unroll):
                act_ref[pl.ds(g * unroll + k, 1), :] = lanes(sums, k)[0:1]
            return carry

        lax.fori_loop(0, tb // unroll, act_body, 0)

        act = act_ref[...]
        wgt = (0.5 * act * (1.0 + lax.erf(act * (2.0 ** -0.5))) * gate_ref[tok0:tok0 + tb, :]).astype(BF16)

        def mix_body(g, carry):
            pick = jnp.where(trow == g * unroll + lane_tok, 1.0, 0.0).astype(BF16)
            col = _dot_tn(wgt, pick)
            wide = _dot(rep_ref[...], col.astype(BF16))
            for k in range(unroll):
                t = g * unroll + k
                issue_token(nxt_idx, nxt_row0, t, nxt_buf, nxt_sl, PEER_ACT_SHARE, PEER_HK)
                v = lax.bitcast_convert_type(tile_rows(t) & jnp.uint32(0xFFFF0000), F32)
                out_ref[tok0 + t] = h_ref[tok0 + t] + jnp.sum(
                    lanes(wide, k).reshape(PEER_HK, sub, 128) * v, axis=0)
            return carry

        lax.fori_loop(0, tb // unroll, mix_body, 0)

    nbuf = len(bufs)
    ahead = PEER_AHEAD

    @pl.when(i == 0)
    def _():
        def body(t, carry):
            for p in range(ahead):
                issue_token(idx_ref, p * tb, t, bufs[p], p)
            return carry
        lax.fori_loop(0, tb, body, 0)

    for p in range(nbuf):
        q = p + ahead
        src, row0 = (idx_ref, q * tb) if q < nbuf else (idx_next_ref, (q - nbuf) * tb)
        phase(bufs[p], p, p * tb, bufs[q % nbuf], q % nbuf, src, row0)

    @pl.when(i == n - 1)
    def _():
        for p in range(ahead):
            drain(bufs[p], p)


def _pack_tables(u_tab, v_tab):
    half = lambda t: lax.bitcast_convert_type(t.astype(BF16), jnp.uint16).astype(jnp.uint32)
    packed = half(u_tab) | (half(v_tab) << 16)
    return packed.reshape(u_tab.shape[0], D_MODEL // 128, 128)


def _peer(idx, gate, xn, h1, u_tab, v_tab):
    n = h1.shape[0]
    tb = PEER_BUFS * PEER_TB
    steps = n // tb
    sub = D_MODEL // 128
    tiles = lambda t: t.reshape(n, sub, 128)
    e = np.arange(PEER_HK)
    sel = jnp.asarray(e[:, None] == (np.arange(PEER_HK * sub)[None, :] // sub), BF16)
    smem = lambda f: pl.BlockSpec((tb, PEER_HK), f, memory_space=pltpu.SMEM)
    tok = pl.BlockSpec((tb, sub, 128), lambda i: (i, 0, 0))
    out = pl.pallas_call(
        _peer_kernel,
        grid=(steps,),
        in_specs=[smem(lambda i: (i, 0)),
                  smem(lambda i: (jnp.minimum(i + 1, steps - 1), 0)),
                  pl.BlockSpec((tb, PEER_HK), lambda i: (i, 0)),
                  tok, tok,
                  pl.BlockSpec((PEER_HK, PEER_HK * sub), lambda i: (0, 0)),
                  pl.BlockSpec((PEER_HK * sub, PEER_HK), lambda i: (0, 0)),
                  pl.BlockSpec(memory_space=pl.ANY)],
        out_specs=tok,
        out_shape=jax.ShapeDtypeStruct((n, sub, 128), F32),
        scratch_shapes=[pltpu.VMEM((PEER_TB, PEER_HK, sub, 128), jnp.uint32) for _ in range(PEER_BUFS)]
        + [pltpu.VMEM((PEER_TB, PEER_HK), F32), pltpu.SemaphoreType.DMA((PEER_BUFS,))],
        compiler_params=pltpu.CompilerParams(dimension_semantics=("arbitrary",),
                                             vmem_limit_bytes=VMEM_LIMIT),
        name="peer",
    )(idx, idx, gate, tiles(xn), tiles(h1), sel, sel.T, _pack_tables(u_tab, v_tab))
    return out.reshape(n, D_MODEL)


def _constants():
    r = np.arange(WIDTH)
    ph = (r[:, None] // HEAD_DIM == r[None, :] // HEAD_DIM).astype(np.float32) / HEAD_DIM
    s = np.arange(BLK)
    later = (s[:, None] > s[None, :]).astype(np.float32)
    half = np.concatenate([later, np.ones((BLK, BLK), np.float32)], axis=1)
    tri = np.concatenate([half, half], axis=0)
    return jnp.asarray(ph, BF16), jnp.asarray(tri, BF16)


def _layer(x, meta_tokens, norm1_g, w_in, sb_q_norm_g, sb_k_norm_g, dn_conv_w, dn_a_log, dn_dt_bias,
           dn_out_norm_g, w_sb_out, w_dn_out, w_o, norm2_g, peer_w_q, peer_sub_keys, peer_u, peer_v):
    batch, seq, d = x.shape
    lp = seq + BLK
    ph, tri = _constants()

    tail = jnp.concatenate([jnp.zeros((PAD, d), x.dtype), meta_tokens.astype(x.dtype)], axis=0)
    hp = jnp.concatenate([x, jnp.broadcast_to(tail[None], (batch, BLK, d))], axis=1).reshape(batch * lp, d)

    c_ba = 3 * WIDTH + 3 * WIDTH + WIDTH
    w_all = jnp.concatenate([w_in[:, :c_ba], w_in[:, c_ba + 2 * HEADS:], w_in[:, c_ba:c_ba + 2 * HEADS],
                             jnp.zeros((d, 128 - 2 * HEADS), w_in.dtype)], axis=1).astype(BF16)
    gq = (jnp.tile(sb_q_norm_g.astype(F32), HEADS) * (HEAD_DIM ** -0.5))[None]
    gk = jnp.tile(sb_k_norm_g.astype(F32), HEADS)[None]
    lane_pad = lambda t: jnp.pad(t.astype(F32), (HEADS, 128 - 2 * HEADS))[None]
    acoef = lane_pad(-jnp.exp(dn_a_log.astype(F32)))
    dtb = lane_pad(dn_dt_bias)

    sbqkv, dnqkv, z, gates, bg = _inproj(hp, norm1_g.astype(F32)[None], w_all, ph, gq, gk, acoef, dtb)
    o_sb = _sb_attn(sbqkv, tri, batch, lp)
    gn = jnp.tile(dn_out_norm_g.astype(F32), 2)[None]
    o_dn = _deltanet(dnqkv, z, bg, dn_conv_w.astype(F32), gn, batch, lp)
    h1 = _merge(o_sb, o_dn, gates, x.reshape(batch * seq, d), w_sb_out.astype(BF16),
                w_dn_out.astype(BF16), w_o.astype(BF16), batch, seq, lp)
    xn2, idx, gate = _route(h1, norm2_g.astype(F32)[None], peer_w_q.astype(BF16), peer_sub_keys)
    h2 = _peer(idx, gate, xn2, h1, peer_u, peer_v)
    return h2.reshape(batch, seq, d)


def kernel(x, meta_tokens, norm1_g, w_in, sb_q_norm_g, sb_k_norm_g, dn_conv_w, dn_a_log, dn_dt_bias,
           dn_out_norm_g, w_sb_out, w_dn_out, w_o, norm2_g, peer_w_q, peer_sub_keys, peer_u, peer_v):
    assert norm1_g.shape[0] == 1, "one layer"
    return _layer(x, meta_tokens, norm1_g[0], w_in[0], sb_q_norm_g[0], sb_k_norm_g[0], dn_conv_w[0],
                  dn_a_log[0], dn_dt_bias[0], dn_out_norm_g[0], w_sb_out[0], w_dn_out[0], w_o[0],
                  norm2_g[0], peer_w_q[0], peer_sub_keys[0], peer_u[0], peer_v[0])
```
